```python
import jax, jax.numpy as jnp
from jax import lax
import numpy as np

D_MODEL = 4096
BATCH = 2
SEQ = 8192
DEPTH = 2

CHUNK = 64
Q_BLOCK = 128
HEAD_DIM = 128
FOX_HEADS = 8
FOX_WIDTH = FOX_HEADS * HEAD_DIM
HGRN_HEADS = 8
HGRN_KEY = 128
HGRN_VAL = 128
HGRN_KWIDTH = HGRN_HEADS * HGRN_KEY
HGRN_VWIDTH = HGRN_HEADS * HGRN_VAL
CONV_WIDTH = 1024
CONV_K = 3
N_BRANCH = 3
D_FF = 11008
N_EXPERTS = 8
TOP_K = 2
D_FF_EXPERT = 2816
N_DENSE = (DEPTH + 1) // 2
N_MOE = DEPTH // 2
EPS = 1e-6
NEG_INF = -1e30

IN_WIDTHS = (FOX_WIDTH, FOX_WIDTH, FOX_WIDTH, FOX_HEADS,
             HGRN_KWIDTH, HGRN_KWIDTH, HGRN_VWIDTH, HGRN_VWIDTH,
             CONV_WIDTH, CONV_WIDTH, CONV_WIDTH,
             N_BRANCH * D_MODEL)
IN_SPLITS = tuple(int(s) for s in np.cumsum(IN_WIDTHS)[:-1])
N_IN = int(sum(IN_WIDTHS))

kernel_name = "hybrid_hgrn2_conv_fox_moe_encoder"


def rmsnorm(x, g):
    xf = x.astype(jnp.float32)
    y = xf * lax.rsqrt(jnp.mean(xf * xf, axis=-1, keepdims=True) + EPS)
    return (y * g.astype(jnp.float32)).astype(x.dtype)


def forgetting_attention(q, k, v, f_logit):
    B, S, H, Dh = q.shape
    nb = S // Q_BLOCK
    qf = q.astype(jnp.float32) * (Dh ** -0.5)
    kf = k.astype(jnp.float32)
    vf = v.astype(jnp.float32)
    cum = jnp.cumsum(jax.nn.log_sigmoid(f_logit.astype(jnp.float32)), axis=1)
    cum_k = cum.transpose(0, 2, 1)
    q_blocks = qf.reshape(B, nb, Q_BLOCK, H, Dh).transpose(1, 0, 2, 3, 4)
    c_blocks = cum.reshape(B, nb, Q_BLOCK, H).transpose(1, 0, 3, 2)
    starts = jnp.arange(nb, dtype=jnp.int32) * Q_BLOCK
    key_pos = jnp.arange(S, dtype=jnp.int32)

    def one_block(args):
        qb, cb, start = args
        s = jnp.einsum('bqhd,bkhd->bhqk', qb, kf)
        s = s + cb[..., None] - cum_k[:, :, None, :]
        q_pos = start + jnp.arange(Q_BLOCK, dtype=jnp.int32)
        mask = key_pos[None, :] <= q_pos[:, None]
        p = jax.nn.softmax(jnp.where(mask, s, NEG_INF), axis=-1)
        return jnp.einsum('bhqk,bkhd->bqhd', p, vf)

    out = lax.map(one_block, (q_blocks, c_blocks, starts))
    return out.transpose(1, 0, 2, 3, 4).reshape(B, S, H * Dh).astype(q.dtype)


def hgrn2_recurrence(q, k, log_f, v):
    B, S, H, Dk = q.shape
    Dv = v.shape[-1]
    nc = S // CHUNK

    def to_chunks(a):
        return a.reshape(B, nc, CHUNK, H, a.shape[-1]).transpose(1, 0, 3, 2, 4)

    tri = jnp.tril(jnp.ones((CHUNK, CHUNK), dtype=bool))

    def step(state, xs):
        qc, kc, gc, vc = xs
        b = jnp.cumsum(gc, axis=2)
        diff = b[:, :, :, None, :] - b[:, :, None, :, :]
        decay = jnp.exp(jnp.where(tri[:, :, None], diff, NEG_INF))
        attn = jnp.einsum('bhtd,bhsd,bhtsd->bhts', qc, kc, decay)
        out = (jnp.einsum('bhts,bhsv->bhtv', attn, vc)
               + jnp.einsum('bhtd,bhdv->bhtv', qc * jnp.exp(b), state))
        b_last = b[:, :, -1:, :]
        new_state = (jnp.exp(b_last[:, :, 0, :])[..., None] * state
                     + jnp.einsum('bhsd,bhsv->bhdv', kc * jnp.exp(b_last - b), vc))
        return new_state, out

    state0 = jnp.zeros((B, H, Dk, Dv), jnp.float32)
    _, out = lax.scan(step, state0, (to_chunks(q), to_chunks(k), to_chunks(log_f), to_chunks(v)))
    return out.transpose(1, 0, 3, 2, 4).reshape(B, S, H, Dv)


def short_gated_conv(h_in, c_gate, b_gate, conv_w, conv_b):
    u = c_gate * h_in
    y = lax.conv_general_dilated(u, conv_w[:, None, :], window_strides=(1,),
                                 padding=[(CONV_K - 1, 0)],
                                 dimension_numbers=('NWC', 'WIO', 'NWC'),
                                 feature_group_count=CONV_WIDTH)
    return b_gate * (y + conv_b)


def hybrid_mixer(h, w_in, b_gate, fox_b_f, lower_bound, hgrn_norm, conv_w, conv_b, w_branch, w_o):
    B, S, D = h.shape
    proj = h @ w_in
    (fq, fk, fv, ff, aq, af, ai, ag, ch, cc, cb, gl) = jnp.split(proj, IN_SPLITS, axis=-1)

    lb = lower_bound.astype(jnp.float32).reshape(HGRN_HEADS, HGRN_KEY)
    f = lb + (1.0 - lb) * jax.nn.sigmoid(af.astype(jnp.float32).reshape(B, S, HGRN_HEADS, HGRN_KEY))
    o = hgrn2_recurrence(aq.astype(jnp.float32).reshape(B, S, HGRN_HEADS, HGRN_KEY),
                         1.0 - f, jnp.log(f),
                         ai.astype(jnp.float32).reshape(B, S, HGRN_HEADS, HGRN_VAL))
    o = o * lax.rsqrt(jnp.mean(o * o, axis=-1, keepdims=True) + EPS)
    y_a = (o.reshape(B, S, HGRN_VWIDTH) * hgrn_norm.astype(jnp.float32)
           * jax.nn.silu(ag.astype(jnp.float32))).astype(h.dtype)

    y_b = short_gated_conv(ch, cc, cb, conv_w, conv_b)

    y_c = forgetting_attention(fq.reshape(B, S, FOX_HEADS, HEAD_DIM),
                               fk.reshape(B, S, FOX_HEADS, HEAD_DIM),
                               fv.reshape(B, S, FOX_HEADS, HEAD_DIM),
                               ff + fox_b_f)

    gates = jax.nn.sigmoid(gl + b_gate).reshape(B, S, N_BRANCH, D)
    merged = gates[:, :, 0] * (y_a @ w_branch[0])
    merged = merged + gates[:, :, 1] * (y_b @ w_branch[1])
    merged = merged + gates[:, :, 2] * (y_c @ w_branch[2])
    return merged @ w_o


def swiglu(h, w1, w3, w2):
    return (jax.nn.silu(h @ w1) * (h @ w3)) @ w2


def moe_swiglu(h, router_w, router_b, w1, w3, w2):
    B, S, D = h.shape
    t = h.reshape(B * S, D)
    logits = (t @ router_w).astype(jnp.float32) + router_b.astype(jnp.float32)
    top_val, top_idx = lax.top_k(logits, TOP_K)
    top_w = jax.nn.softmax(top_val, axis=-1)
    combine = jnp.sum(jax.nn.one_hot(top_idx, N_EXPERTS, dtype=jnp.float32) * top_w[..., None], axis=1)
    out = jnp.zeros_like(t)
    for e in range(N_EXPERTS):
        out = out + combine[:, e:e + 1].astype(t.dtype) * swiglu(t, w1[e], w3[e], w2[e])
    return out.reshape(B, S, D)


def setup_inputs(seed: int = 0) -> dict:
    key = jax.random.key(seed)
    ks = jax.random.split(key, 24)
    f32 = jnp.float32

    def dense(k, shape, fan_in):
        return jax.random.normal(k, shape, f32) * (fan_in ** -0.5)

    def gain(k, shape):
        return 1.0 + 0.05 * jax.random.normal(k, shape, f32)

    def small(k, shape, s=0.01):
        return s * jax.random.normal(k, shape, f32)

    D = D_MODEL
    return {
        "x": jax.random.normal(ks[0], (BATCH, SEQ, D), f32),
        "c": jax.random.normal(ks[1], (BATCH, D), f32),
        "norm_mix": gain(ks[2], (DEPTH, D)),
        "norm_ffn": gain(ks[3], (DEPTH, D)),
        "w_ada": dense(ks[4], (DEPTH, D, 6 * D), D),
        "b_ada": small(ks[5], (DEPTH, 6 * D)),
        "w_in": dense(ks[6], (DEPTH, D, N_IN), D),
        "b_gate": small(ks[7], (DEPTH, N_BRANCH * D)),
        "fox_b_f": 2.0 + 0.1 * jax.random.normal(ks[8], (DEPTH, FOX_HEADS), f32),
        "hgrn_lower_bounds": 0.1 * jax.random.normal(ks[9], (DEPTH, HGRN_KWIDTH), f32),
        "hgrn_norm": gain(ks[10], (DEPTH, HGRN_VWIDTH)),
        "conv_w": dense(ks[11], (DEPTH, CONV_K, CONV_WIDTH), CONV_K),
        "conv_b": small(ks[12], (DEPTH, CONV_WIDTH)),
        "w_branch": dense(ks[13], (DEPTH, N_BRANCH, FOX_WIDTH, D), FOX_WIDTH),
        "w_o": dense(ks[14], (DEPTH, D, D), D),
        "ffn_w1": dense(ks[15], (N_DENSE, D, D_FF), D),
        "ffn_w3": dense(ks[16], (N_DENSE, D, D_FF), D),
        "ffn_w2": dense(ks[17], (N_DENSE, D_FF, D), D_FF),
        "router_w": dense(ks[18], (N_MOE, D, N_EXPERTS), D),
        "router_b": small(ks[19], (N_MOE, N_EXPERTS)),
        "expert_w1": dense(ks[20], (N_MOE, N_EXPERTS, D, D_FF_EXPERT), D),
        "expert_w3": dense(ks[21], (N_MOE, N_EXPERTS, D, D_FF_EXPERT), D),
        "expert_w2": dense(ks[22], (N_MOE, N_EXPERTS, D_FF_EXPERT, D), D_FF_EXPERT),
        "norm_final": gain(ks[23], (D,)),
    }


def reference(x, c, norm_mix, norm_ffn, w_ada, b_ada, w_in, b_gate, fox_b_f,
              hgrn_lower_bounds, hgrn_norm, conv_w, conv_b, w_branch, w_o,
              ffn_w1, ffn_w3, ffn_w2, router_w, router_b,
              expert_w1, expert_w3, expert_w2, norm_final):
    p = jax.nn.softmax(hgrn_lower_bounds.astype(jnp.float32), axis=0)
    lower_bounds = jnp.cumsum(p, axis=0) - p[0]
    c_act = jax.nn.silu(c)
    for layer in range(DEPTH):
        mod = (c_act @ w_ada[layer] + b_ada[layer])[:, None, :]
        sh1, sc1, g1, sh2, sc2, g2 = jnp.split(mod, 6, axis=-1)
        h = rmsnorm(x, norm_mix[layer]) * (1 + sc1) + sh1
        x = x + g1 * hybrid_mixer(h, w_in[layer], b_gate[layer], fox_b_f[layer],
                                  lower_bounds[layer], hgrn_norm[layer], conv_w[layer],
                                  conv_b[layer], w_branch[layer], w_o[layer])
        h = rmsnorm(x, norm_ffn[layer]) * (1 + sc2) + sh2
        if layer % 2 == 0:
            i = layer // 2
            y = swiglu(h, ffn_w1[i], ffn_w3[i], ffn_w2[i])
        else:
            i = layer // 2
            y = moe_swiglu(h, router_w[i], router_b[i], expert_w1[i], expert_w3[i], expert_w2[i])
        x = x + g2 * y
    return rmsnorm(x, norm_final)
```

```python
import functools

import jax
import jax.numpy as jnp
import numpy as np
from jax import lax
from jax.experimental import pallas as pl
from jax.experimental.pallas import tpu as pltpu

F32 = jnp.float32
BF16 = jnp.bfloat16

N_HEADS = 8
HEAD_DIM = 128
WIDTH = N_HEADS * HEAD_DIM
N_BRANCH = 3
TOP_K = 2
EPS = 1e-6
NEG_INF = -1e30
LANES = 128
HGRN_CHUNK = 128
HGRN_DIAG = 16
VMEM_LIMIT = 56 * 1024 * 1024


def _cparams(*sem):
    return pltpu.CompilerParams(dimension_semantics=sem, vmem_limit_bytes=VMEM_LIMIT)


def _tile(n, pref):
    t = min(n, pref)
    while n % t:
        t //= 2
    return t


def _sigmoid(z):
    return 1.0 / (1.0 + jnp.exp(-z))


def _silu(z):
    return z * _sigmoid(z)


def _ada_kernel(c_ref, w_ref, b_ref, o_ref):
    c = c_ref[...]
    ca = _silu(c).astype(BF16)
    w = w_ref[0].astype(BF16)
    o_ref[0] = jnp.dot(ca, w, preferred_element_type=F32) + b_ref[0]


def ada_modulation(c, w_ada, b_ada):
    depth, d, n = w_ada.shape
    b = c.shape[0]
    rows = 8
    c_pad = jnp.zeros((rows, d), F32).at[:b].set(c)
    tn = _tile(n, 512)
    out = pl.pallas_call(
        _ada_kernel,
        grid=(depth, n // tn),
        in_specs=[
            pl.BlockSpec((rows, d), lambda l, j: (0, 0)),
            pl.BlockSpec((1, d, tn), lambda l, j: (l, 0, j)),
            pl.BlockSpec((1, 1, tn), lambda l, j: (l, 0, j)),
        ],
        out_specs=pl.BlockSpec((1, rows, tn), lambda l, j: (l, 0, j)),
        out_shape=jax.ShapeDtypeStruct((depth, rows, n), F32),
        compiler_params=_cparams("parallel", "parallel"),
        name="ada_modulation",
    )(c_pad, w_ada, b_ada.reshape(depth, 1, n))
    return out[:, :b].reshape(depth, b, 6, d)


def _norm_mod(x_ref, g_ref, mod_ref, shift_row, scale_row):
    x = x_ref[...]
    y = x * lax.rsqrt(jnp.mean(x * x, axis=-1, keepdims=True) + EPS) * g_ref[...]
    return y * (1.0 + mod_ref[0, scale_row:scale_row + 1, :]) + mod_ref[0, shift_row:shift_row + 1, :]


def _log_sigmoid(z):
    return jnp.minimum(z, 0.0) - jnp.log(1.0 + jnp.exp(-jnp.abs(z)))


def _norm_forget_kernel(x_ref, g_ref, mod_ref, wf_ref, bf_ref, h_ref, lsf_ref, *, shift_row, scale_row):
    h = _norm_mod(x_ref, g_ref, mod_ref, shift_row, scale_row).astype(BF16)
    h_ref[...] = h
    z = jnp.dot(h, wf_ref[...], preferred_element_type=F32) + bf_ref[...]
    lsf_ref[...] = _log_sigmoid(z)


def _split_bf16(v):
    hi = v.astype(BF16)
    lo = (v - hi.astype(F32)).astype(BF16)
    return hi, lo


def _norm_router_kernel(x_ref, g_ref, mod_ref, wr_ref, br_ref, h_ref, comb_ref, *,
                        shift_row, scale_row, n_experts):
    h = _norm_mod(x_ref, g_ref, mod_ref, shift_row, scale_row)
    h_ref[...] = h.astype(BF16)
    h_hi, h_lo = _split_bf16(h)
    w_hi, w_lo = _split_bf16(wr_ref[...])
    logits = (jnp.dot(h_hi, w_hi, preferred_element_type=F32)
              + jnp.dot(h_hi, w_lo, preferred_element_type=F32)
              + jnp.dot(h_lo, w_hi, preferred_element_type=F32)) + br_ref[...]
    lane = lax.broadcasted_iota(jnp.int32, logits.shape, 1)
    lg = jnp.where(lane < n_experts, logits, -jnp.inf)
    m1 = jnp.max(lg, axis=1, keepdims=True)
    i1 = jnp.min(jnp.where(lg == m1, lane, LANES), axis=1, keepdims=True)
    lg2 = jnp.where(lane == i1, -jnp.inf, lg)
    m2 = jnp.max(lg2, axis=1, keepdims=True)
    i2 = jnp.min(jnp.where(lg2 == m2, lane, LANES), axis=1, keepdims=True)
    e = jnp.exp(m2 - m1)
    w1 = 1.0 / (1.0 + e)
    w2 = e / (1.0 + e)
    comb_ref[...] = jnp.where(lane == i1, w1, 0.0) + jnp.where(lane == i2, w2, 0.0)


def _norm_only_kernel(x_ref, g_ref, mod_ref, h_ref, *, shift_row, scale_row):
    h_ref[...] = _norm_mod(x_ref, g_ref, mod_ref, shift_row, scale_row).astype(BF16)


def _norm_call(body, x, gain, mod, seq, extra_in, extra_specs, extra_out, extra_out_specs, tm):
    t, d = x.shape
    per_seq = seq // tm
    in_specs = [
        pl.BlockSpec((tm, d), lambda i: (i, 0)),
        pl.BlockSpec((1, d), lambda i: (0, 0)),
        pl.BlockSpec((1, 6, d), lambda i: (i // per_seq, 0, 0)),
    ] + extra_specs
    out_shape = [jax.ShapeDtypeStruct((t, d), BF16)] + extra_out
    out_specs = [pl.BlockSpec((tm, d), lambda i: (i, 0))] + extra_out_specs
    return pl.pallas_call(
        body,
        grid=(t // tm,),
        in_specs=in_specs,
        out_specs=out_specs,
        out_shape=out_shape,
        compiler_params=_cparams("parallel"),
        name="norm_mod",
    )(x, gain.reshape(1, d), mod, *extra_in)


def norm_forget(x, gain, mod, seq, w_f, b_f):
    t, d = x.shape
    tm = _tile(seq, 256)
    body = functools.partial(_norm_forget_kernel, shift_row=0, scale_row=1)
    return _norm_call(
        body, x, gain, mod, seq, [w_f, b_f],
        [pl.BlockSpec((d, LANES), lambda i: (0, 0)), pl.BlockSpec((1, LANES), lambda i: (0, 0))],
        [jax.ShapeDtypeStruct((t, LANES), F32)], [pl.BlockSpec((tm, LANES), lambda i: (i, 0))], tm)


def norm_router(x, gain, mod, seq, w_r, b_r, n_experts):
    t, d = x.shape
    tm = _tile(seq, 256)
    body = functools.partial(_norm_router_kernel, shift_row=3, scale_row=4, n_experts=n_experts)
    return _norm_call(
        body, x, gain, mod, seq, [w_r, b_r],
        [pl.BlockSpec((d, LANES), lambda i: (0, 0)), pl.BlockSpec((1, LANES), lambda i: (0, 0))],
        [jax.ShapeDtypeStruct((t, LANES), F32)], [pl.BlockSpec((tm, LANES), lambda i: (i, 0))], tm)


def norm_only(x, gain, mod, seq):
    tm = _tile(seq, 256)
    body = functools.partial(_norm_only_kernel, shift_row=3, scale_row=4)
    return _norm_call(body, x, gain, mod, seq, [], [], [], [], tm)[0]


def _final_norm_kernel(x_ref, g_ref, o_ref):
    x = x_ref[...]
    o_ref[...] = x * lax.rsqrt(jnp.mean(x * x, axis=-1, keepdims=True) + EPS) * g_ref[...]


def final_norm(x, gain):
    t, d = x.shape
    tm = _tile(t, 256)
    return pl.pallas_call(
        _final_norm_kernel,
        grid=(t // tm,),
        in_specs=[pl.BlockSpec((tm, d), lambda i: (i, 0)), pl.BlockSpec((1, d), lambda i: (0, 0))],
        out_specs=pl.BlockSpec((tm, d), lambda i: (i, 0)),
        out_shape=jax.ShapeDtypeStruct((t, d), F32),
        compiler_params=_cparams("parallel"),
        name="final_norm",
    )(x, gain.reshape(1, d))


def _proj_kernel(a_ref, w_ref, o_ref, *, scale):
    acc = jnp.dot(a_ref[...], w_ref[...], preferred_element_type=F32)
    if scale != 1.0:
        acc = acc * scale
    o_ref[...] = acc.astype(o_ref.dtype)


def _gate_proj_kernel(a_ref, w_ref, b_ref, o_ref):
    acc = jnp.dot(a_ref[...], w_ref[...], preferred_element_type=F32)
    o_ref[...] = _sigmoid(acc + b_ref[...]).astype(o_ref.dtype)


def _mm_tiles(m, n, seq):
    return _tile(seq, 1024), _tile(n, 1024)


def project(a, w, seq, scale=1.0):
    m, k = a.shape
    n = w.shape[1]
    tm, tn = _mm_tiles(m, n, seq)
    return pl.pallas_call(
        functools.partial(_proj_kernel, scale=scale),
        grid=(m // tm, n // tn),
        in_specs=[pl.BlockSpec((tm, k), lambda i, j: (i, 0)), pl.BlockSpec((k, tn), lambda i, j: (0, j))],
        out_specs=pl.BlockSpec((tm, tn), lambda i, j: (i, j)),
        out_shape=jax.ShapeDtypeStruct((m, n), BF16),
        compiler_params=_cparams("parallel", "parallel"),
        name="project",
    )(a, w)


def gate_project(a, w, bias, seq):
    m, k = a.shape
    n = w.shape[1]
    tm, tn = _mm_tiles(m, n, seq)
    return pl.pallas_call(
        _gate_proj_kernel,
        grid=(m // tm, n // tn),
        in_specs=[pl.BlockSpec((tm, k), lambda i, j: (i, 0)), pl.BlockSpec((k, tn), lambda i, j: (0, j)),
                  pl.BlockSpec((1, tn), lambda i, j: (0, j))],
        out_specs=pl.BlockSpec((tm, tn), lambda i, j: (i, j)),
        out_shape=jax.ShapeDtypeStruct((m, n), BF16),
        compiler_params=_cparams("parallel", "parallel"),
        name="gate_project",
    )(a, w, bias.reshape(1, n))


def _residual_kernel(a_ref, w_ref, x_ref, mod_ref, o_ref, acc_ref, *, gate_row):
    k = pl.program_id(2)

    @pl.when(k == 0)
    def _():
        acc_ref[...] = jnp.zeros_like(acc_ref)

    acc_ref[...] += jnp.dot(a_ref[...], w_ref[...], preferred_element_type=F32)

    @pl.when(k == pl.num_programs(2) - 1)
    def _():
        o_ref[...] = x_ref[...] + mod_ref[0, gate_row:gate_row + 1, :] * acc_ref[...]


def residual_project(a, w, x, mod, seq, gate_row, tk_pref):
    m, kdim = a.shape
    n = w.shape[1]
    tm, tn = _mm_tiles(m, n, seq)
    tk = _tile(kdim, tk_pref)
    per_seq = seq // tm
    return pl.pallas_call(
        functools.partial(_residual_kernel, gate_row=gate_row),
        grid=(m // tm, n // tn, kdim // tk),
        in_specs=[
            pl.BlockSpec((tm, tk), lambda i, j, k: (i, k)),
            pl.BlockSpec((tk, tn), lambda i, j, k: (k, j)),
            pl.BlockSpec((tm, tn), lambda i, j, k: (i, j)),
            pl.BlockSpec((1, 6, tn), lambda i, j, k: (i // per_seq, 0, j)),
        ],
        out_specs=pl.BlockSpec((tm, tn), lambda i, j, k: (i, j)),
        out_shape=jax.ShapeDtypeStruct((m, n), F32),
        scratch_shapes=[pltpu.VMEM((tm, tn), F32)],
        compiler_params=_cparams("parallel", "parallel", "arbitrary"),
        name="residual_project",
    )(a, w, x, mod)


def _glu_kernel(h_ref, w1_ref, w3_ref, o_ref):
    h = h_ref[...]
    a = jnp.dot(h, w1_ref[...], preferred_element_type=F32)
    b = jnp.dot(h, w3_ref[...], preferred_element_type=F32)
    o_ref[...] = (_silu(a) * b).astype(o_ref.dtype)


def _glu_expert_kernel(h_ref, w1_ref, w3_ref, comb_ref, o_ref, *, tiles_per_expert):
    h = h_ref[...]
    a = jnp.dot(h, w1_ref[...], preferred_element_type=F32)
    b = jnp.dot(h, w3_ref[...], preferred_element_type=F32)
    e = pl.program_id(1) // tiles_per_expert
    comb = comb_ref[...]
    lane = lax.broadcasted_iota(jnp.int32, comb.shape, 1)
    cw = jnp.sum(jnp.where(lane == e, comb, 0.0), axis=1, keepdims=True)
    o_ref[...] = (cw * (_silu(a) * b)).astype(o_ref.dtype)


def glu(h, w1, w3, seq, comb=None, expert_width=None):
    m, k = h.shape
    n = w1.shape[1]
    tm = _tile(seq, 1024)
    tn = _tile(n if comb is None else expert_width, 512)
    in_specs = [pl.BlockSpec((tm, k), lambda i, j: (i, 0)),
                pl.BlockSpec((k, tn), lambda i, j: (0, j)),
                pl.BlockSpec((k, tn), lambda i, j: (0, j))]
    args = [h, w1, w3]
    if comb is None:
        body = _glu_kernel
    else:
        body = functools.partial(_glu_expert_kernel, tiles_per_expert=expert_width // tn)
        in_specs.append(pl.BlockSpec((tm, LANES), lambda i, j: (i, 0)))
        args.append(comb)
    return pl.pallas_call(
        body,
        grid=(m // tm, n // tn),
        in_specs=in_specs,
        out_specs=pl.BlockSpec((tm, tn), lambda i, j: (i, j)),
        out_shape=jax.ShapeDtypeStruct((m, n), BF16),
        compiler_params=_cparams("parallel", "parallel"),
        name="glu",
    )(*args)


def _merge_kernel(ya_ref, yb_ref, yc_ref, wa_ref, wb_ref, wc_ref, ga_ref, gb_ref, gc_ref, o_ref):
    out = ga_ref[...].astype(F32) * jnp.dot(ya_ref[...], wa_ref[0], preferred_element_type=F32)
    out += gb_ref[...].astype(F32) * jnp.dot(yb_ref[...], wb_ref[0], preferred_element_type=F32)
    out += gc_ref[...].astype(F32) * jnp.dot(yc_ref[...], wc_ref[0], preferred_element_type=F32)
    o_ref[...] = out.astype(o_ref.dtype)


def merge_branches(y_a, y_b, y_c, w_branch, gates, seq):
    m, k = y_a.shape
    n = w_branch.shape[2]
    tm = _tile(seq, 512)
    tn = _tile(n, 1024)
    nj = n // tn
    y_spec = pl.BlockSpec((tm, k), lambda i, j: (i, 0))

    def w_spec(b):
        return pl.BlockSpec((1, k, tn), lambda i, j: (b, 0, j))

    def g_spec(b):
        return pl.BlockSpec((tm, tn), lambda i, j: (i, b * nj + j))

    return pl.pallas_call(
        _merge_kernel,
        grid=(m // tm, nj),
        in_specs=[y_spec, y_spec, y_spec, w_spec(0), w_spec(1), w_spec(2), g_spec(0), g_spec(1), g_spec(2)],
        out_specs=pl.BlockSpec((tm, tn), lambda i, j: (i, j)),
        out_shape=jax.ShapeDtypeStruct((m, n), BF16),
        compiler_params=_cparams("parallel", "parallel"),
        name="merge_branches",
    )(y_a, y_b, y_c, w_branch, w_branch, w_branch, gates, gates, gates)


def _conv_kernel(h_ref, c_ref, b_ref, hp_ref, cp_ref, w_ref, bias_ref, o_ref, *, blocks_per_seq):
    u = c_ref[...].astype(F32) * h_ref[...].astype(F32)
    tm = u.shape[0]
    halo = hp_ref.shape[0]
    up = cp_ref[...].astype(F32) * hp_ref[...].astype(F32)
    first = (pl.program_id(0) % blocks_per_seq) == 0
    up = jnp.where(first, 0.0, up)
    p1 = up[halo - 1:halo, :]
    p2 = up[halo - 2:halo - 1, :]
    row = lax.broadcasted_iota(jnp.int32, (tm, 1), 0)
    u1 = jnp.where(row == 0, p1, pltpu.roll(u, 1, 0))
    u2 = jnp.where(row == 0, p2, jnp.where(row == 1, p1, pltpu.roll(u, 2, 0)))
    y = w_ref[0:1, :] * u2 + w_ref[1:2, :] * u1 + w_ref[2:3, :] * u + bias_ref[...]
    o_ref[...] = (b_ref[...].astype(F32) * y).astype(o_ref.dtype)


def short_conv(p, conv_w, conv_b, seq):
    t = p.shape[0]
    w = conv_w.shape[1]
    tm = _tile(seq, 512)
    halo = 16
    ratio = tm // halo
    prev = lambda i: jnp.maximum(i * ratio - 1, 0)
    return pl.pallas_call(
        functools.partial(_conv_kernel, blocks_per_seq=seq // tm),
        grid=(t // tm,),
        in_specs=[
            pl.BlockSpec((tm, w), lambda i: (i, 0)),
            pl.BlockSpec((tm, w), lambda i: (i, 1)),
            pl.BlockSpec((tm, w), lambda i: (i, 2)),
            pl.BlockSpec((halo, w), lambda i: (prev(i), 0)),
            pl.BlockSpec((halo, w), lambda i: (prev(i), 1)),
            pl.BlockSpec((3, w), lambda i: (0, 0)),
            pl.BlockSpec((1, w), lambda i: (0, 0)),
        ],
        out_specs=pl.BlockSpec((tm, w), lambda i: (i, 0)),
        out_shape=jax.ShapeDtypeStruct((t, w), BF16),
        compiler_params=_cparams("parallel"),
        name="short_conv",
    )(p, p, p, p, p, conv_w, conv_b.reshape(1, w))


def _cumsum_kernel(x_ref, cum_ref, cumt_ref, carry_ref):
    @pl.when(pl.program_id(1) == 0)
    def _():
        carry_ref[...] = jnp.zeros_like(carry_ref)

    x = x_ref[...]
    n = x.shape[0]
    r = lax.broadcasted_iota(jnp.int32, (n, n), 0)
    c = lax.broadcasted_iota(jnp.int32, (n, n), 1)
    tri = (r >= c).astype(BF16)
    x1 = x.astype(BF16)
    r1 = x - x1.astype(F32)
    x2 = r1.astype(BF16)
    x3 = (r1 - x2.astype(F32)).astype(BF16)
    cum = (jnp.dot(tri, x1, preferred_element_type=F32) + jnp.dot(tri, x2, preferred_element_type=F32)
           + jnp.dot(tri, x3, preferred_element_type=F32)) + carry_ref[...]
    cum_ref[...] = cum
    cumt_ref[0] = cum.T[:cumt_ref.shape[1], :]
    carry_ref[...] = cum[n - 1:n, :]


def forget_cumsum(lsf, batch, seq):
    blk = _tile(seq, 256)
    nb = seq // blk
    return pl.pallas_call(
        _cumsum_kernel,
        grid=(batch, nb),
        in_specs=[pl.BlockSpec((blk, LANES), lambda b, i: (b * nb + i, 0))],
        out_specs=[pl.BlockSpec((blk, LANES), lambda b, i: (b * nb + i, 0)),
                   pl.BlockSpec((1, N_HEADS, blk), lambda b, i: (b, 0, i))],
        out_shape=[jax.ShapeDtypeStruct((batch * seq, LANES), F32),
                   jax.ShapeDtypeStruct((batch, N_HEADS, seq), F32)],
        scratch_shapes=[pltpu.VMEM((1, LANES), F32)],
        compiler_params=_cparams("parallel", "arbitrary"),
        name="forget_cumsum",
    )(lsf)


def _fox_kernel(qi_ref, kj_ref, q_ref, k_ref, v_ref, cq_ref, ck_ref, o_ref, m_ref, l_ref, acc_ref, cqh_ref):
    h = pl.program_id(1)
    p = pl.program_id(2)
    i = qi_ref[p]
    j = kj_ref[p]

    @pl.when(j == 0)
    def _():
        m_ref[...] = jnp.full_like(m_ref, NEG_INF)
        l_ref[...] = jnp.zeros_like(l_ref)
        acc_ref[...] = jnp.zeros_like(acc_ref)
        cq = cq_ref[...]
        lane = lax.broadcasted_iota(jnp.int32, cq.shape, 1)
        cqh_ref[...] = jnp.sum(jnp.where(lane == h, cq, 0.0), axis=1, keepdims=True)

    s = lax.dot_general(q_ref[...], k_ref[...], (((1,), (1,)), ((), ())), preferred_element_type=F32)
    s = s + cqh_ref[...] - ck_ref[0, pl.ds(h, 1), :]
    tq, tk = s.shape
    row = lax.broadcasted_iota(jnp.int32, (tq, tk), 0) + i * tq
    col = lax.broadcasted_iota(jnp.int32, (tq, tk), 1) + j * tk
    s = jnp.where(col <= row, s, NEG_INF)
    m_old = m_ref[...]
    m_new = jnp.maximum(m_old, jnp.max(s, axis=1, keepdims=True))
    alpha = jnp.exp(m_old - m_new)
    pr = jnp.exp(s - m_new)
    l_ref[...] = alpha * l_ref[...] + jnp.sum(pr, axis=1, keepdims=True)
    acc_ref[...] = alpha * acc_ref[...] + jnp.dot(pr.astype(BF16), v_ref[...], preferred_element_type=F32)
    m_ref[...] = m_new

    @pl.when(j == ((i + 1) * tq - 1) // tk)
    def _():
        o_ref[...] = (acc_ref[...] / l_ref[...]).astype(o_ref.dtype)


def fox_attention(q, kv, cum, cum_t, batch, seq):
    tq = _tile(seq, 512)
    tk = _tile(seq, 512)
    nq, nk = seq // tq, seq // tk
    pairs = [(i, j) for i in range(nq) for j in range(nk) if j * tk <= (i + 1) * tq - 1]
    qi = jnp.asarray(np.array([p[0] for p in pairs], np.int32))
    kj = jnp.asarray(np.array([p[1] for p in pairs], np.int32))
    grid_spec = pltpu.PrefetchScalarGridSpec(
        num_scalar_prefetch=2,
        grid=(batch, N_HEADS, len(pairs)),
        in_specs=[
            pl.BlockSpec((tq, HEAD_DIM), lambda b, h, p, qi, kj: (b * nq + qi[p], h)),
            pl.BlockSpec((tk, HEAD_DIM), lambda b, h, p, qi, kj: (b * nk + kj[p], h)),
            pl.BlockSpec((tk, HEAD_DIM), lambda b, h, p, qi, kj: (b * nk + kj[p], N_HEADS + h)),
            pl.BlockSpec((tq, LANES), lambda b, h, p, qi, kj: (b * nq + qi[p], 0)),
            pl.BlockSpec((1, N_HEADS, tk), lambda b, h, p, qi, kj: (b, 0, kj[p])),
        ],
        out_specs=pl.BlockSpec((tq, HEAD_DIM), lambda b, h, p, qi, kj: (b * nq + qi[p], h)),
        scratch_shapes=[pltpu.VMEM((tq, 1), F32), pltpu.VMEM((tq, 1), F32),
                        pltpu.VMEM((tq, HEAD_DIM), F32), pltpu.VMEM((tq, 1), F32)],
    )
    return pl.pallas_call(
        _fox_kernel,
        grid_spec=grid_spec,
        out_shape=jax.ShapeDtypeStruct((batch * seq, WIDTH), BF16),
        compiler_params=_cparams("parallel", "parallel", "arbitrary"),
        name="fox_attention",
    )(qi, kj, q, kv, kv, cum, cum_t)


def _hgrn_levels():
    sizes = []
    half = HGRN_CHUNK // 2
    while half >= HGRN_DIAG:
        sizes.append(half)
        half //= 2
    return sizes


def _hgrn_coefficients():
    c = HGRN_CHUNK
    t = np.arange(c)[:, None]
    u = np.arange(c)[None, :]
    slabs = [(u <= t), (u > t)]
    for size in _hgrn_levels():
        ref = (t // (2 * size)) * (2 * size) + size - 1
        upper = (t % (2 * size)) >= size
        slabs.append(np.where(upper, (u > ref) & (u <= t), (u > t) & (u <= ref)))
    slabs.append((u <= t) & (u // HGRN_DIAG == t // HGRN_DIAG))
    return np.concatenate(slabs, axis=0).astype(np.float32)


def _hgrn_level_masks():
    c = HGRN_CHUNK
    t = np.arange(c)[:, None]
    s = np.arange(c)[None, :]
    masks = []
    for size in _hgrn_levels():
        same = (t // (2 * size)) == (s // (2 * size))
        masks.append(same & ((t % (2 * size)) >= size) & ((s % (2 * size)) < size))
    return np.stack(masks).astype(np.float32)


def _hgrn_kernel(q_ref, f_ref, i_ref, g_ref, lb_ref, norm_ref, coef_ref, mask_ref, o_ref,
                 st_ref, e_ref, kpad_ref, bpad_ref, *, layer):
    c = HGRN_CHUNK
    pad = HGRN_DIAG
    n_lev = len(_hgrn_levels())

    @pl.when(pl.program_id(1) == 0)
    def _():
        st_ref[...] = jnp.zeros_like(st_ref)
        kpad_ref[0:pad, :] = jnp.zeros((pad, WIDTH), F32)
        bpad_ref[0:pad, :] = jnp.zeros((pad, WIDTH), F32)

    lbr = lb_ref[...]
    le = jnp.exp(lbr - jnp.max(lbr, axis=0, keepdims=True))
    lp = le / jnp.sum(le, axis=0, keepdims=True)
    lb = jnp.zeros((1, WIDTH), F32)
    for r in range(1, layer + 1):
        lb = lb + lp[r:r + 1, :]

    f = lb + (1.0 - lb) * _sigmoid(f_ref[...].astype(F32))
    g = jnp.log(f)
    kpad_ref[pad:pad + c, :] = 1.0 - f
    g1 = g.astype(BF16)
    r1 = g - g1.astype(F32)
    g2 = r1.astype(BF16)
    g3 = (r1 - g2.astype(F32)).astype(BF16)
    coef = coef_ref[...]
    e_ref[...] = (jnp.dot(coef, g1, preferred_element_type=F32) + jnp.dot(coef, g2, preferred_element_type=F32)
                  + jnp.dot(coef, g3, preferred_element_type=F32))
    bpad_ref[pad:pad + c, :] = e_ref[(2 + n_lev) * c:(3 + n_lev) * c, :]

    row = lax.broadcasted_iota(jnp.int32, (c, c), 0)
    col = lax.broadcasted_iota(jnp.int32, (c, c), 1)
    sub = lax.broadcasted_iota(jnp.int32, (c, 1), 0) % pad

    def head(h, carry):
        lanes = pl.ds(pl.multiple_of(h * HEAD_DIM, HEAD_DIM), HEAD_DIM)
        q = q_ref[:, lanes].astype(F32)
        k = kpad_ref[pad:pad + c, lanes]
        v = i_ref[:, lanes]
        st = st_ref[h]
        qe = (q * jnp.exp(e_ref[0:c, lanes])).astype(BF16)
        out = lax.dot_general(qe, st.astype(BF16), (((1,), (1,)), ((), ())), preferred_element_type=F32)
        attn = jnp.zeros((c, c), F32)
        for lev in range(n_lev):
            pw = jnp.exp(e_ref[(2 + lev) * c:(3 + lev) * c, lanes])
            a = lax.dot_general((q * pw).astype(BF16), (k * pw).astype(BF16), (((1,), (1,)), ((), ())),
                                preferred_element_type=F32)
            attn = attn + jnp.where(mask_ref[lev] > 0.0, a, 0.0)
        b_in = bpad_ref[pad:pad + c, lanes]
        for off in range(pad):
            k_off = kpad_ref[pad - off:pad - off + c, lanes]
            b_off = bpad_ref[pad - off:pad - off + c, lanes]
            z = q * k_off * jnp.exp(jnp.where(sub >= off, b_in - b_off, NEG_INF))
            a_off = jnp.sum(z, axis=1, keepdims=True)
            attn = attn + jnp.where(col == row - off, a_off, 0.0)
        out = out + jnp.dot(attn.astype(BF16), v, preferred_element_type=F32)
        ke = (k * jnp.exp(e_ref[c:2 * c, lanes])).astype(BF16)
        decay = jnp.exp(e_ref[c - 1:c, lanes])
        st_ref[h] = st * decay + lax.dot_general(v, ke, (((0,), (0,)), ((), ())), preferred_element_type=F32)
        out = out * lax.rsqrt(jnp.mean(out * out, axis=-1, keepdims=True) + EPS)
        out = out * norm_ref[:, lanes] * _silu(g_ref[:, lanes].astype(F32))
        o_ref[:, lanes] = out.astype(o_ref.dtype)
        return carry

    lax.fori_loop(0, N_HEADS, head, 0)


def hgrn_mixer(p, lower_bounds, norm, layer, batch, seq):
    c = HGRN_CHUNK
    nc = seq // c
    depth = lower_bounds.shape[0]
    coef = jnp.asarray(_hgrn_coefficients(), BF16)
    masks = jnp.asarray(_hgrn_level_masks(), F32)
    n_slab = coef.shape[0] // c

    def col_spec(j):
        return pl.BlockSpec((c, WIDTH), lambda b, n: (b * nc + n, j))

    return pl.pallas_call(
        functools.partial(_hgrn_kernel, layer=layer),
        grid=(batch, nc),
        in_specs=[col_spec(0), col_spec(1), col_spec(2), col_spec(3),
                  pl.BlockSpec((depth, WIDTH), lambda b, n: (0, 0)),
                  pl.BlockSpec((1, WIDTH), lambda b, n: (0, 0)),
                  pl.BlockSpec((n_slab * c, c), lambda b, n: (0, 0)),
                  pl.BlockSpec((masks.shape[0], c, c), lambda b, n: (0, 0, 0))],
        out_specs=pl.BlockSpec((c, WIDTH), lambda b, n: (b * nc + n, 0)),
        out_shape=jax.ShapeDtypeStruct((batch * seq, WIDTH), BF16),
        scratch_shapes=[pltpu.VMEM((N_HEADS, HEAD_DIM, HEAD_DIM), F32),
                        pltpu.VMEM((n_slab * c, WIDTH), F32),
                        pltpu.VMEM((HGRN_DIAG + c, WIDTH), F32),
                        pltpu.VMEM((HGRN_DIAG + c, WIDTH), F32)],
        compiler_params=_cparams("parallel", "arbitrary"),
        name="hgrn_mixer",
    )(p, p, p, p, lower_bounds, norm.reshape(1, WIDTH), coef, masks)


def _pad_cols(w, n):
    return jnp.pad(w, ((0, 0), (0, n - w.shape[1])))


def _round_up(n, m):
    return ((n + m - 1) // m) * m


def kernel(x, c, norm_mix, norm_ffn, w_ada, b_ada, w_in, b_gate, fox_b_f, hgrn_lower_bounds, hgrn_norm,
           conv_w, conv_b, w_branch, w_o, ffn_w1, ffn_w3, ffn_w2, router_w, router_b,
           expert_w1, expert_w3, expert_w2, norm_final):
    batch, seq, d = x.shape
    depth = w_ada.shape[0]
    t = batch * seq
    n_experts = router_w.shape[2]

    o_fox = 0
    o_ff = 3 * WIDTH
    o_hgrn = o_ff + N_HEADS
    o_conv = o_hgrn + 4 * WIDTH
    o_gate = o_conv + 3 * WIDTH

    mod_all = ada_modulation(c, w_ada, b_ada)
    xt = x.reshape(t, d)
    for layer in range(depth):
        mod = mod_all[layer]
        wl = w_in[layer]
        w_q = wl[:, o_fox:o_fox + WIDTH].astype(BF16)
        w_kv = wl[:, o_fox + WIDTH:o_ff].astype(BF16)
        w_ff = _pad_cols(wl[:, o_ff:o_hgrn], LANES).astype(BF16)
        w_hgrn = wl[:, o_hgrn:o_conv].astype(BF16)
        w_conv = wl[:, o_conv:o_gate].astype(BF16)
        w_gate = wl[:, o_gate:].astype(BF16)
        b_ff = _pad_cols(fox_b_f[layer].reshape(1, N_HEADS), LANES)

        h, lsf = norm_forget(xt, norm_mix[layer], mod, seq, w_ff, b_ff)
        q = project(h, w_q, seq, scale=HEAD_DIM ** -0.5)
        kv = project(h, w_kv, seq)
        p_hgrn = project(h, w_hgrn, seq)
        p_conv = project(h, w_conv, seq)
        gates = gate_project(h, w_gate, b_gate[layer], seq)

        y_a = hgrn_mixer(p_hgrn, hgrn_lower_bounds, hgrn_norm[layer], layer, batch, seq)
        y_b = short_conv(p_conv, conv_w[layer], conv_b[layer], seq)
        cum, cum_t = forget_cumsum(lsf, batch, seq)
        y_c = fox_attention(q, kv, cum, cum_t, batch, seq)

        merged = merge_branches(y_a, y_b, y_c, w_branch[layer].astype(BF16), gates, seq)
        xt = residual_project(merged, w_o[layer].astype(BF16), xt, mod, seq, 2, d // 2)

        i = layer // 2
        if layer % 2 == 0:
            dff = ffn_w1.shape[2]
            dff_pad = _round_up(dff, 1024)
            w1 = _pad_cols(ffn_w1[i], dff_pad).astype(BF16)
            w3 = _pad_cols(ffn_w3[i], dff_pad).astype(BF16)
            w2 = jnp.pad(ffn_w2[i], ((0, dff_pad - dff), (0, 0))).astype(BF16)
            h2 = norm_only(xt, norm_ffn[layer], mod, seq)
            act = glu(h2, w1, w3, seq)
            xt = residual_project(act, w2, xt, mod, seq, 5, dff_pad // 4)
        else:
            dfe = expert_w1.shape[3]
            w1 = expert_w1[i].transpose(1, 0, 2).reshape(d, n_experts * dfe).astype(BF16)
            w3 = expert_w3[i].transpose(1, 0, 2).reshape(d, n_experts * dfe).astype(BF16)
            w2 = expert_w2[i].reshape(n_experts * dfe, d).astype(BF16)
            w_r = _pad_cols(router_w[i], LANES)
            b_r = _pad_cols(router_b[i].reshape(1, n_experts), LANES)
            h2, comb = norm_router(xt, norm_ffn[layer], mod, seq, w_r, b_r, n_experts)
            act = glu(h2, w1, w3, seq, comb=comb, expert_width=dfe)
            xt = residual_project(act, w2, xt, mod, seq, 5, dfe)
    return final_norm(xt, norm_final).reshape(batch, seq, d)
```

```python
import functools

import jax
import jax.numpy as jnp
import numpy as np
from jax import lax
from jax.experimental import pallas as pl
from jax.experimental.pallas import tpu as pltpu

F32 = jnp.float32
BF16 = jnp.bfloat16

N_HEADS = 8
HEAD_DIM = 128
WIDTH = N_HEADS * HEAD_DIM
N_BRANCH = 3
TOP_K = 2
EPS = 1e-6
NEG_INF = -1e30
LANES = 128
HGRN_CHUNK = 128
HGRN_DIAG = 16
VMEM_LIMIT = 56 * 1024 * 1024


def _cparams(*sem):
    return pltpu.CompilerParams(dimension_semantics=sem, vmem_limit_bytes=VMEM_LIMIT)


def _tile(n, pref):
    t = min(n, pref)
    while n % t:
        t //= 2
    return t


def _sigmoid(z):
    return 1.0 / (1.0 + jnp.exp(-z))


def _silu(z):
    return z * _sigmoid(z)


def _ada_kernel(c_ref, w_ref, b_ref, o_ref):
    c = c_ref[...]
    ca = _silu(c).astype(BF16)
    w = w_ref[0].astype(BF16)
    o_ref[0] = jnp.dot(ca, w, preferred_element_type=F32) + b_ref[0]


def ada_modulation(c, w_ada, b_ada):
    depth, d, n = w_ada.shape
    b = c.shape[0]
    rows = 8
    c_pad = jnp.zeros((rows, d), F32).at[:b].set(c)
    tn = _tile(n, 512)
    out = pl.pallas_call(
        _ada_kernel,
        grid=(depth, n // tn),
        in_specs=[
            pl.BlockSpec((rows, d), lambda l, j: (0, 0)),
            pl.BlockSpec((1, d, tn), lambda l, j: (l, 0, j)),
            pl.BlockSpec((1, 1, tn), lambda l, j: (l, 0, j)),
        ],
        out_specs=pl.BlockSpec((1, rows, tn), lambda l, j: (l, 0, j)),
        out_shape=jax.ShapeDtypeStruct((depth, rows, n), F32),
        compiler_params=_cparams("parallel", "parallel"),
        name="ada_modulation",
    )(c_pad, w_ada, b_ada.reshape(depth, 1, n))
    return out[:, :b].reshape(depth, b, 6, d)


def _norm_mod(x_ref, g_ref, mod_ref, shift_row, scale_row):
    x = x_ref[...]
    y = x * lax.rsqrt(jnp.mean(x * x, axis=-1, keepdims=True) + EPS) * g_ref[...]
    return y * (1.0 + mod_ref[0, scale_row:scale_row + 1, :]) + mod_ref[0, shift_row:shift_row + 1, :]


def _log_sigmoid(z):
    return jnp.minimum(z, 0.0) - jnp.log(1.0 + jnp.exp(-jnp.abs(z)))


def _norm_forget_kernel(x_ref, g_ref, mod_ref, wf_ref, bf_ref, h_ref, lsf_ref, *, shift_row, scale_row):
    h = _norm_mod(x_ref, g_ref, mod_ref, shift_row, scale_row).astype(BF16)
    h_ref[...] = h
    z = jnp.dot(h, wf_ref[...], preferred_element_type=F32) + bf_ref[...]
    lsf_ref[...] = _log_sigmoid(z)


def _split_bf16(v):
    hi = v.astype(BF16)
    lo = (v - hi.astype(F32)).astype(BF16)
    return hi, lo


def _norm_router_kernel(x_ref, g_ref, mod_ref, wr_ref, br_ref, h_ref, route_ref, *,
                        shift_row, scale_row, n_experts):
    h = _norm_mod(x_ref, g_ref, mod_ref, shift_row, scale_row)
    h_ref[...] = h
    h_hi, h_lo = _split_bf16(h)
    w_hi, w_lo = _split_bf16(wr_ref[...])
    logits = (jnp.dot(h_hi, w_hi, preferred_element_type=F32)
              + jnp.dot(h_hi, w_lo, preferred_element_type=F32)
              + jnp.dot(h_lo, w_hi, preferred_element_type=F32)) + br_ref[...]
    lane = lax.broadcasted_iota(jnp.int32, logits.shape, 1)
    lg = jnp.where(lane < n_experts, logits, -jnp.inf)
    m1 = jnp.max(lg, axis=1, keepdims=True)
    i1 = jnp.min(jnp.where(lg == m1, lane, LANES), axis=1, keepdims=True)
    lg2 = jnp.where(lane == i1, -jnp.inf, lg)
    m2 = jnp.max(lg2, axis=1, keepdims=True)
    i2 = jnp.min(jnp.where(lg2 == m2, lane, LANES), axis=1, keepdims=True)
    e = jnp.exp(m2 - m1)
    w1 = 1.0 / (1.0 + e)
    w2 = e / (1.0 + e)
    rec = jnp.where((lane == i1) | (lane == i2), 1.0, 0.0)
    rec = jnp.where(lane == n_experts, w1, rec)
    rec = jnp.where(lane == n_experts + 1, w2, rec)
    rec = jnp.where(lane == n_experts + 2, i1.astype(F32), rec)
    rec = jnp.where(lane == n_experts + 3, i2.astype(F32), rec)
    route_ref[...] = rec


def _norm_only_kernel(x_ref, g_ref, mod_ref, h_ref, *, shift_row, scale_row):
    h_ref[...] = _norm_mod(x_ref, g_ref, mod_ref, shift_row, scale_row).astype(BF16)


def _norm_call(body, x, gain, mod, seq, extra_in, extra_specs, extra_out, extra_out_specs, tm, h_dtype=BF16):
    t, d = x.shape
    per_seq = seq // tm
    in_specs = [
        pl.BlockSpec((tm, d), lambda i: (i, 0)),
        pl.BlockSpec((1, d), lambda i: (0, 0)),
        pl.BlockSpec((1, 6, d), lambda i: (i // per_seq, 0, 0)),
    ] + extra_specs
    out_shape = [jax.ShapeDtypeStruct((t, d), h_dtype)] + extra_out
    out_specs = [pl.BlockSpec((tm, d), lambda i: (i, 0))] + extra_out_specs
    return pl.pallas_call(
        body,
        grid=(t // tm,),
        in_specs=in_specs,
        out_specs=out_specs,
        out_shape=out_shape,
        compiler_params=_cparams("parallel"),
        name="norm_mod",
    )(x, gain.reshape(1, d), mod, *extra_in)


def norm_forget(x, gain, mod, seq, w_f, b_f):
    t, d = x.shape
    tm = _tile(seq, 256)
    body = functools.partial(_norm_forget_kernel, shift_row=0, scale_row=1)
    return _norm_call(
        body, x, gain, mod, seq, [w_f, b_f],
        [pl.BlockSpec((d, LANES), lambda i: (0, 0)), pl.BlockSpec((1, LANES), lambda i: (0, 0))],
        [jax.ShapeDtypeStruct((t, LANES), F32)], [pl.BlockSpec((tm, LANES), lambda i: (i, 0))], tm)


def norm_router(x, gain, mod, seq, w_r, b_r, n_experts):
    t, d = x.shape
    assert n_experts + 4 <= LANES
    tm = _tile(seq, 256)
    body = functools.partial(_norm_router_kernel, shift_row=3, scale_row=4, n_experts=n_experts)
    return _norm_call(
        body, x, gain, mod, seq, [w_r, b_r],
        [pl.BlockSpec((d, LANES), lambda i: (0, 0)), pl.BlockSpec((1, LANES), lambda i: (0, 0))],
        [jax.ShapeDtypeStruct((t, LANES), F32)], [pl.BlockSpec((tm, LANES), lambda i: (i, 0))], tm,
        h_dtype=F32)


def norm_only(x, gain, mod, seq):
    tm = _tile(seq, 256)
    body = functools.partial(_norm_only_kernel, shift_row=3, scale_row=4)
    return _norm_call(body, x, gain, mod, seq, [], [], [], [], tm)[0]


def _final_norm_kernel(x_ref, g_ref, o_ref):
    x = x_ref[...]
    o_ref[...] = x * lax.rsqrt(jnp.mean(x * x, axis=-1, keepdims=True) + EPS) * g_ref[...]


def final_norm(x, gain):
    t, d = x.shape
    tm = _tile(t, 256)
    return pl.pallas_call(
        _final_norm_kernel,
        grid=(t // tm,),
        in_specs=[pl.BlockSpec((tm, d), lambda i: (i, 0)), pl.BlockSpec((1, d), lambda i: (0, 0))],
        out_specs=pl.BlockSpec((tm, d), lambda i: (i, 0)),
        out_shape=jax.ShapeDtypeStruct((t, d), F32),
        compiler_params=_cparams("parallel"),
        name="final_norm",
    )(x, gain.reshape(1, d))


def _proj_kernel(a_ref, w_ref, o_ref, *, scale):
    acc = jnp.dot(a_ref[...], w_ref[...], preferred_element_type=F32)
    if scale != 1.0:
        acc = acc * scale
    o_ref[...] = acc.astype(o_ref.dtype)


def _gate_proj_kernel(a_ref, w_ref, b_ref, o_ref):
    acc = jnp.dot(a_ref[...], w_ref[...], preferred_element_type=F32)
    o_ref[...] = _sigmoid(acc + b_ref[...]).astype(o_ref.dtype)


def _mm_tiles(m, n, seq):
    return _tile(seq, 1024), _tile(n, 1024)


def project(a, w, seq, scale=1.0):
    m, k = a.shape
    n = w.shape[1]
    tm, tn = _mm_tiles(m, n, seq)
    return pl.pallas_call(
        functools.partial(_proj_kernel, scale=scale),
        grid=(m // tm, n // tn),
        in_specs=[pl.BlockSpec((tm, k), lambda i, j: (i, 0)), pl.BlockSpec((k, tn), lambda i, j: (0, j))],
        out_specs=pl.BlockSpec((tm, tn), lambda i, j: (i, j)),
        out_shape=jax.ShapeDtypeStruct((m, n), BF16),
        compiler_params=_cparams("parallel", "parallel"),
        name="project",
    )(a, w)


def gate_project(a, w, bias, seq):
    m, k = a.shape
    n = w.shape[1]
    tm, tn = _mm_tiles(m, n, seq)
    return pl.pallas_call(
        _gate_proj_kernel,
        grid=(m // tm, n // tn),
        in_specs=[pl.BlockSpec((tm, k), lambda i, j: (i, 0)), pl.BlockSpec((k, tn), lambda i, j: (0, j)),
                  pl.BlockSpec((1, tn), lambda i, j: (0, j))],
        out_specs=pl.BlockSpec((tm, tn), lambda i, j: (i, j)),
        out_shape=jax.ShapeDtypeStruct((m, n), BF16),
        compiler_params=_cparams("parallel", "parallel"),
        name="gate_project",
    )(a, w, bias.reshape(1, n))


def _residual_kernel(a_ref, w_ref, x_ref, mod_ref, o_ref, acc_ref, *, gate_row):
    k = pl.program_id(2)

    @pl.when(k == 0)
    def _():
        acc_ref[...] = jnp.zeros_like(acc_ref)

    acc_ref[...] += jnp.dot(a_ref[...], w_ref[...], preferred_element_type=F32)

    @pl.when(k == pl.num_programs(2) - 1)
    def _():
        o_ref[...] = x_ref[...] + mod_ref[0, gate_row:gate_row + 1, :] * acc_ref[...]


def residual_project(a, w, x, mod, seq, gate_row, tk_pref):
    m, kdim = a.shape
    n = w.shape[1]
    tm, tn = _mm_tiles(m, n, seq)
    tk = _tile(kdim, tk_pref)
    per_seq = seq // tm
    return pl.pallas_call(
        functools.partial(_residual_kernel, gate_row=gate_row),
        grid=(m // tm, n // tn, kdim // tk),
        in_specs=[
            pl.BlockSpec((tm, tk), lambda i, j, k: (i, k)),
            pl.BlockSpec((tk, tn), lambda i, j, k: (k, j)),
            pl.BlockSpec((tm, tn), lambda i, j, k: (i, j)),
            pl.BlockSpec((1, 6, tn), lambda i, j, k: (i // per_seq, 0, j)),
        ],
        out_specs=pl.BlockSpec((tm, tn), lambda i, j, k: (i, j)),
        out_shape=jax.ShapeDtypeStruct((m, n), F32),
        scratch_shapes=[pltpu.VMEM((tm, tn), F32)],
        compiler_params=_cparams("parallel", "parallel", "arbitrary"),
        name="residual_project",
    )(a, w, x, mod)


def _glu_kernel(h_ref, w1_ref, w3_ref, o_ref):
    h = h_ref[...]
    a = jnp.dot(h, w1_ref[...], preferred_element_type=F32)
    b = jnp.dot(h, w3_ref[...], preferred_element_type=F32)
    o_ref[...] = (_silu(a) * b).astype(o_ref.dtype)


def glu(h, w1, w3, seq):
    m, k = h.shape
    n = w1.shape[1]
    tm = _tile(seq, 1024)
    tn = _tile(n, 512)
    return pl.pallas_call(
        _glu_kernel,
        grid=(m // tm, n // tn),
        in_specs=[pl.BlockSpec((tm, k), lambda i, j: (i, 0)),
                  pl.BlockSpec((k, tn), lambda i, j: (0, j)),
                  pl.BlockSpec((k, tn), lambda i, j: (0, j))],
        out_specs=pl.BlockSpec((tm, tn), lambda i, j: (i, j)),
        out_shape=jax.ShapeDtypeStruct((m, n), BF16),
        compiler_params=_cparams("parallel", "parallel"),
        name="glu",
    )(h, w1, w3)


MOE_ROW_TILE = 512


def _invert_kernel(pos_ref, src_ref, *, n_tokens):
    def clear(p, carry):
        src_ref[p] = 0
        return carry

    lax.fori_loop(0, src_ref.shape[0], clear, 0, unroll=8)

    def place(t, carry):
        src_ref[pos_ref[t]] = t
        src_ref[pos_ref[n_tokens + t]] = t
        return carry

    lax.fori_loop(0, n_tokens, place, 0, unroll=8)


def invert_positions(pos, n_slots, n_tokens):
    return pl.pallas_call(
        functools.partial(_invert_kernel, n_tokens=n_tokens),
        in_specs=[pl.BlockSpec(memory_space=pltpu.SMEM)],
        out_specs=pl.BlockSpec(memory_space=pltpu.SMEM),
        out_shape=jax.ShapeDtypeStruct((n_slots,), jnp.int32),
        name="invert_positions",
    )(pos)


def _moe_up_kernel(src_ref, texp_ref, nv_ref, h_hbm, w1_ref, w3_ref, o_ref, buf, xs, sem):
    i = pl.program_id(0)
    j = pl.program_id(1)
    nv = nv_ref[0]
    tm = xs.shape[0]
    slot = i % 2

    def row_copy(tile, r, s):
        tok = src_ref[tile * tm + r]
        return pltpu.make_async_copy(h_hbm.at[pl.ds(tok, 1), :], buf.at[s, pl.ds(r, 1), :], sem.at[s])

    def start_gather(tile, s):
        def body(r, carry):
            row_copy(tile, r, s).start()
            return carry
        lax.fori_loop(0, tm, body, 0, unroll=8)

    def wait_gather(tile, s):
        def body(r, carry):
            row_copy(tile, r, s).wait()
            return carry
        lax.fori_loop(0, tm, body, 0, unroll=8)

    @pl.when((j == 0) & (i < nv))
    def _():
        @pl.when(i == 0)
        def _():
            start_gather(0, 0)

        @pl.when(i + 1 < nv)
        def _():
            start_gather(i + 1, 1 - slot)

        wait_gather(i, slot)
        xs[...] = buf[slot].astype(BF16)

    @pl.when(i < nv)
    def _():
        x = xs[...]
        a = jnp.dot(x, w1_ref[0], preferred_element_type=F32)
        b = jnp.dot(x, w3_ref[0], preferred_element_type=F32)
        o_ref[...] = (_silu(a) * b).astype(o_ref.dtype)

    @pl.when(i >= nv)
    def _():
        o_ref[...] = jnp.zeros_like(o_ref)


def moe_up(h, w1, w3, src, tile_expert, n_valid):
    d = h.shape[1]
    f = w1.shape[2]
    tm = MOE_ROW_TILE
    n_tiles = src.shape[0] // tm
    tn = _tile(f, 256)
    nj = f // tn

    def w_map(i, j, src, texp, nv):
        ie = jnp.minimum(i, nv[0] - 1)
        return (texp[ie], 0, jnp.where(i < nv[0], j, nj - 1))

    grid_spec = pltpu.PrefetchScalarGridSpec(
        num_scalar_prefetch=3,
        grid=(n_tiles, nj),
        in_specs=[pl.BlockSpec(memory_space=pl.ANY),
                  pl.BlockSpec((1, d, tn), w_map),
                  pl.BlockSpec((1, d, tn), w_map)],
        out_specs=pl.BlockSpec((tm, tn), lambda i, j, src, texp, nv: (i, j)),
        scratch_shapes=[pltpu.VMEM((2, tm, d), F32), pltpu.VMEM((tm, d), BF16),
                        pltpu.SemaphoreType.DMA((2,))],
    )
    return pl.pallas_call(
        _moe_up_kernel,
        grid_spec=grid_spec,
        out_shape=jax.ShapeDtypeStruct((n_tiles * tm, f), BF16),
        compiler_params=_cparams("arbitrary", "arbitrary"),
        name="moe_up",
    )(src, tile_expert, n_valid, h, w1, w3)


def _moe_down_kernel(texp_ref, nv_ref, a_ref, w_ref, o_ref):
    @pl.when(pl.program_id(0) < nv_ref[0])
    def _():
        o_ref[...] = jnp.dot(a_ref[...], w_ref[0], preferred_element_type=F32)

    @pl.when(pl.program_id(0) >= nv_ref[0])
    def _():
        o_ref[...] = jnp.zeros_like(o_ref)


def moe_down(act, w2, tile_expert, n_valid):
    m, f = act.shape
    d = w2.shape[2]
    tm = MOE_ROW_TILE
    tn = _tile(d, 1024)
    nj = d // tn

    def row(i, nv):
        return jnp.minimum(i, nv[0] - 1)

    def col(i, j, nv):
        return jnp.where(i < nv[0], j, nj - 1)

    grid_spec = pltpu.PrefetchScalarGridSpec(
        num_scalar_prefetch=2,
        grid=(m // tm, nj),
        in_specs=[pl.BlockSpec((tm, f), lambda i, j, texp, nv: (row(i, nv), 0)),
                  pl.BlockSpec((1, f, tn), lambda i, j, texp, nv: (texp[row(i, nv)], 0, col(i, j, nv)))],
        out_specs=pl.BlockSpec((tm, tn), lambda i, j, texp, nv: (i, j)),
    )
    return pl.pallas_call(
        _moe_down_kernel,
        grid_spec=grid_spec,
        out_shape=jax.ShapeDtypeStruct((m, d), F32),
        compiler_params=_cparams("arbitrary", "arbitrary"),
        name="moe_down",
    )(tile_expert, n_valid, act, w2)


def _moe_combine_kernel(pos_ref, y_hbm, x_ref, route_ref, mod_ref, o_ref, buf, sem, *,
                        n_tokens, n_experts, gate_row):
    i = pl.program_id(0)
    tm = x_ref.shape[0]
    slot = i % 2

    def row_copy(tile, r, choice, s):
        p = pos_ref[choice * n_tokens + tile * tm + r]
        return pltpu.make_async_copy(y_hbm.at[pl.ds(p, 1), :], buf.at[s, choice, pl.ds(r, 1), :], sem.at[s])

    def start_gather(tile, s):
        def body(r, carry):
            row_copy(tile, r, 0, s).start()
            row_copy(tile, r, 1, s).start()
            return carry
        lax.fori_loop(0, tm, body, 0, unroll=8)

    def wait_gather(tile, s):
        def body(r, carry):
            row_copy(tile, r, 0, s).wait()
            row_copy(tile, r, 1, s).wait()
            return carry
        lax.fori_loop(0, tm, body, 0, unroll=8)

    @pl.when(i == 0)
    def _():
        start_gather(0, 0)

    @pl.when(i + 1 < pl.num_programs(0))
    def _():
        start_gather(i + 1, 1 - slot)

    wait_gather(i, slot)
    rec = route_ref[...]
    lane = lax.broadcasted_iota(jnp.int32, rec.shape, 1)
    w1 = jnp.sum(jnp.where(lane == n_experts, rec, 0.0), axis=1, keepdims=True)
    w2 = jnp.sum(jnp.where(lane == n_experts + 1, rec, 0.0), axis=1, keepdims=True)
    y = w1 * buf[slot, 0] + w2 * buf[slot, 1]
    o_ref[...] = x_ref[...] + mod_ref[0, gate_row:gate_row + 1, :] * y


def moe_combine(y, pos, x, route, mod, seq, n_experts, gate_row):
    t, d = x.shape
    tm = _tile(seq, 256)
    per_seq = seq // tm
    grid_spec = pltpu.PrefetchScalarGridSpec(
        num_scalar_prefetch=1,
        grid=(t // tm,),
        in_specs=[pl.BlockSpec(memory_space=pl.ANY),
                  pl.BlockSpec((tm, d), lambda i, pos: (i, 0)),
                  pl.BlockSpec((tm, LANES), lambda i, pos: (i, 0)),
                  pl.BlockSpec((1, 6, d), lambda i, pos: (i // per_seq, 0, 0))],
        out_specs=pl.BlockSpec((tm, d), lambda i, pos: (i, 0)),
        scratch_shapes=[pltpu.VMEM((2, TOP_K, tm, d), F32), pltpu.SemaphoreType.DMA((2,))],
    )
    return pl.pallas_call(
        functools.partial(_moe_combine_kernel, n_tokens=t, n_experts=n_experts, gate_row=gate_row),
        grid_spec=grid_spec,
        out_shape=jax.ShapeDtypeStruct((t, d), F32),
        compiler_params=_cparams("arbitrary"),
        name="moe_combine",
    )(pos, y, x, route, mod)


def moe_ffn(x, gain, mod, seq, router_w, router_b, w1, w3, w2, gate_row):
    t, d = x.shape
    n_experts = router_w.shape[1]
    tg = MOE_ROW_TILE
    w_r = _pad_cols(router_w, LANES)
    b_r = _pad_cols(router_b.reshape(1, n_experts), LANES)
    h, route = norm_router(x, gain, mod, seq, w_r, b_r, n_experts)
    cum = token_cumsum(route)
    chosen = route[:, n_experts + 2:n_experts + 4].astype(jnp.int32)
    rank = jnp.take_along_axis(cum[:, :n_experts] - route[:, :n_experts], chosen, axis=1).astype(jnp.int32)
    counts = cum[t - 1, :n_experts].astype(jnp.int32)
    padded = (counts + tg - 1) // tg * tg
    ends = jnp.cumsum(padded)
    pos = (jnp.take(ends - padded, chosen) + rank).T.reshape(-1)
    n_tiles = (TOP_K * t + n_experts * (tg - 1)) // tg
    n_valid = (ends[n_experts - 1] // tg).reshape(1)
    tile_expert = jnp.minimum(jnp.searchsorted(ends, jnp.arange(n_tiles, dtype=jnp.int32) * tg, side="right"),
                              n_experts - 1).astype(jnp.int32)
    src = invert_positions(pos, n_tiles * tg, t)
    act = moe_up(h, w1, w3, src, tile_expert, n_valid)
    y = moe_down(act, w2, tile_expert, n_valid)
    return moe_combine(y, pos, x, route, mod, seq, n_experts, gate_row)


def _merge_kernel(ya_ref, yb_ref, yc_ref, wa_ref, wb_ref, wc_ref, ga_ref, gb_ref, gc_ref, o_ref):
    out = ga_ref[...].astype(F32) * jnp.dot(ya_ref[...], wa_ref[0], preferred_element_type=F32)
    out += gb_ref[...].astype(F32) * jnp.dot(yb_ref[...], wb_ref[0], preferred_element_type=F32)
    out += gc_ref[...].astype(F32) * jnp.dot(yc_ref[...], wc_ref[0], preferred_element_type=F32)
    o_ref[...] = out.astype(o_ref.dtype)


def merge_branches(y_a, y_b, y_c, w_branch, gates, seq):
    m, k = y_a.shape
    n = w_branch.shape[2]
    tm = _tile(seq, 512)
    tn = _tile(n, 1024)
    nj = n // tn
    y_spec = pl.BlockSpec((tm, k), lambda i, j: (i, 0))

    def w_spec(b):
        return pl.BlockSpec((1, k, tn), lambda i, j: (b, 0, j))

    def g_spec(b):
        return pl.BlockSpec((tm, tn), lambda i, j: (i, b * nj + j))

    return pl.pallas_call(
        _merge_kernel,
        grid=(m // tm, nj),
        in_specs=[y_spec, y_spec, y_spec, w_spec(0), w_spec(1), w_spec(2), g_spec(0), g_spec(1), g_spec(2)],
        out_specs=pl.BlockSpec((tm, tn), lambda i, j: (i, j)),
        out_shape=jax.ShapeDtypeStruct((m, n), BF16),
        compiler_params=_cparams("parallel", "parallel"),
        name="merge_branches",
    )(y_a, y_b, y_c, w_branch, w_branch, w_branch, gates, gates, gates)


def _conv_kernel(h_ref, c_ref, b_ref, hp_ref, cp_ref, w_ref, bias_ref, o_ref, *, blocks_per_seq):
    u = c_ref[...].astype(F32) * h_ref[...].astype(F32)
    tm = u.shape[0]
    halo = hp_ref.shape[0]
    up = cp_ref[...].astype(F32) * hp_ref[...].astype(F32)
    first = (pl.program_id(0) % blocks_per_seq) == 0
    up = jnp.where(first, 0.0, up)
    p1 = up[halo - 1:halo, :]
    p2 = up[halo - 2:halo - 1, :]
    row = lax.broadcasted_iota(jnp.int32, (tm, 1), 0)
    u1 = jnp.where(row == 0, p1, pltpu.roll(u, 1, 0))
    u2 = jnp.where(row == 0, p2, jnp.where(row == 1, p1, pltpu.roll(u, 2, 0)))
    y = w_ref[0:1, :] * u2 + w_ref[1:2, :] * u1 + w_ref[2:3, :] * u + bias_ref[...]
    o_ref[...] = (b_ref[...].astype(F32) * y).astype(o_ref.dtype)


def short_conv(p, conv_w, conv_b, seq):
    t = p.shape[0]
    w = conv_w.shape[1]
    tm = _tile(seq, 512)
    halo = 16
    ratio = tm // halo
    prev = lambda i: jnp.maximum(i * ratio - 1, 0)
    return pl.pallas_call(
        functools.partial(_conv_kernel, blocks_per_seq=seq // tm),
        grid=(t // tm,),
        in_specs=[
            pl.BlockSpec((tm, w), lambda i: (i, 0)),
            pl.BlockSpec((tm, w), lambda i: (i, 1)),
            pl.BlockSpec((tm, w), lambda i: (i, 2)),
            pl.BlockSpec((halo, w), lambda i: (prev(i), 0)),
            pl.BlockSpec((halo, w), lambda i: (prev(i), 1)),
            pl.BlockSpec((3, w), lambda i: (0, 0)),
            pl.BlockSpec((1, w), lambda i: (0, 0)),
        ],
        out_specs=pl.BlockSpec((tm, w), lambda i: (i, 0)),
        out_shape=jax.ShapeDtypeStruct((t, w), BF16),
        compiler_params=_cparams("parallel"),
        name="short_conv",
    )(p, p, p, p, p, conv_w, conv_b.reshape(1, w))


def _block_cumsum(x_ref, cum_ref, carry_ref):
    @pl.when(pl.program_id(1) == 0)
    def _():
        carry_ref[...] = jnp.zeros_like(carry_ref)

    x = x_ref[...]
    n = x.shape[0]
    r = lax.broadcasted_iota(jnp.int32, (n, n), 0)
    c = lax.broadcasted_iota(jnp.int32, (n, n), 1)
    tri = (r >= c).astype(BF16)
    x1 = x.astype(BF16)
    r1 = x - x1.astype(F32)
    x2 = r1.astype(BF16)
    x3 = (r1 - x2.astype(F32)).astype(BF16)
    cum = (jnp.dot(tri, x1, preferred_element_type=F32) + jnp.dot(tri, x2, preferred_element_type=F32)
           + jnp.dot(tri, x3, preferred_element_type=F32)) + carry_ref[...]
    cum_ref[...] = cum
    carry_ref[...] = cum[n - 1:n, :]
    return cum


def _cumsum_kernel(x_ref, cum_ref, cumt_ref, carry_ref):
    cum = _block_cumsum(x_ref, cum_ref, carry_ref)
    cumt_ref[0] = cum.T[:cumt_ref.shape[1], :]


def _cumsum_rows_kernel(x_ref, cum_ref, carry_ref):
    _block_cumsum(x_ref, cum_ref, carry_ref)


def token_cumsum(x):
    t = x.shape[0]
    blk = _tile(t, 256)
    return pl.pallas_call(
        _cumsum_rows_kernel,
        grid=(1, t // blk),
        in_specs=[pl.BlockSpec((blk, LANES), lambda b, i: (i, 0))],
        out_specs=pl.BlockSpec((blk, LANES), lambda b, i: (i, 0)),
        out_shape=jax.ShapeDtypeStruct((t, LANES), F32),
        scratch_shapes=[pltpu.VMEM((1, LANES), F32)],
        compiler_params=_cparams("arbitrary", "arbitrary"),
        name="token_cumsum",
    )(x)


def forget_cumsum(lsf, batch, seq):
    blk = _tile(seq, 256)
    nb = seq // blk
    return pl.pallas_call(
        _cumsum_kernel,
        grid=(batch, nb),
        in_specs=[pl.BlockSpec((blk, LANES), lambda b, i: (b * nb + i, 0))],
        out_specs=[pl.BlockSpec((blk, LANES), lambda b, i: (b * nb + i, 0)),
                   pl.BlockSpec((1, N_HEADS, blk), lambda b, i: (b, 0, i))],
        out_shape=[jax.ShapeDtypeStruct((batch * seq, LANES), F32),
                   jax.ShapeDtypeStruct((batch, N_HEADS, seq), F32)],
        scratch_shapes=[pltpu.VMEM((1, LANES), F32)],
        compiler_params=_cparams("parallel", "arbitrary"),
        name="forget_cumsum",
    )(lsf)


FOX_GROUP = 2


def _fox_kernel(q_ref, k_ref, v_ref, cq_ref, ck_ref, o_ref, m_ref, l_ref, acc_ref, cqh_ref):
    hp = pl.program_id(1)
    i = pl.program_id(2)
    tq = q_ref.shape[0]
    tk = tq
    m_ref[...] = jnp.full_like(m_ref, NEG_INF)
    l_ref[...] = jnp.zeros_like(l_ref)
    acc_ref[...] = jnp.zeros_like(acc_ref)
    cq = cq_ref[...]
    lane = lax.broadcasted_iota(jnp.int32, cq.shape, 1)
    for g in range(FOX_GROUP):
        cqh_ref[g] = jnp.sum(jnp.where(lane == hp * FOX_GROUP + g, cq, 0.0), axis=1, keepdims=True)
    row = lax.broadcasted_iota(jnp.int32, (tq, tk), 0)
    col = lax.broadcasted_iota(jnp.int32, (tq, tk), 1)

    def block(j, on_diagonal):
        start = pl.multiple_of(j * tk, tk)
        for g in range(FOX_GROUP):
            cols = slice(g * HEAD_DIM, (g + 1) * HEAD_DIM)
            k = k_ref[pl.ds(start, tk), cols]
            v = v_ref[pl.ds(start, tk), cols]
            s = lax.dot_general(q_ref[:, cols], k, (((1,), (1,)), ((), ())), preferred_element_type=F32)
            s = s + cqh_ref[g] - ck_ref[0, pl.ds(hp * FOX_GROUP + g, 1), pl.ds(start, tk)]
            if on_diagonal:
                s = jnp.where(col <= row, s, NEG_INF)
            m_old = m_ref[g]
            m_new = jnp.maximum(m_old, jnp.max(s, axis=1, keepdims=True))
            alpha = jnp.exp(m_old - m_new)
            pr = jnp.exp(s - m_new)
            l_ref[g] = alpha * l_ref[g] + jnp.sum(pr, axis=1, keepdims=True)
            acc_ref[g] = alpha * acc_ref[g] + jnp.dot(pr.astype(BF16), v, preferred_element_type=F32)
            m_ref[g] = m_new

    def below_diagonal(j, carry):
        block(j, False)
        return carry

    lax.fori_loop(0, i, below_diagonal, 0)
    block(i, True)
    for g in range(FOX_GROUP):
        o_ref[:, g * HEAD_DIM:(g + 1) * HEAD_DIM] = (acc_ref[g] / l_ref[g]).astype(o_ref.dtype)


def fox_attention(q, kv, cum, cum_t, batch, seq):
    tq = _tile(seq, 512)
    nq = seq // tq
    gw = FOX_GROUP * HEAD_DIM
    n_groups = N_HEADS // FOX_GROUP
    return pl.pallas_call(
        _fox_kernel,
        grid=(batch, n_groups, nq),
        in_specs=[
            pl.BlockSpec((tq, gw), lambda b, h, i: (b * nq + i, h)),
            pl.BlockSpec((seq, gw), lambda b, h, i: (b, h)),
            pl.BlockSpec((seq, gw), lambda b, h, i: (b, n_groups + h)),
            pl.BlockSpec((tq, LANES), lambda b, h, i: (b * nq + i, 0)),
            pl.BlockSpec((1, N_HEADS, seq), lambda b, h, i: (b, 0, 0)),
        ],
        out_specs=pl.BlockSpec((tq, gw), lambda b, h, i: (b * nq + i, h)),
        out_shape=jax.ShapeDtypeStruct((batch * seq, WIDTH), BF16),
        scratch_shapes=[pltpu.VMEM((FOX_GROUP, tq, 1), F32), pltpu.VMEM((FOX_GROUP, tq, 1), F32),
                        pltpu.VMEM((FOX_GROUP, tq, HEAD_DIM), F32), pltpu.VMEM((FOX_GROUP, tq, 1), F32)],
        compiler_params=_cparams("parallel", "parallel", "parallel"),
        name="fox_attention",
    )(q, kv, kv, cum, cum_t)


def _hgrn_levels():
    sizes = []
    half = HGRN_CHUNK // 2
    while half >= HGRN_DIAG:
        sizes.append(half)
        half //= 2
    return sizes


def _hgrn_coefficients():
    c = HGRN_CHUNK
    t = np.arange(c)[:, None]
    u = np.arange(c)[None, :]
    slabs = [(u <= t), (u > t)]
    for size in _hgrn_levels():
        ref = (t // (2 * size)) * (2 * size) + size - 1
        upper = (t % (2 * size)) >= size
        slabs.append(np.where(upper, (u > ref) & (u <= t), (u > t) & (u <= ref)))
    slabs.append((u <= t) & (u // HGRN_DIAG == t // HGRN_DIAG))
    return np.concatenate(slabs, axis=0).astype(np.float32)


def _hgrn_level_masks():
    c = HGRN_CHUNK
    t = np.arange(c)[:, None]
    s = np.arange(c)[None, :]
    masks = []
    for size in _hgrn_levels():
        same = (t // (2 * size)) == (s // (2 * size))
        masks.append(same & ((t % (2 * size)) >= size) & ((s % (2 * size)) < size))
    return np.stack(masks).astype(np.float32)


def _hgrn_kernel(q_ref, f_ref, i_ref, g_ref, lb_ref, norm_ref, coef_ref, mask_ref, o_ref,
                 st_ref, e_ref, kpad_ref, bpad_ref, *, layer):
    c = HGRN_CHUNK
    pad = HGRN_DIAG
    n_lev = len(_hgrn_levels())

    @pl.when(pl.program_id(1) == 0)
    def _():
        st_ref[...] = jnp.zeros_like(st_ref)
        kpad_ref[0:pad, :] = jnp.zeros((pad, WIDTH), F32)
        bpad_ref[0:pad, :] = jnp.zeros((pad, WIDTH), F32)

    lbr = lb_ref[...]
    le = jnp.exp(lbr - jnp.max(lbr, axis=0, keepdims=True))
    lp = le / jnp.sum(le, axis=0, keepdims=True)
    lb = jnp.zeros((1, WIDTH), F32)
    for r in range(1, layer + 1):
        lb = lb + lp[r:r + 1, :]

    f = lb + (1.0 - lb) * _sigmoid(f_ref[...].astype(F32))
    g = jnp.log(f)
    kpad_ref[pad:pad + c, :] = 1.0 - f
    g1 = g.astype(BF16)
    r1 = g - g1.astype(F32)
    g2 = r1.astype(BF16)
    g3 = (r1 - g2.astype(F32)).astype(BF16)
    coef = coef_ref[...]
    e_ref[...] = (jnp.dot(coef, g1, preferred_element_type=F32) + jnp.dot(coef, g2, preferred_element_type=F32)
                  + jnp.dot(coef, g3, preferred_element_type=F32))
    bpad_ref[pad:pad + c, :] = e_ref[(2 + n_lev) * c:(3 + n_lev) * c, :]

    row = lax.broadcasted_iota(jnp.int32, (c, c), 0)
    col = lax.broadcasted_iota(jnp.int32, (c, c), 1)
    sub = lax.broadcasted_iota(jnp.int32, (c, 1), 0) % pad

    def head(h, carry):
        lanes = pl.ds(pl.multiple_of(h * HEAD_DIM, HEAD_DIM), HEAD_DIM)
        q = q_ref[:, lanes].astype(F32)
        k = kpad_ref[pad:pad + c, lanes]
        v = i_ref[:, lanes]
        st = st_ref[h]
        qe = (q * jnp.exp(e_ref[0:c, lanes])).astype(BF16)
        out = lax.dot_general(qe, st.astype(BF16), (((1,), (1,)), ((), ())), preferred_element_type=F32)
        attn = jnp.zeros((c, c), F32)
        for lev in range(n_lev):
            pw = jnp.exp(e_ref[(2 + lev) * c:(3 + lev) * c, lanes])
            a = lax.dot_general((q * pw).astype(BF16), (k * pw).astype(BF16), (((1,), (1,)), ((), ())),
                                preferred_element_type=F32)
            attn = attn + jnp.where(mask_ref[lev] > 0.0, a, 0.0)
        b_in = bpad_ref[pad:pad + c, lanes]
        for off in range(pad):
            k_off = kpad_ref[pad - off:pad - off + c, lanes]
            b_off = bpad_ref[pad - off:pad - off + c, lanes]
            z = q * k_off * jnp.exp(jnp.where(sub >= off, b_in - b_off, NEG_INF))
            a_off = jnp.sum(z, axis=1, keepdims=True)
            attn = attn + jnp.where(col == row - off, a_off, 0.0)
        out = out + jnp.dot(attn.astype(BF16), v, preferred_element_type=F32)
        ke = (k * jnp.exp(e_ref[c:2 * c, lanes])).astype(BF16)
        decay = jnp.exp(e_ref[c - 1:c, lanes])
        st_ref[h] = st * decay + lax.dot_general(v, ke, (((0,), (0,)), ((), ())), preferred_element_type=F32)
        out = out * lax.rsqrt(jnp.mean(out * out, axis=-1, keepdims=True) + EPS)
        out = out * norm_ref[:, lanes] * _silu(g_ref[:, lanes].astype(F32))
        o_ref[:, lanes] = out.astype(o_ref.dtype)
        return carry

    lax.fori_loop(0, N_HEADS, head, 0)


def hgrn_mixer(p, lower_bounds, norm, layer, batch, seq):
    c = HGRN_CHUNK
    nc = seq // c
    depth = lower_bounds.shape[0]
    coef = jnp.asarray(_hgrn_coefficients(), BF16)
    masks = jnp.asarray(_hgrn_level_masks(), F32)
    n_slab = coef.shape[0] // c

    def col_spec(j):
        return pl.BlockSpec((c, WIDTH), lambda b, n: (b * nc + n, j))

    return pl.pallas_call(
        functools.partial(_hgrn_kernel, layer=layer),
        grid=(batch, nc),
        in_specs=[col_spec(0), col_spec(1), col_spec(2), col_spec(3),
                  pl.BlockSpec((depth, WIDTH), lambda b, n: (0, 0)),
                  pl.BlockSpec((1, WIDTH), lambda b, n: (0, 0)),
                  pl.BlockSpec((n_slab * c, c), lambda b, n: (0, 0)),
                  pl.BlockSpec((masks.shape[0], c, c), lambda b, n: (0, 0, 0))],
        out_specs=pl.BlockSpec((c, WIDTH), lambda b, n: (b * nc + n, 0)),
        out_shape=jax.ShapeDtypeStruct((batch * seq, WIDTH), BF16),
        scratch_shapes=[pltpu.VMEM((N_HEADS, HEAD_DIM, HEAD_DIM), F32),
                        pltpu.VMEM((n_slab * c, WIDTH), F32),
                        pltpu.VMEM((HGRN_DIAG + c, WIDTH), F32),
                        pltpu.VMEM((HGRN_DIAG + c, WIDTH), F32)],
        compiler_params=_cparams("parallel", "arbitrary"),
        name="hgrn_mixer",
    )(p, p, p, p, lower_bounds, norm.reshape(1, WIDTH), coef, masks)


def _pad_cols(w, n):
    return jnp.pad(w, ((0, 0), (0, n - w.shape[1])))


def _round_up(n, m):
    return ((n + m - 1) // m) * m


def kernel(x, c, norm_mix, norm_ffn, w_ada, b_ada, w_in, b_gate, fox_b_f, hgrn_lower_bounds, hgrn_norm,
           conv_w, conv_b, w_branch, w_o, ffn_w1, ffn_w3, ffn_w2, router_w, router_b,
           expert_w1, expert_w3, expert_w2, norm_final):
    batch, seq, d = x.shape
    depth = w_ada.shape[0]
    t = batch * seq

    o_fox = 0
    o_ff = 3 * WIDTH
    o_hgrn = o_ff + N_HEADS
    o_conv = o_hgrn + 4 * WIDTH
    o_gate = o_conv + 3 * WIDTH

    mod_all = ada_modulation(c, w_ada, b_ada)
    xt = x.reshape(t, d)
    for layer in range(depth):
        mod = mod_all[layer]
        wl = w_in[layer]
        w_q = wl[:, o_fox:o_fox + WIDTH].astype(BF16)
        w_kv = wl[:, o_fox + WIDTH:o_ff].astype(BF16)
        w_ff = _pad_cols(wl[:, o_ff:o_hgrn], LANES).astype(BF16)
        w_hgrn = wl[:, o_hgrn:o_conv].astype(BF16)
        w_conv = wl[:, o_conv:o_gate].astype(BF16)
        w_gate = wl[:, o_gate:].astype(BF16)
        b_ff = _pad_cols(fox_b_f[layer].reshape(1, N_HEADS), LANES)

        h, lsf = norm_forget(xt, norm_mix[layer], mod, seq, w_ff, b_ff)
        q = project(h, w_q, seq, scale=HEAD_DIM ** -0.5)
        kv = project(h, w_kv, seq)
        p_hgrn = project(h, w_hgrn, seq)
        p_conv = project(h, w_conv, seq)
        gates = gate_project(h, w_gate, b_gate[layer], seq)

        y_a = hgrn_mixer(p_hgrn, hgrn_lower_bounds, hgrn_norm[layer], layer, batch, seq)
        y_b = short_conv(p_conv, conv_w[layer], conv_b[layer], seq)
        cum, cum_t = forget_cumsum(lsf, batch, seq)
        y_c = fox_attention(q, kv, cum, cum_t, batch, seq)

        merged = merge_branches(y_a, y_b, y_c, w_branch[layer].astype(BF16), gates, seq)
        xt = residual_project(merged, w_o[layer].astype(BF16), xt, mod, seq, 2, d // 2)

        i = layer // 2
        if layer % 2 == 0:
            dff = ffn_w1.shape[2]
            dff_pad = _round_up(dff, 1024)
            w1 = _pad_cols(ffn_w1[i], dff_pad).astype(BF16)
            w3 = _pad_cols(ffn_w3[i], dff_pad).astype(BF16)
            w2 = jnp.pad(ffn_w2[i], ((0, dff_pad - dff), (0, 0))).astype(BF16)
            h2 = norm_only(xt, norm_ffn[layer], mod, seq)
            act = glu(h2, w1, w3, seq)
            xt = residual_project(act, w2, xt, mod, seq, 5, dff_pad // 4)
        else:
            xt = moe_ffn(xt, norm_ffn[layer], mod, seq, router_w[i], router_b[i],
                         expert_w1[i].astype(BF16), expert_w3[i].astype(BF16), expert_w2[i].astype(BF16), 5)
    return final_norm(xt, norm_final).reshape(batch, seq, d)
```

```python
import functools

import jax
import jax.numpy as jnp
import numpy as np
from jax import lax
from jax.experimental import pallas as pl
from jax.experimental.pallas import tpu as pltpu

F32 = jnp.float32
BF16 = jnp.bfloat16

N_HEADS = 8
HEAD_DIM = 128
WIDTH = N_HEADS * HEAD_DIM
N_BRANCH = 3
TOP_K = 2
EPS = 1e-6
NEG_INF = -1e30
LANES = 128
HGRN_CHUNK = 128
HGRN_DIAG = 16
VMEM_LIMIT = 56 * 1024 * 1024


def _cparams(*sem):
    return pltpu.CompilerParams(dimension_semantics=sem, vmem_limit_bytes=VMEM_LIMIT)


def _tile(n, pref):
    t = min(n, pref)
    while n % t:
        t //= 2
    return t


def _sigmoid(z):
    return 1.0 / (1.0 + jnp.exp(-z))


def _silu(z):
    return z * _sigmoid(z)


def _ada_kernel(c_ref, w_ref, b_ref, o_ref):
    c = c_ref[...]
    ca = _silu(c).astype(BF16)
    w = w_ref[0].astype(BF16)
    o_ref[0] = jnp.dot(ca, w, preferred_element_type=F32) + b_ref[0]


def ada_modulation(c, w_ada, b_ada):
    depth, d, n = w_ada.shape
    b = c.shape[0]
    rows = 8
    c_pad = jnp.zeros((rows, d), F32).at[:b].set(c)
    tn = _tile(n, 512)
    out = pl.pallas_call(
        _ada_kernel,
        grid=(depth, n // tn),
        in_specs=[
            pl.BlockSpec((rows, d), lambda l, j: (0, 0)),
            pl.BlockSpec((1, d, tn), lambda l, j: (l, 0, j)),
            pl.BlockSpec((1, 1, tn), lambda l, j: (l, 0, j)),
        ],
        out_specs=pl.BlockSpec((1, rows, tn), lambda l, j: (l, 0, j)),
        out_shape=jax.ShapeDtypeStruct((depth, rows, n), F32),
        compiler_params=_cparams("parallel", "parallel"),
        name="ada_modulation",
    )(c_pad, w_ada, b_ada.reshape(depth, 1, n))
    return out[:, :b].reshape(depth, b, 6, d)


def _norm_mod(x_ref, g_ref, mod_ref, shift_row, scale_row):
    x = x_ref[...]
    y = x * lax.rsqrt(jnp.mean(x * x, axis=-1, keepdims=True) + EPS) * g_ref[...]
    return y * (1.0 + mod_ref[0, scale_row:scale_row + 1, :]) + mod_ref[0, shift_row:shift_row + 1, :]


def _log_sigmoid(z):
    return jnp.minimum(z, 0.0) - jnp.log(1.0 + jnp.exp(-jnp.abs(z)))


def _norm_forget_kernel(x_ref, g_ref, mod_ref, wf_ref, bf_ref, h_ref, lsf_ref, *, shift_row, scale_row):
    h = _norm_mod(x_ref, g_ref, mod_ref, shift_row, scale_row).astype(BF16)
    h_ref[...] = h
    z = jnp.dot(h, wf_ref[...], preferred_element_type=F32) + bf_ref[...]
    lsf_ref[...] = _log_sigmoid(z)


def _split_bf16(v):
    hi = v.astype(BF16)
    lo = (v - hi.astype(F32)).astype(BF16)
    return hi, lo


def _norm_router_kernel(x_ref, g_ref, mod_ref, wr_ref, br_ref, h_ref, route_ref, *,
                        shift_row, scale_row, n_experts):
    h = _norm_mod(x_ref, g_ref, mod_ref, shift_row, scale_row)
    h_ref[...] = h
    h_hi, h_lo = _split_bf16(h)
    w_hi, w_lo = _split_bf16(wr_ref[...])
    logits = (jnp.dot(h_hi, w_hi, preferred_element_type=F32)
              + jnp.dot(h_hi, w_lo, preferred_element_type=F32)
              + jnp.dot(h_lo, w_hi, preferred_element_type=F32)) + br_ref[...]
    lane = lax.broadcasted_iota(jnp.int32, logits.shape, 1)
    lg = jnp.where(lane < n_experts, logits, -jnp.inf)
    m1 = jnp.max(lg, axis=1, keepdims=True)
    i1 = jnp.min(jnp.where(lg == m1, lane, LANES), axis=1, keepdims=True)
    lg2 = jnp.where(lane == i1, -jnp.inf, lg)
    m2 = jnp.max(lg2, axis=1, keepdims=True)
    i2 = jnp.min(jnp.where(lg2 == m2, lane, LANES), axis=1, keepdims=True)
    e = jnp.exp(m2 - m1)
    w1 = 1.0 / (1.0 + e)
    w2 = e / (1.0 + e)
    rec = jnp.where((lane == i1) | (lane == i2), 1.0, 0.0)
    rec = jnp.where(lane == n_experts, w1, rec)
    rec = jnp.where(lane == n_experts + 1, w2, rec)
    rec = jnp.where(lane == n_experts + 2, i1.astype(F32), rec)
    rec = jnp.where(lane == n_experts + 3, i2.astype(F32), rec)
    route_ref[...] = rec


def _norm_only_kernel(x_ref, g_ref, mod_ref, h_ref, *, shift_row, scale_row):
    h_ref[...] = _norm_mod(x_ref, g_ref, mod_ref, shift_row, scale_row).astype(BF16)


def _norm_call(body, x, gain, mod, seq, extra_in, extra_specs, extra_out, extra_out_specs, tm, h_dtype=BF16):
    t, d = x.shape
    per_seq = seq // tm
    in_specs = [
        pl.BlockSpec((tm, d), lambda i: (i, 0)),
        pl.BlockSpec((1, d), lambda i: (0, 0)),
        pl.BlockSpec((1, 6, d), lambda i: (i // per_seq, 0, 0)),
    ] + extra_specs
    out_shape = [jax.ShapeDtypeStruct((t, d), h_dtype)] + extra_out
    out_specs = [pl.BlockSpec((tm, d), lambda i: (i, 0))] + extra_out_specs
    return pl.pallas_call(
        body,
        grid=(t // tm,),
        in_specs=in_specs,
        out_specs=out_specs,
        out_shape=out_shape,
        compiler_params=_cparams("parallel"),
        name="norm_mod",
    )(x, gain.reshape(1, d), mod, *extra_in)


def norm_forget(x, gain, mod, seq, w_f, b_f):
    t, d = x.shape
    tm = _tile(seq, 256)
    body = functools.partial(_norm_forget_kernel, shift_row=0, scale_row=1)
    return _norm_call(
        body, x, gain, mod, seq, [w_f, b_f],
        [pl.BlockSpec((d, LANES), lambda i: (0, 0)), pl.BlockSpec((1, LANES), lambda i: (0, 0))],
        [jax.ShapeDtypeStruct((t, LANES), F32)], [pl.BlockSpec((tm, LANES), lambda i: (i, 0))], tm)


def norm_router(x, gain, mod, seq, w_r, b_r, n_experts):
    t, d = x.shape
    assert n_experts + 4 <= LANES
    tm = _tile(seq, 256)
    body = functools.partial(_norm_router_kernel, shift_row=3, scale_row=4, n_experts=n_experts)
    return _norm_call(
        body, x, gain, mod, seq, [w_r, b_r],
        [pl.BlockSpec((d, LANES), lambda i: (0, 0)), pl.BlockSpec((1, LANES), lambda i: (0, 0))],
        [jax.ShapeDtypeStruct((t, LANES), F32)], [pl.BlockSpec((tm, LANES), lambda i: (i, 0))], tm,
        h_dtype=F32)


def norm_only(x, gain, mod, seq):
    tm = _tile(seq, 256)
    body = functools.partial(_norm_only_kernel, shift_row=3, scale_row=4)
    return _norm_call(body, x, gain, mod, seq, [], [], [], [], tm)[0]


def _final_norm_kernel(x_ref, g_ref, o_ref):
    x = x_ref[...]
    o_ref[...] = x * lax.rsqrt(jnp.mean(x * x, axis=-1, keepdims=True) + EPS) * g_ref[...]


def final_norm(x, gain):
    t, d = x.shape
    tm = _tile(t, 256)
    return pl.pallas_call(
        _final_norm_kernel,
        grid=(t // tm,),
        in_specs=[pl.BlockSpec((tm, d), lambda i: (i, 0)), pl.BlockSpec((1, d), lambda i: (0, 0))],
        out_specs=pl.BlockSpec((tm, d), lambda i: (i, 0)),
        out_shape=jax.ShapeDtypeStruct((t, d), F32),
        compiler_params=_cparams("parallel"),
        name="final_norm",
    )(x, gain.reshape(1, d))


def _proj_kernel(a_ref, w_ref, o_ref, *, scale):
    acc = jnp.dot(a_ref[...], w_ref[...], preferred_element_type=F32)
    if scale != 1.0:
        acc = acc * scale
    o_ref[...] = acc.astype(o_ref.dtype)


def _gate_proj_kernel(a_ref, w_ref, b_ref, o_ref):
    acc = jnp.dot(a_ref[...], w_ref[...], preferred_element_type=F32)
    o_ref[...] = _sigmoid(acc + b_ref[...]).astype(o_ref.dtype)


def _mm_tiles(m, n, seq):
    return _tile(seq, 1024), _tile(n, 1024)


def project(a, w, seq, scale=1.0):
    m, k = a.shape
    n = w.shape[1]
    tm, tn = _mm_tiles(m, n, seq)
    return pl.pallas_call(
        functools.partial(_proj_kernel, scale=scale),
        grid=(m // tm, n // tn),
        in_specs=[pl.BlockSpec((tm, k), lambda i, j: (i, 0)), pl.BlockSpec((k, tn), lambda i, j: (0, j))],
        out_specs=pl.BlockSpec((tm, tn), lambda i, j: (i, j)),
        out_shape=jax.ShapeDtypeStruct((m, n), BF16),
        compiler_params=_cparams("parallel", "parallel"),
        name="project",
    )(a, w)


def gate_project(a, w, bias, seq):
    m, k = a.shape
    n = w.shape[1]
    tm, tn = _mm_tiles(m, n, seq)
    return pl.pallas_call(
        _gate_proj_kernel,
        grid=(m // tm, n // tn),
        in_specs=[pl.BlockSpec((tm, k), lambda i, j: (i, 0)), pl.BlockSpec((k, tn), lambda i, j: (0, j)),
                  pl.BlockSpec((1, tn), lambda i, j: (0, j))],
        out_specs=pl.BlockSpec((tm, tn), lambda i, j: (i, j)),
        out_shape=jax.ShapeDtypeStruct((m, n), BF16),
        compiler_params=_cparams("parallel", "parallel"),
        name="gate_project",
    )(a, w, bias.reshape(1, n))


def _residual_kernel(a_ref, w_ref, x_ref, mod_ref, o_ref, acc_ref, *, gate_row):
    k = pl.program_id(2)

    @pl.when(k == 0)
    def _():
        acc_ref[...] = jnp.zeros_like(acc_ref)

    acc_ref[...] += jnp.dot(a_ref[...], w_ref[...], preferred_element_type=F32)

    @pl.when(k == pl.num_programs(2) - 1)
    def _():
        o_ref[...] = x_ref[...] + mod_ref[0, gate_row:gate_row + 1, :] * acc_ref[...]


def residual_project(a, w, x, mod, seq, gate_row, tk_pref):
    m, kdim = a.shape
    n = w.shape[1]
    tm, tn = _mm_tiles(m, n, seq)
    tk = _tile(kdim, tk_pref)
    per_seq = seq // tm
    return pl.pallas_call(
        functools.partial(_residual_kernel, gate_row=gate_row),
        grid=(m // tm, n // tn, kdim // tk),
        in_specs=[
            pl.BlockSpec((tm, tk), lambda i, j, k: (i, k)),
            pl.BlockSpec((tk, tn), lambda i, j, k: (k, j)),
            pl.BlockSpec((tm, tn), lambda i, j, k: (i, j)),
            pl.BlockSpec((1, 6, tn), lambda i, j, k: (i // per_seq, 0, j)),
        ],
        out_specs=pl.BlockSpec((tm, tn), lambda i, j, k: (i, j)),
        out_shape=jax.ShapeDtypeStruct((m, n), F32),
        scratch_shapes=[pltpu.VMEM((tm, tn), F32)],
        compiler_params=_cparams("parallel", "parallel", "arbitrary"),
        name="residual_project",
    )(a, w, x, mod)


def _glu_kernel(h_ref, w1_ref, w3_ref, o_ref):
    h = h_ref[...]
    a = jnp.dot(h, w1_ref[...], preferred_element_type=F32)
    b = jnp.dot(h, w3_ref[...], preferred_element_type=F32)
    o_ref[...] = (_silu(a) * b).astype(o_ref.dtype)


def glu(h, w1, w3, seq):
    m, k = h.shape
    n = w1.shape[1]
    tm = _tile(seq, 1024)
    tn = _tile(n, 512)
    return pl.pallas_call(
        _glu_kernel,
        grid=(m // tm, n // tn),
        in_specs=[pl.BlockSpec((tm, k), lambda i, j: (i, 0)),
                  pl.BlockSpec((k, tn), lambda i, j: (0, j)),
                  pl.BlockSpec((k, tn), lambda i, j: (0, j))],
        out_specs=pl.BlockSpec((tm, tn), lambda i, j: (i, j)),
        out_shape=jax.ShapeDtypeStruct((m, n), BF16),
        compiler_params=_cparams("parallel", "parallel"),
        name="glu",
    )(h, w1, w3)


MOE_ROW_TILE = 512


def _invert_kernel(pos_ref, src_ref, *, n_tokens):
    def clear(p, carry):
        src_ref[p] = 0
        return carry

    lax.fori_loop(0, src_ref.shape[0], clear, 0, unroll=8)

    def place(t, carry):
        src_ref[pos_ref[t]] = t
        src_ref[pos_ref[n_tokens + t]] = t
        return carry

    lax.fori_loop(0, n_tokens, place, 0, unroll=8)


def invert_positions(pos, n_slots, n_tokens):
    return pl.pallas_call(
        functools.partial(_invert_kernel, n_tokens=n_tokens),
        in_specs=[pl.BlockSpec(memory_space=pltpu.SMEM)],
        out_specs=pl.BlockSpec(memory_space=pltpu.SMEM),
        out_shape=jax.ShapeDtypeStruct((n_slots,), jnp.int32),
        name="invert_positions",
    )(pos)


def _moe_up_kernel(src_ref, texp_ref, nv_ref, h_hbm, w1_ref, w3_ref, o_ref, buf, xs, sem):
    i = pl.program_id(0)
    j = pl.program_id(1)
    nv = nv_ref[0]
    tm = xs.shape[0]
    slot = i % 2

    def row_copy(tile, r, s):
        tok = src_ref[tile * tm + r]
        return pltpu.make_async_copy(h_hbm.at[pl.ds(tok, 1), :], buf.at[s, pl.ds(r, 1), :], sem.at[s])

    def start_gather(tile, s):
        def body(r, carry):
            row_copy(tile, r, s).start()
            return carry
        lax.fori_loop(0, tm, body, 0, unroll=8)

    def wait_gather(tile, s):
        def body(r, carry):
            row_copy(tile, r, s).wait()
            return carry
        lax.fori_loop(0, tm, body, 0, unroll=8)

    @pl.when((j == 0) & (i < nv))
    def _():
        @pl.when(i == 0)
        def _():
            start_gather(0, 0)

        @pl.when(i + 1 < nv)
        def _():
            start_gather(i + 1, 1 - slot)

        wait_gather(i, slot)
        xs[...] = buf[slot].astype(BF16)

    @pl.when(i < nv)
    def _():
        x = xs[...]
        a = jnp.dot(x, w1_ref[0], preferred_element_type=F32)
        b = jnp.dot(x, w3_ref[0], preferred_element_type=F32)
        o_ref[...] = (_silu(a) * b).astype(o_ref.dtype)

    @pl.when(i >= nv)
    def _():
        o_ref[...] = jnp.zeros_like(o_ref)


def moe_up(h, w1, w3, src, tile_expert, n_valid):
    d = h.shape[1]
    f = w1.shape[2]
    tm = MOE_ROW_TILE
    n_tiles = src.shape[0] // tm
    tn = _tile(f, 256)
    nj = f // tn

    def w_map(i, j, src, texp, nv):
        ie = jnp.minimum(i, nv[0] - 1)
        return (texp[ie], 0, jnp.where(i < nv[0], j, nj - 1))

    grid_spec = pltpu.PrefetchScalarGridSpec(
        num_scalar_prefetch=3,
        grid=(n_tiles, nj),
        in_specs=[pl.BlockSpec(memory_space=pl.ANY),
                  pl.BlockSpec((1, d, tn), w_map),
                  pl.BlockSpec((1, d, tn), w_map)],
        out_specs=pl.BlockSpec((tm, tn), lambda i, j, src, texp, nv: (i, j)),
        scratch_shapes=[pltpu.VMEM((2, tm, d), F32), pltpu.VMEM((tm, d), BF16),
                        pltpu.SemaphoreType.DMA((2,))],
    )
    return pl.pallas_call(
        _moe_up_kernel,
        grid_spec=grid_spec,
        out_shape=jax.ShapeDtypeStruct((n_tiles * tm, f), BF16),
        compiler_params=_cparams("arbitrary", "arbitrary"),
        name="moe_up",
    )(src, tile_expert, n_valid, h, w1, w3)


def _moe_down_kernel(texp_ref, nv_ref, a_ref, w_ref, o_ref):
    @pl.when(pl.program_id(0) < nv_ref[0])
    def _():
        o_ref[...] = jnp.dot(a_ref[...], w_ref[0], preferred_element_type=F32)

    @pl.when(pl.program_id(0) >= nv_ref[0])
    def _():
        o_ref[...] = jnp.zeros_like(o_ref)


def moe_down(act, w2, tile_expert, n_valid):
    m, f = act.shape
    d = w2.shape[2]
    tm = MOE_ROW_TILE
    tn = _tile(d, 1024)
    nj = d // tn

    def row(i, nv):
        return jnp.minimum(i, nv[0] - 1)

    def col(i, j, nv):
        return jnp.where(i < nv[0], j, nj - 1)

    grid_spec = pltpu.PrefetchScalarGridSpec(
        num_scalar_prefetch=2,
        grid=(m // tm, nj),
        in_specs=[pl.BlockSpec((tm, f), lambda i, j, texp, nv: (row(i, nv), 0)),
                  pl.BlockSpec((1, f, tn), lambda i, j, texp, nv: (texp[row(i, nv)], 0, col(i, j, nv)))],
        out_specs=pl.BlockSpec((tm, tn), lambda i, j, texp, nv: (i, j)),
    )
    return pl.pallas_call(
        _moe_down_kernel,
        grid_spec=grid_spec,
        out_shape=jax.ShapeDtypeStruct((m, d), F32),
        compiler_params=_cparams("arbitrary", "arbitrary"),
        name="moe_down",
    )(tile_expert, n_valid, act, w2)


def _moe_combine_kernel(pos_ref, y_hbm, x_ref, route_ref, mod_ref, o_ref, buf, sem, *,
                        n_tokens, n_experts, gate_row):
    i = pl.program_id(0)
    tm = x_ref.shape[0]
    slot = i % 2

    def row_copy(tile, r, choice, s):
        p = pos_ref[choice * n_tokens + tile * tm + r]
        return pltpu.make_async_copy(y_hbm.at[pl.ds(p, 1), :], buf.at[s, choice, pl.ds(r, 1), :], sem.at[s])

    def start_gather(tile, s):
        def body(r, carry):
            row_copy(tile, r, 0, s).start()
            row_copy(tile, r, 1, s).start()
            return carry
        lax.fori_loop(0, tm, body, 0, unroll=8)

    def wait_gather(tile, s):
        def body(r, carry):
            row_copy(tile, r, 0, s).wait()
            row_copy(tile, r, 1, s).wait()
            return carry
        lax.fori_loop(0, tm, body, 0, unroll=8)

    @pl.when(i == 0)
    def _():
        start_gather(0, 0)

    @pl.when(i + 1 < pl.num_programs(0))
    def _():
        start_gather(i + 1, 1 - slot)

    wait_gather(i, slot)
    rec = route_ref[...]
    lane = lax.broadcasted_iota(jnp.int32, rec.shape, 1)
    w1 = jnp.sum(jnp.where(lane == n_experts, rec, 0.0), axis=1, keepdims=True)
    w2 = jnp.sum(jnp.where(lane == n_experts + 1, rec, 0.0), axis=1, keepdims=True)
    y = w1 * buf[slot, 0] + w2 * buf[slot, 1]
    o_ref[...] = x_ref[...] + mod_ref[0, gate_row:gate_row + 1, :] * y


def moe_combine(y, pos, x, route, mod, seq, n_experts, gate_row):
    t, d = x.shape
    tm = _tile(seq, 256)
    per_seq = seq // tm
    grid_spec = pltpu.PrefetchScalarGridSpec(
        num_scalar_prefetch=1,
        grid=(t // tm,),
        in_specs=[pl.BlockSpec(memory_space=pl.ANY),
                  pl.BlockSpec((tm, d), lambda i, pos: (i, 0)),
                  pl.BlockSpec((tm, LANES), lambda i, pos: (i, 0)),
                  pl.BlockSpec((1, 6, d), lambda i, pos: (i // per_seq, 0, 0))],
        out_specs=pl.BlockSpec((tm, d), lambda i, pos: (i, 0)),
        scratch_shapes=[pltpu.VMEM((2, TOP_K, tm, d), F32), pltpu.SemaphoreType.DMA((2,))],
    )
    return pl.pallas_call(
        functools.partial(_moe_combine_kernel, n_tokens=t, n_experts=n_experts, gate_row=gate_row),
        grid_spec=grid_spec,
        out_shape=jax.ShapeDtypeStruct((t, d), F32),
        compiler_params=_cparams("arbitrary"),
        name="moe_combine",
    )(pos, y, x, route, mod)


def moe_ffn(x, gain, mod, seq, router_w, router_b, w1, w3, w2, gate_row):
    t, d = x.shape
    n_experts = router_w.shape[1]
    tg = MOE_ROW_TILE
    w_r = _pad_cols(router_w, LANES)
    b_r = _pad_cols(router_b.reshape(1, n_experts), LANES)
    h, route = norm_router(x, gain, mod, seq, w_r, b_r, n_experts)
    cum = row_cumsum(route, 1, t)
    chosen = route[:, n_experts + 2:n_experts + 4].astype(jnp.int32)
    rank = jnp.take_along_axis(cum[:, :n_experts] - route[:, :n_experts], chosen, axis=1).astype(jnp.int32)
    counts = cum[t - 1, :n_experts].astype(jnp.int32)
    padded = (counts + tg - 1) // tg * tg
    ends = jnp.cumsum(padded)
    pos = (jnp.take(ends - padded, chosen) + rank).T.reshape(-1)
    n_tiles = (TOP_K * t + n_experts * (tg - 1)) // tg
    n_valid = (ends[n_experts - 1] // tg).reshape(1)
    tile_expert = jnp.minimum(jnp.searchsorted(ends, jnp.arange(n_tiles, dtype=jnp.int32) * tg, side="right"),
                              n_experts - 1).astype(jnp.int32)
    src = invert_positions(pos, n_tiles * tg, t)
    act = moe_up(h, w1, w3, src, tile_expert, n_valid)
    y = moe_down(act, w2, tile_expert, n_valid)
    return moe_combine(y, pos, x, route, mod, seq, n_experts, gate_row)


def _merge_kernel(ya_ref, yb_ref, yc_ref, wa_ref, wb_ref, wc_ref, ga_ref, gb_ref, gc_ref, o_ref):
    out = ga_ref[...].astype(F32) * jnp.dot(ya_ref[...], wa_ref[0], preferred_element_type=F32)
    out += gb_ref[...].astype(F32) * jnp.dot(yb_ref[...], wb_ref[0], preferred_element_type=F32)
    out += gc_ref[...].astype(F32) * jnp.dot(yc_ref[...], wc_ref[0], preferred_element_type=F32)
    o_ref[...] = out.astype(o_ref.dtype)


def merge_branches(y_a, y_b, y_c, w_branch, gates, seq):
    m, k = y_a.shape
    n = w_branch.shape[2]
    tm = _tile(seq, 512)
    tn = _tile(n, 1024)
    nj = n // tn
    y_spec = pl.BlockSpec((tm, k), lambda i, j: (i, 0))

    def w_spec(b):
        return pl.BlockSpec((1, k, tn), lambda i, j: (b, 0, j))

    def g_spec(b):
        return pl.BlockSpec((tm, tn), lambda i, j: (i, b * nj + j))

    return pl.pallas_call(
        _merge_kernel,
        grid=(m // tm, nj),
        in_specs=[y_spec, y_spec, y_spec, w_spec(0), w_spec(1), w_spec(2), g_spec(0), g_spec(1), g_spec(2)],
        out_specs=pl.BlockSpec((tm, tn), lambda i, j: (i, j)),
        out_shape=jax.ShapeDtypeStruct((m, n), BF16),
        compiler_params=_cparams("parallel", "parallel"),
        name="merge_branches",
    )(y_a, y_b, y_c, w_branch, w_branch, w_branch, gates, gates, gates)


def _conv_kernel(h_ref, c_ref, b_ref, hp_ref, cp_ref, w_ref, bias_ref, o_ref, *, blocks_per_seq):
    u = c_ref[...].astype(F32) * h_ref[...].astype(F32)
    tm = u.shape[0]
    halo = hp_ref.shape[0]
    up = cp_ref[...].astype(F32) * hp_ref[...].astype(F32)
    first = (pl.program_id(0) % blocks_per_seq) == 0
    up = jnp.where(first, 0.0, up)
    p1 = up[halo - 1:halo, :]
    p2 = up[halo - 2:halo - 1, :]
    row = lax.broadcasted_iota(jnp.int32, (tm, 1), 0)
    u1 = jnp.where(row == 0, p1, pltpu.roll(u, 1, 0))
    u2 = jnp.where(row == 0, p2, jnp.where(row == 1, p1, pltpu.roll(u, 2, 0)))
    y = w_ref[0:1, :] * u2 + w_ref[1:2, :] * u1 + w_ref[2:3, :] * u + bias_ref[...]
    o_ref[...] = (b_ref[...].astype(F32) * y).astype(o_ref.dtype)


def short_conv(p, conv_w, conv_b, seq):
    t = p.shape[0]
    w = conv_w.shape[1]
    tm = _tile(seq, 512)
    halo = 16
    ratio = tm // halo
    prev = lambda i: jnp.maximum(i * ratio - 1, 0)
    return pl.pallas_call(
        functools.partial(_conv_kernel, blocks_per_seq=seq // tm),
        grid=(t // tm,),
        in_specs=[
            pl.BlockSpec((tm, w), lambda i: (i, 0)),
            pl.BlockSpec((tm, w), lambda i: (i, 1)),
            pl.BlockSpec((tm, w), lambda i: (i, 2)),
            pl.BlockSpec((halo, w), lambda i: (prev(i), 0)),
            pl.BlockSpec((halo, w), lambda i: (prev(i), 1)),
            pl.BlockSpec((3, w), lambda i: (0, 0)),
            pl.BlockSpec((1, w), lambda i: (0, 0)),
        ],
        out_specs=pl.BlockSpec((tm, w), lambda i: (i, 0)),
        out_shape=jax.ShapeDtypeStruct((t, w), BF16),
        compiler_params=_cparams("parallel"),
        name="short_conv",
    )(p, p, p, p, p, conv_w, conv_b.reshape(1, w))


def _block_cumsum(x_ref, cum_ref, carry_ref):
    @pl.when(pl.program_id(1) == 0)
    def _():
        carry_ref[...] = jnp.zeros_like(carry_ref)

    x = x_ref[...]
    n = x.shape[0]
    r = lax.broadcasted_iota(jnp.int32, (n, n), 0)
    c = lax.broadcasted_iota(jnp.int32, (n, n), 1)
    tri = (r >= c).astype(BF16)
    x1 = x.astype(BF16)
    r1 = x - x1.astype(F32)
    x2 = r1.astype(BF16)
    x3 = (r1 - x2.astype(F32)).astype(BF16)
    cum = (jnp.dot(tri, x1, preferred_element_type=F32) + jnp.dot(tri, x2, preferred_element_type=F32)
           + jnp.dot(tri, x3, preferred_element_type=F32)) + carry_ref[...]
    cum_ref[...] = cum
    carry_ref[...] = cum[n - 1:n, :]
    return cum


def _cumsum_kernel(x_ref, cum_ref, carry_ref):
    _block_cumsum(x_ref, cum_ref, carry_ref)


def row_cumsum(x, n_seq, seq):
    blk = _tile(seq, 256)
    nb = seq // blk
    return pl.pallas_call(
        _cumsum_kernel,
        grid=(n_seq, nb),
        in_specs=[pl.BlockSpec((blk, LANES), lambda b, i: (b * nb + i, 0))],
        out_specs=pl.BlockSpec((blk, LANES), lambda b, i: (b * nb + i, 0)),
        out_shape=jax.ShapeDtypeStruct((n_seq * seq, LANES), F32),
        scratch_shapes=[pltpu.VMEM((1, LANES), F32)],
        compiler_params=_cparams("parallel", "arbitrary"),
        name="row_cumsum",
    )(x)


FOX_GROUP = 2
LOG2_E = 1.4426950408889634


def _fox_bias_columns(cum_rows, head, value_lane, ones_lane, ones):
    lane = lax.broadcasted_iota(jnp.int32, cum_rows.shape, 1)
    c = jnp.sum(jnp.where(lane == head, cum_rows, 0.0), axis=1, keepdims=True) * LOG2_E
    hi = c.astype(BF16).astype(F32)
    r1 = c - hi
    mid = r1.astype(BF16).astype(F32)
    lo = r1 - mid
    out = jnp.where((lane >= ones_lane) & (lane < ones_lane + 3), ones, 0.0)
    out = jnp.where(lane == value_lane, hi, out)
    out = jnp.where(lane == value_lane + 1, mid, out)
    out = jnp.where(lane == value_lane + 2, lo, out)
    return out.astype(BF16)


def _fox_kernel(q_ref, k_ref, v_ref, cq_ref, ck_ref, o_ref, kaug_ref, vt_ref, qaug_ref, m_ref, l_ref, acc_ref):
    hp = pl.program_id(1)
    i = pl.program_id(2)
    tq = q_ref.shape[0]
    tk = tq
    seq = k_ref.shape[0]

    @pl.when(i == 0)
    def _():
        def fill(c, carry):
            rows = pl.ds(pl.multiple_of(c * tk, tk), tk)
            for g in range(FOX_GROUP):
                cols = slice(g * HEAD_DIM, (g + 1) * HEAD_DIM)
                kaug_ref[g, rows, 0:HEAD_DIM] = k_ref[rows, cols]
                kaug_ref[g, rows, HEAD_DIM:2 * HEAD_DIM] = _fox_bias_columns(
                    ck_ref[rows, :], hp * FOX_GROUP + g, 0, 3, 1.0)
                vt_ref[g, :, rows] = v_ref[rows, cols].astype(F32).T.astype(BF16)
            return carry

        lax.fori_loop(0, seq // tk, fill, 0)

    for g in range(FOX_GROUP):
        cols = slice(g * HEAD_DIM, (g + 1) * HEAD_DIM)
        qaug_ref[g, :, 0:HEAD_DIM] = q_ref[:, cols]
        qaug_ref[g, :, HEAD_DIM:2 * HEAD_DIM] = _fox_bias_columns(cq_ref[...], hp * FOX_GROUP + g, 3, 0, -1.0)
    m_ref[...] = jnp.full_like(m_ref, NEG_INF)
    l_ref[...] = jnp.zeros_like(l_ref)
    acc_ref[...] = jnp.zeros_like(acc_ref)
    key = lax.broadcasted_iota(jnp.int32, (tk, tq), 0)
    qry = lax.broadcasted_iota(jnp.int32, (tk, tq), 1)

    def block(j, on_diagonal):
        start = pl.multiple_of(j * tk, tk)
        for g in range(FOX_GROUP):
            st = lax.dot_general(kaug_ref[g, pl.ds(start, tk), :], qaug_ref[g], (((1,), (1,)), ((), ())),
                                 preferred_element_type=F32)
            if on_diagonal:
                st = jnp.where(key <= qry, st, NEG_INF)
            m_old = m_ref[g]
            m_new = jnp.maximum(m_old, jnp.max(st, axis=0, keepdims=True))
            alpha = jnp.exp2(m_old - m_new)
            pt = jnp.exp2(st - m_new)
            l_ref[g] = alpha * l_ref[g] + jnp.sum(pt, axis=0, keepdims=True)
            acc_ref[g] = alpha * acc_ref[g] + jnp.dot(vt_ref[g, :, pl.ds(start, tk)], pt.astype(BF16),
                                                      preferred_element_type=F32)
            m_ref[g] = m_new

    def below_diagonal(j, carry):
        block(j, False)
        return carry

    lax.fori_loop(0, i, below_diagonal, 0)
    block(i, True)
    for g in range(FOX_GROUP):
        o_ref[:, g * HEAD_DIM:(g + 1) * HEAD_DIM] = (acc_ref[g] / l_ref[g]).T.astype(o_ref.dtype)


def fox_attention(q, kv, cum, batch, seq):
    tq = _tile(seq, 512)
    nq = seq // tq
    gw = FOX_GROUP * HEAD_DIM
    n_groups = N_HEADS // FOX_GROUP
    return pl.pallas_call(
        _fox_kernel,
        grid=(batch, n_groups, nq),
        in_specs=[
            pl.BlockSpec((tq, gw), lambda b, h, i: (b * nq + i, h)),
            pl.BlockSpec((seq, gw), lambda b, h, i: (b, h)),
            pl.BlockSpec((seq, gw), lambda b, h, i: (b, n_groups + h)),
            pl.BlockSpec((tq, LANES), lambda b, h, i: (b * nq + i, 0)),
            pl.BlockSpec((seq, LANES), lambda b, h, i: (b, 0)),
        ],
        out_specs=pl.BlockSpec((tq, gw), lambda b, h, i: (b * nq + i, h)),
        out_shape=jax.ShapeDtypeStruct((batch * seq, WIDTH), BF16),
        scratch_shapes=[pltpu.VMEM((FOX_GROUP, seq, 2 * HEAD_DIM), BF16),
                        pltpu.VMEM((FOX_GROUP, HEAD_DIM, seq), BF16),
                        pltpu.VMEM((FOX_GROUP, tq, 2 * HEAD_DIM), BF16),
                        pltpu.VMEM((FOX_GROUP, 1, tq), F32), pltpu.VMEM((FOX_GROUP, 1, tq), F32),
                        pltpu.VMEM((FOX_GROUP, HEAD_DIM, tq), F32)],
        compiler_params=_cparams("parallel", "parallel", "arbitrary"),
        name="fox_attention",
    )(q, kv, kv, cum, cum)


def _hgrn_levels():
    sizes = []
    half = HGRN_CHUNK // 2
    while half >= HGRN_DIAG:
        sizes.append(half)
        half //= 2
    return sizes


def _hgrn_coefficients():
    c = HGRN_CHUNK
    t = np.arange(c)[:, None]
    u = np.arange(c)[None, :]
    slabs = [(u <= t), (u > t)]
    for size in _hgrn_levels():
        ref = (t // (2 * size)) * (2 * size) + size - 1
        upper = (t % (2 * size)) >= size
        slabs.append(np.where(upper, (u > ref) & (u <= t), (u > t) & (u <= ref)))
    slabs.append((u <= t) & (u // HGRN_DIAG == t // HGRN_DIAG))
    coef = np.concatenate(slabs, axis=0).astype(np.float32)
    return np.concatenate([coef, coef], axis=1)


def _hgrn_level_masks():
    c = HGRN_CHUNK
    t = np.arange(c)[:, None]
    s = np.arange(c)[None, :]
    masks = []
    for size in _hgrn_levels():
        same = (t // (2 * size)) == (s // (2 * size))
        masks.append(same & ((t % (2 * size)) >= size) & ((s % (2 * size)) < size))
    masks.append((t // HGRN_DIAG == s // HGRN_DIAG) & (s <= t))
    return np.stack(masks).astype(np.float32)


def _hgrn_placement():
    place = np.zeros((HGRN_DIAG, HEAD_DIM, HGRN_CHUNK), np.float32)
    for j in range(HGRN_DIAG):
        place[j, :, j::HGRN_DIAG] = 1.0
    return place.reshape(HGRN_DIAG * HEAD_DIM, HGRN_CHUNK)


def _hgrn_kernel(q_ref, f_ref, i_ref, g_ref, lb_ref, norm_ref, coef_ref, mask_ref, place_ref, o_ref,
                 st_ref, e_ref, kf_ref, *, layer):
    c = HGRN_CHUNK
    blk_rows = HGRN_DIAG
    n_lev = len(_hgrn_levels())
    b_in_rows = (2 + n_lev) * c

    @pl.when(pl.program_id(1) == 0)
    def _():
        st_ref[...] = jnp.zeros_like(st_ref)

    lbr = lb_ref[...]
    le = jnp.exp(lbr - jnp.max(lbr, axis=0, keepdims=True))
    lp = le / jnp.sum(le, axis=0, keepdims=True)
    lb = jnp.zeros((1, WIDTH), F32)
    for r in range(1, layer + 1):
        lb = lb + lp[r:r + 1, :]

    f = lb + (1.0 - lb) * _sigmoid(f_ref[...].astype(F32))
    g = jnp.log(f)
    kf_ref[...] = 1.0 - f
    g_hi, g_lo = _split_bf16(g)
    e_ref[...] = jnp.dot(coef_ref[...], jnp.concatenate([g_hi, g_lo], axis=0), preferred_element_type=F32)

    def head(h, carry):
        lanes = pl.ds(pl.multiple_of(h * HEAD_DIM, HEAD_DIM), HEAD_DIM)
        q = q_ref[:, lanes].astype(F32)
        k = kf_ref[:, lanes]
        v = i_ref[:, lanes]
        st = st_ref[h]
        qe = (q * jnp.exp(e_ref[0:c, lanes])).astype(BF16)
        out = lax.dot_general(qe, st.astype(BF16), (((1,), (1,)), ((), ())), preferred_element_type=F32)
        attn = jnp.zeros((c, c), F32)
        for lev in range(n_lev):
            pw = jnp.exp(e_ref[(2 + lev) * c:(3 + lev) * c, lanes])
            a = lax.dot_general((q * pw).astype(BF16), (k * pw).astype(BF16), (((1,), (1,)), ((), ())),
                                preferred_element_type=F32)
            attn = attn + jnp.where(mask_ref[lev] > 0.0, a, 0.0)
        b_in = e_ref[b_in_rows:b_in_rows + c, lanes]

        def block_row(ref, base, j):
            rows = [jnp.broadcast_to(ref[pl.ds(base + blk * blk_rows + j, 1), lanes], (blk_rows, HEAD_DIM))
                    for blk in range(c // blk_rows)]
            return jnp.concatenate(rows, axis=0)

        z = []
        for j in range(blk_rows):
            k_j = block_row(kf_ref, 0, j)
            b_j = block_row(e_ref, b_in_rows, j)
            z.append((q * k_j * jnp.exp(jnp.minimum(b_in - b_j, 0.0))).astype(BF16))
        diag = jnp.dot(jnp.concatenate(z, axis=1), place_ref[...], preferred_element_type=F32)
        attn = attn + jnp.where(mask_ref[n_lev] > 0.0, diag, 0.0)
        out = out + jnp.dot(attn.astype(BF16), v, preferred_element_type=F32)
        ke = (k * jnp.exp(e_ref[c:2 * c, lanes])).astype(BF16)
        decay = jnp.exp(e_ref[c - 1:c, lanes])
        st_ref[h] = st * decay + lax.dot_general(v, ke, (((0,), (0,)), ((), ())), preferred_element_type=F32)
        out = out * lax.rsqrt(jnp.mean(out * out, axis=-1, keepdims=True) + EPS)
        out = out * norm_ref[:, lanes] * _silu(g_ref[:, lanes].astype(F32))
        o_ref[:, lanes] = out.astype(o_ref.dtype)
        return carry

    lax.fori_loop(0, N_HEADS, head, 0, unroll=4)


def hgrn_mixer(p, lower_bounds, norm, layer, batch, seq):
    c = HGRN_CHUNK
    nc = seq // c
    depth = lower_bounds.shape[0]
    coef = jnp.asarray(_hgrn_coefficients(), BF16)
    masks = jnp.asarray(_hgrn_level_masks(), F32)
    place = jnp.asarray(_hgrn_placement(), BF16)
    n_slab = coef.shape[0] // c

    def col_spec(j):
        return pl.BlockSpec((c, WIDTH), lambda b, n: (b * nc + n, j))

    return pl.pallas_call(
        functools.partial(_hgrn_kernel, layer=layer),
        grid=(batch, nc),
        in_specs=[col_spec(0), col_spec(1), col_spec(2), col_spec(3),
                  pl.BlockSpec((depth, WIDTH), lambda b, n: (0, 0)),
                  pl.BlockSpec((1, WIDTH), lambda b, n: (0, 0)),
                  pl.BlockSpec(coef.shape, lambda b, n: (0, 0)),
                  pl.BlockSpec(masks.shape, lambda b, n: (0, 0, 0)),
                  pl.BlockSpec(place.shape, lambda b, n: (0, 0))],
        out_specs=pl.BlockSpec((c, WIDTH), lambda b, n: (b * nc + n, 0)),
        out_shape=jax.ShapeDtypeStruct((batch * seq, WIDTH), BF16),
        scratch_shapes=[pltpu.VMEM((N_HEADS, HEAD_DIM, HEAD_DIM), F32),
                        pltpu.VMEM((n_slab * c, WIDTH), F32),
                        pltpu.VMEM((c, WIDTH), F32)],
        compiler_params=_cparams("parallel", "arbitrary"),
        name="hgrn_mixer",
    )(p, p, p, p, lower_bounds, norm.reshape(1, WIDTH), coef, masks, place)


def _pad_cols(w, n):
    return jnp.pad(w, ((0, 0), (0, n - w.shape[1])))


def _round_up(n, m):
    return ((n + m - 1) // m) * m


def kernel(x, c, norm_mix, norm_ffn, w_ada, b_ada, w_in, b_gate, fox_b_f, hgrn_lower_bounds, hgrn_norm,
           conv_w, conv_b, w_branch, w_o, ffn_w1, ffn_w3, ffn_w2, router_w, router_b,
           expert_w1, expert_w3, expert_w2, norm_final):
    batch, seq, d = x.shape
    depth = w_ada.shape[0]
    t = batch * seq

    o_fox = 0
    o_ff = 3 * WIDTH
    o_hgrn = o_ff + N_HEADS
    o_conv = o_hgrn + 4 * WIDTH
    o_gate = o_conv + 3 * WIDTH

    mod_all = ada_modulation(c, w_ada, b_ada)
    xt = x.reshape(t, d)
    for layer in range(depth):
        mod = mod_all[layer]
        wl = w_in[layer]
        w_q = wl[:, o_fox:o_fox + WIDTH].astype(BF16)
        w_kv = wl[:, o_fox + WIDTH:o_ff].astype(BF16)
        w_ff = _pad_cols(wl[:, o_ff:o_hgrn], LANES).astype(BF16)
        w_hgrn = wl[:, o_hgrn:o_conv].astype(BF16)
        w_conv = wl[:, o_conv:o_gate].astype(BF16)
        w_gate = wl[:, o_gate:].astype(BF16)
        b_ff = _pad_cols(fox_b_f[layer].reshape(1, N_HEADS), LANES)

        h, lsf = norm_forget(xt, norm_mix[layer], mod, seq, w_ff, b_ff)
        q = project(h, w_q, seq, scale=HEAD_DIM ** -0.5 * LOG2_E)
        kv = project(h, w_kv, seq)
        p_hgrn = project(h, w_hgrn, seq)
        p_conv = project(h, w_conv, seq)
        gates = gate_project(h, w_gate, b_gate[layer], seq)

        y_a = hgrn_mixer(p_hgrn, hgrn_lower_bounds, hgrn_norm[layer], layer, batch, seq)
        y_b = short_conv(p_conv, conv_w[layer], conv_b[layer], seq)
        cum = row_cumsum(lsf, batch, seq)
        y_c = fox_attention(q, kv, cum, batch, seq)

        merged = merge_branches(y_a, y_b, y_c, w_branch[layer].astype(BF16), gates, seq)
        xt = residual_project(merged, w_o[layer].astype(BF16), xt, mod, seq, 2, d // 2)

        i = layer // 2
        if layer % 2 == 0:
            dff = ffn_w1.shape[2]
            dff_pad = _round_up(dff, 1024)
            w1 = _pad_cols(ffn_w1[i], dff_pad).astype(BF16)
            w3 = _pad_cols(ffn_w3[i], dff_pad).astype(BF16)
            w2 = jnp.pad(ffn_w2[i], ((0, dff_pad - dff), (0, 0))).astype(BF16)
            h2 = norm_only(xt, norm_ffn[layer], mod, seq)
            act = glu(h2, w1, w3, seq)
            xt = residual_project(act, w2, xt, mod, seq, 5, dff_pad // 4)
        else:
            xt = moe_ffn(xt, norm_ffn[layer], mod, seq, router_w[i], router_b[i],
                         expert_w1[i].astype(BF16), expert_w3[i].astype(BF16), expert_w2[i].astype(BF16), 5)
    return final_norm(xt, norm_final).reshape(batch, seq, d)
```

```python
import functools

import jax
import jax.numpy as jnp
import numpy as np
from jax import lax
from jax.experimental import pallas as pl
from jax.experimental.pallas import tpu as pltpu

F32 = jnp.float32
BF16 = jnp.bfloat16

N_HEADS = 8
HEAD_DIM = 128
WIDTH = N_HEADS * HEAD_DIM
N_BRANCH = 3
TOP_K = 2
EPS = 1e-6
NEG_INF = -1e30
LANES = 128
HGRN_CHUNK = 128
HGRN_DIAG = 16
NORM_ROWS = 512
VMEM_LIMIT = 56 * 1024 * 1024


def _cparams(*sem):
    return pltpu.CompilerParams(dimension_semantics=sem, vmem_limit_bytes=VMEM_LIMIT)


def _tile(n, pref):
    t = min(n, pref)
    while n % t:
        t //= 2
    return t


def _sigmoid(z):
    return 1.0 / (1.0 + jnp.exp(-z))


def _silu(z):
    return z * _sigmoid(z)


def _ada_kernel(ct_ref, w_ref, b_ref, o_ref, act_ref, *, batch):
    @pl.when((pl.program_id(0) == 0) & (pl.program_id(1) == 0))
    def _():
        act = _silu(ct_ref[...])
        for b in range(batch):
            act_ref[b] = jnp.broadcast_to(act[:, b:b + 1], act_ref.shape[1:])

    tn = o_ref.shape[2]
    o_ref[0] = jnp.zeros(o_ref.shape[1:], F32)
    for b in range(batch):
        a = act_ref[b]
        for jb in range(tn // LANES):
            cols = slice(jb * LANES, (jb + 1) * LANES)
            o_ref[0, b:b + 1, cols] = (jnp.sum(w_ref[0, :, cols] * a, axis=0, keepdims=True)
                                       + b_ref[0, :, cols])


def ada_modulation(c, w_ada, b_ada):
    depth, d, n = w_ada.shape
    b = c.shape[0]
    rows = _round_up(b, 8)
    tn = _tile(n, 512)
    out = pl.pallas_call(
        functools.partial(_ada_kernel, batch=b),
        grid=(depth, n // tn),
        in_specs=[
            pl.BlockSpec((d, b), lambda l, j: (0, 0)),
            pl.BlockSpec((1, d, tn), lambda l, j: (l, 0, j)),
            pl.BlockSpec((1, 1, tn), lambda l, j: (l, 0, j)),
        ],
        out_specs=pl.BlockSpec((1, rows, tn), lambda l, j: (l, 0, j)),
        out_shape=jax.ShapeDtypeStruct((depth, rows, n), F32),
        scratch_shapes=[pltpu.VMEM((b, d, LANES), F32)],
        compiler_params=_cparams("arbitrary", "arbitrary"),
        name="ada_modulation",
    )(c.T, w_ada, b_ada.reshape(depth, 1, n))
    return out[:, :b].reshape(depth, b, 6, d)


def _norm_mod(x_ref, g_ref, mod_ref, shift_row, scale_row):
    x = x_ref[...]
    y = x * lax.rsqrt(jnp.mean(x * x, axis=-1, keepdims=True) + EPS) * g_ref[...]
    return y * (1.0 + mod_ref[0, scale_row:scale_row + 1, :]) + mod_ref[0, shift_row:shift_row + 1, :]


def _log_sigmoid(z):
    return jnp.minimum(z, 0.0) - jnp.log(1.0 + jnp.exp(-jnp.abs(z)))


def _norm_forget_kernel(x_ref, g_ref, mod_ref, wf_ref, bf_ref, h_ref, lsf_ref, *, shift_row, scale_row):
    h = _norm_mod(x_ref, g_ref, mod_ref, shift_row, scale_row).astype(BF16)
    h_ref[...] = h
    z = jnp.dot(h, wf_ref[...], preferred_element_type=F32) + bf_ref[...]
    lsf_ref[...] = _log_sigmoid(z)


def _split_bf16(v):
    hi = v.astype(BF16)
    lo = (v - hi.astype(F32)).astype(BF16)
    return hi, lo


def _norm_router_kernel(x_ref, g_ref, mod_ref, wr_ref, br_ref, h_ref, route_ref, *,
                        shift_row, scale_row, n_experts):
    h = _norm_mod(x_ref, g_ref, mod_ref, shift_row, scale_row)
    h_ref[...] = h
    h_hi, h_lo = _split_bf16(h)
    w_hi, w_lo = _split_bf16(wr_ref[...])
    logits = (jnp.dot(h_hi, w_hi, preferred_element_type=F32)
              + jnp.dot(h_hi, w_lo, preferred_element_type=F32)
              + jnp.dot(h_lo, w_hi, preferred_element_type=F32)) + br_ref[...]
    lane = lax.broadcasted_iota(jnp.int32, logits.shape, 1)
    lg = jnp.where(lane < n_experts, logits, -jnp.inf)
    m1 = jnp.max(lg, axis=1, keepdims=True)
    i1 = jnp.min(jnp.where(lg == m1, lane, LANES), axis=1, keepdims=True)
    lg2 = jnp.where(lane == i1, -jnp.inf, lg)
    m2 = jnp.max(lg2, axis=1, keepdims=True)
    i2 = jnp.min(jnp.where(lg2 == m2, lane, LANES), axis=1, keepdims=True)
    e = jnp.exp(m2 - m1)
    w1 = 1.0 / (1.0 + e)
    w2 = e / (1.0 + e)
    rec = jnp.where((lane == i1) | (lane == i2), 1.0, 0.0)
    rec = jnp.where(lane == n_experts, w1, rec)
    rec = jnp.where(lane == n_experts + 1, w2, rec)
    rec = jnp.where(lane == n_experts + 2, i1.astype(F32), rec)
    rec = jnp.where(lane == n_experts + 3, i2.astype(F32), rec)
    route_ref[...] = rec


def _norm_only_kernel(x_ref, g_ref, mod_ref, h_ref, *, shift_row, scale_row):
    h_ref[...] = _norm_mod(x_ref, g_ref, mod_ref, shift_row, scale_row).astype(BF16)


def _norm_call(body, x, gain, mod, seq, extra_in, extra_specs, extra_out, extra_out_specs, tm, h_dtype=BF16):
    t, d = x.shape
    per_seq = seq // tm
    in_specs = [
        pl.BlockSpec((tm, d), lambda i: (i, 0)),
        pl.BlockSpec((1, d), lambda i: (0, 0)),
        pl.BlockSpec((1, 6, d), lambda i: (i // per_seq, 0, 0)),
    ] + extra_specs
    out_shape = [jax.ShapeDtypeStruct((t, d), h_dtype)] + extra_out
    out_specs = [pl.BlockSpec((tm, d), lambda i: (i, 0))] + extra_out_specs
    return pl.pallas_call(
        body,
        grid=(t // tm,),
        in_specs=in_specs,
        out_specs=out_specs,
        out_shape=out_shape,
        compiler_params=_cparams("parallel"),
        name="norm_mod",
    )(x, gain.reshape(1, d), mod, *extra_in)


def norm_forget(x, gain, mod, seq, w_f, b_f):
    t, d = x.shape
    tm = _tile(seq, NORM_ROWS)
    body = functools.partial(_norm_forget_kernel, shift_row=0, scale_row=1)
    return _norm_call(
        body, x, gain, mod, seq, [w_f, b_f],
        [pl.BlockSpec((d, LANES), lambda i: (0, 0)), pl.BlockSpec((1, LANES), lambda i: (0, 0))],
        [jax.ShapeDtypeStruct((t, LANES), F32)], [pl.BlockSpec((tm, LANES), lambda i: (i, 0))], tm)


def norm_router(x, gain, mod, seq, w_r, b_r, n_experts):
    t, d = x.shape
    assert n_experts + 4 <= LANES
    tm = _tile(seq, NORM_ROWS)
    body = functools.partial(_norm_router_kernel, shift_row=3, scale_row=4, n_experts=n_experts)
    return _norm_call(
        body, x, gain, mod, seq, [w_r, b_r],
        [pl.BlockSpec((d, LANES), lambda i: (0, 0)), pl.BlockSpec((1, LANES), lambda i: (0, 0))],
        [jax.ShapeDtypeStruct((t, LANES), F32)], [pl.BlockSpec((tm, LANES), lambda i: (i, 0))], tm,
        h_dtype=F32)


def norm_only(x, gain, mod, seq):
    tm = _tile(seq, NORM_ROWS)
    body = functools.partial(_norm_only_kernel, shift_row=3, scale_row=4)
    return _norm_call(body, x, gain, mod, seq, [], [], [], [], tm)[0]


def _final_norm_kernel(x_ref, g_ref, o_ref):
    x = x_ref[...]
    o_ref[...] = x * lax.rsqrt(jnp.mean(x * x, axis=-1, keepdims=True) + EPS) * g_ref[...]


def final_norm(x, gain):
    t, d = x.shape
    tm = _tile(t, NORM_ROWS)
    return pl.pallas_call(
        _final_norm_kernel,
        grid=(t // tm,),
        in_specs=[pl.BlockSpec((tm, d), lambda i: (i, 0)), pl.BlockSpec((1, d), lambda i: (0, 0))],
        out_specs=pl.BlockSpec((tm, d), lambda i: (i, 0)),
        out_shape=jax.ShapeDtypeStruct((t, d), F32),
        compiler_params=_cparams("parallel"),
        name="final_norm",
    )(x, gain.reshape(1, d))


def _proj_kernel(a_ref, w_ref, o_ref, *, scale):
    acc = jnp.dot(a_ref[...], w_ref[...], preferred_element_type=F32)
    if scale != 1.0:
        acc = acc * scale
    o_ref[...] = acc.astype(o_ref.dtype)


def _gate_proj_kernel(a_ref, w_ref, b_ref, o_ref):
    acc = jnp.dot(a_ref[...], w_ref[...], preferred_element_type=F32)
    o_ref[...] = _sigmoid(acc + b_ref[...]).astype(o_ref.dtype)


def _mm_tiles(m, n, seq):
    return _tile(seq, 1024), _tile(n, 1024)


def project(a, w, seq, scale=1.0):
    m, k = a.shape
    n = w.shape[1]
    tm, tn = _mm_tiles(m, n, seq)
    return pl.pallas_call(
        functools.partial(_proj_kernel, scale=scale),
        grid=(m // tm, n // tn),
        in_specs=[pl.BlockSpec((tm, k), lambda i, j: (i, 0)), pl.BlockSpec((k, tn), lambda i, j: (0, j))],
        out_specs=pl.BlockSpec((tm, tn), lambda i, j: (i, j)),
        out_shape=jax.ShapeDtypeStruct((m, n), BF16),
        compiler_params=_cparams("parallel", "parallel"),
        name="project",
    )(a, w)


def gate_project(a, w, bias, seq):
    m, k = a.shape
    n = w.shape[1]
    tm, tn = _mm_tiles(m, n, seq)
    return pl.pallas_call(
        _gate_proj_kernel,
        grid=(m // tm, n // tn),
        in_specs=[pl.BlockSpec((tm, k), lambda i, j: (i, 0)), pl.BlockSpec((k, tn), lambda i, j: (0, j)),
                  pl.BlockSpec((1, tn), lambda i, j: (0, j))],
        out_specs=pl.BlockSpec((tm, tn), lambda i, j: (i, j)),
        out_shape=jax.ShapeDtypeStruct((m, n), BF16),
        compiler_params=_cparams("parallel", "parallel"),
        name="gate_project",
    )(a, w, bias.reshape(1, n))


def _residual_kernel(a_ref, w_ref, x_ref, mod_ref, o_ref, acc_ref, *, gate_row):
    k = pl.program_id(2)

    @pl.when(k == 0)
    def _():
        acc_ref[...] = jnp.zeros_like(acc_ref)

    acc_ref[...] += jnp.dot(a_ref[...], w_ref[...], preferred_element_type=F32)

    @pl.when(k == pl.num_programs(2) - 1)
    def _():
        o_ref[...] = x_ref[...] + mod_ref[0, gate_row:gate_row + 1, :] * acc_ref[...]


def residual_project(a, w, x, mod, seq, gate_row, tk_pref):
    m, kdim = a.shape
    n = w.shape[1]
    tm, tn = _mm_tiles(m, n, seq)
    tk = _tile(kdim, tk_pref)
    per_seq = seq // tm
    return pl.pallas_call(
        functools.partial(_residual_kernel, gate_row=gate_row),
        grid=(m // tm, n // tn, kdim // tk),
        in_specs=[
            pl.BlockSpec((tm, tk), lambda i, j, k: (i, k)),
            pl.BlockSpec((tk, tn), lambda i, j, k: (k, j)),
            pl.BlockSpec((tm, tn), lambda i, j, k: (i, j)),
            pl.BlockSpec((1, 6, tn), lambda i, j, k: (i // per_seq, 0, j)),
        ],
        out_specs=pl.BlockSpec((tm, tn), lambda i, j, k: (i, j)),
        out_shape=jax.ShapeDtypeStruct((m, n), F32),
        scratch_shapes=[pltpu.VMEM((tm, tn), F32)],
        compiler_params=_cparams("parallel", "parallel", "arbitrary"),
        name="residual_project",
    )(a, w, x, mod)


def _glu_kernel(h_ref, w1_ref, w3_ref, o_ref):
    h = h_ref[...]
    a = jnp.dot(h, w1_ref[...], preferred_element_type=F32)
    b = jnp.dot(h, w3_ref[...], preferred_element_type=F32)
    o_ref[...] = (_silu(a) * b).astype(o_ref.dtype)


def glu(h, w1, w3, seq):
    m, k = h.shape
    n = w1.shape[1]
    tm = _tile(seq, 1024)
    tn = _tile(n, 512)
    return pl.pallas_call(
        _glu_kernel,
        grid=(m // tm, n // tn),
        in_specs=[pl.BlockSpec((tm, k), lambda i, j: (i, 0)),
                  pl.BlockSpec((k, tn), lambda i, j: (0, j)),
                  pl.BlockSpec((k, tn), lambda i, j: (0, j))],
        out_specs=pl.BlockSpec((tm, tn), lambda i, j: (i, j)),
        out_shape=jax.ShapeDtypeStruct((m, n), BF16),
        compiler_params=_cparams("parallel", "parallel"),
        name="glu",
    )(h, w1, w3)


MOE_ROW_TILE = 512


def _invert_kernel(pos_ref, src_ref, *, n_tokens):
    def clear(p, carry):
        src_ref[p] = 0
        return carry

    lax.fori_loop(0, src_ref.shape[0], clear, 0, unroll=8)

    def place(t, carry):
        src_ref[pos_ref[t]] = t
        src_ref[pos_ref[n_tokens + t]] = t
        return carry

    lax.fori_loop(0, n_tokens, place, 0, unroll=8)


def invert_positions(pos, n_slots, n_tokens):
    return pl.pallas_call(
        functools.partial(_invert_kernel, n_tokens=n_tokens),
        in_specs=[pl.BlockSpec(memory_space=pltpu.SMEM)],
        out_specs=pl.BlockSpec(memory_space=pltpu.SMEM),
        out_shape=jax.ShapeDtypeStruct((n_slots,), jnp.int32),
        name="invert_positions",
    )(pos)


def _moe_up_kernel(src_ref, texp_ref, nv_ref, h_hbm, w1_ref, w3_ref, o_ref, buf, xs, sem):
    i = pl.program_id(0)
    j = pl.program_id(1)
    nv = nv_ref[0]
    tm = xs.shape[0]
    slot = i % 2

    def row_copy(tile, r, s):
        tok = src_ref[tile * tm + r]
        return pltpu.make_async_copy(h_hbm.at[pl.ds(tok, 1), :], buf.at[s, pl.ds(r, 1), :], sem.at[s])

    def start_gather(tile, s):
        def body(r, carry):
            row_copy(tile, r, s).start()
            return carry
        lax.fori_loop(0, tm, body, 0, unroll=8)

    def wait_gather(tile, s):
        def body(r, carry):
            row_copy(tile, r, s).wait()
            return carry
        lax.fori_loop(0, tm, body, 0, unroll=8)

    @pl.when((j == 0) & (i < nv))
    def _():
        @pl.when(i == 0)
        def _():
            start_gather(0, 0)

        @pl.when(i + 1 < nv)
        def _():
            start_gather(i + 1, 1 - slot)

        wait_gather(i, slot)
        xs[...] = buf[slot].astype(BF16)

    @pl.when(i < nv)
    def _():
        x = xs[...]
        a = jnp.dot(x, w1_ref[0], preferred_element_type=F32)
        b = jnp.dot(x, w3_ref[0], preferred_element_type=F32)
        o_ref[...] = (_silu(a) * b).astype(o_ref.dtype)

    @pl.when(i >= nv)
    def _():
        o_ref[...] = jnp.zeros_like(o_ref)


def moe_up(h, w1, w3, src, tile_expert, n_valid):
    d = h.shape[1]
    f = w1.shape[2]
    tm = MOE_ROW_TILE
    n_tiles = src.shape[0] // tm
    tn = _tile(f, 256)
    nj = f // tn

    def w_map(i, j, src, texp, nv):
        ie = jnp.minimum(i, nv[0] - 1)
        return (texp[ie], 0, jnp.where(i < nv[0], j, nj - 1))

    grid_spec = pltpu.PrefetchScalarGridSpec(
        num_scalar_prefetch=3,
        grid=(n_tiles, nj),
        in_specs=[pl.BlockSpec(memory_space=pl.ANY),
                  pl.BlockSpec((1, d, tn), w_map),
                  pl.BlockSpec((1, d, tn), w_map)],
        out_specs=pl.BlockSpec((tm, tn), lambda i, j, src, texp, nv: (i, j)),
        scratch_shapes=[pltpu.VMEM((2, tm, d), F32), pltpu.VMEM((tm, d), BF16),
                        pltpu.SemaphoreType.DMA((2,))],
    )
    return pl.pallas_call(
        _moe_up_kernel,
        grid_spec=grid_spec,
        out_shape=jax.ShapeDtypeStruct((n_tiles * tm, f), BF16),
        compiler_params=_cparams("arbitrary", "arbitrary"),
        name="moe_up",
    )(src, tile_expert, n_valid, h, w1, w3)


def _moe_down_kernel(texp_ref, nv_ref, a_ref, w_ref, o_ref):
    @pl.when(pl.program_id(0) < nv_ref[0])
    def _():
        o_ref[...] = jnp.dot(a_ref[...], w_ref[0], preferred_element_type=F32)

    @pl.when(pl.program_id(0) >= nv_ref[0])
    def _():
        o_ref[...] = jnp.zeros_like(o_ref)


def moe_down(act, w2, tile_expert, n_valid):
    m, f = act.shape
    d = w2.shape[2]
    tm = MOE_ROW_TILE
    tn = _tile(d, 1024)
    nj = d // tn

    def row(i, nv):
        return jnp.minimum(i, nv[0] - 1)

    def col(i, j, nv):
        return jnp.where(i < nv[0], j, nj - 1)

    grid_spec = pltpu.PrefetchScalarGridSpec(
        num_scalar_prefetch=2,
        grid=(m // tm, nj),
        in_specs=[pl.BlockSpec((tm, f), lambda i, j, texp, nv: (row(i, nv), 0)),
                  pl.BlockSpec((1, f, tn), lambda i, j, texp, nv: (texp[row(i, nv)], 0, col(i, j, nv)))],
        out_specs=pl.BlockSpec((tm, tn), lambda i, j, texp, nv: (i, j)),
    )
    return pl.pallas_call(
        _moe_down_kernel,
        grid_spec=grid_spec,
        out_shape=jax.ShapeDtypeStruct((m, d), F32),
        compiler_params=_cparams("arbitrary", "arbitrary"),
        name="moe_down",
    )(tile_expert, n_valid, act, w2)


def _moe_combine_kernel(pos_ref, y_hbm, x_ref, route_ref, mod_ref, fg_ref, o_ref, buf, sem, *,
                        n_tokens, n_experts, gate_row, final_norm):
    i = pl.program_id(0)
    tm = x_ref.shape[0]
    slot = i % 2

    def row_copy(tile, r, choice, s):
        p = pos_ref[choice * n_tokens + tile * tm + r]
        return pltpu.make_async_copy(y_hbm.at[pl.ds(p, 1), :], buf.at[s, choice, pl.ds(r, 1), :], sem.at[s])

    def start_gather(tile, s):
        def body(r, carry):
            row_copy(tile, r, 0, s).start()
            row_copy(tile, r, 1, s).start()
            return carry
        lax.fori_loop(0, tm, body, 0, unroll=8)

    def wait_gather(tile, s):
        def body(r, carry):
            row_copy(tile, r, 0, s).wait()
            row_copy(tile, r, 1, s).wait()
            return carry
        lax.fori_loop(0, tm, body, 0, unroll=8)

    @pl.when(i == 0)
    def _():
        start_gather(0, 0)

    @pl.when(i + 1 < pl.num_programs(0))
    def _():
        start_gather(i + 1, 1 - slot)

    wait_gather(i, slot)
    rec = route_ref[...]
    lane = lax.broadcasted_iota(jnp.int32, rec.shape, 1)
    w1 = jnp.sum(jnp.where(lane == n_experts, rec, 0.0), axis=1, keepdims=True)
    w2 = jnp.sum(jnp.where(lane == n_experts + 1, rec, 0.0), axis=1, keepdims=True)
    y = w1 * buf[slot, 0] + w2 * buf[slot, 1]
    out = x_ref[...] + mod_ref[0, gate_row:gate_row + 1, :] * y
    if final_norm:
        out = out * lax.rsqrt(jnp.mean(out * out, axis=-1, keepdims=True) + EPS) * fg_ref[...]
    o_ref[...] = out


def moe_combine(y, pos, x, route, mod, seq, n_experts, gate_row, final_gain=None):
    t, d = x.shape
    tm = _tile(seq, 256)
    per_seq = seq // tm
    final_norm = final_gain is not None
    fg = (final_gain if final_norm else jnp.ones((d,), F32)).reshape(1, d)
    grid_spec = pltpu.PrefetchScalarGridSpec(
        num_scalar_prefetch=1,
        grid=(t // tm,),
        in_specs=[pl.BlockSpec(memory_space=pl.ANY),
                  pl.BlockSpec((tm, d), lambda i, pos: (i, 0)),
                  pl.BlockSpec((tm, LANES), lambda i, pos: (i, 0)),
                  pl.BlockSpec((1, 6, d), lambda i, pos: (i // per_seq, 0, 0)),
                  pl.BlockSpec((1, d), lambda i, pos: (0, 0))],
        out_specs=pl.BlockSpec((tm, d), lambda i, pos: (i, 0)),
        scratch_shapes=[pltpu.VMEM((2, TOP_K, tm, d), F32), pltpu.SemaphoreType.DMA((2,))],
    )
    return pl.pallas_call(
        functools.partial(_moe_combine_kernel, n_tokens=t, n_experts=n_experts, gate_row=gate_row,
                          final_norm=final_norm),
        grid_spec=grid_spec,
        out_shape=jax.ShapeDtypeStruct((t, d), F32),
        compiler_params=_cparams("arbitrary"),
        name="moe_combine",
    )(pos, y, x, route, mod, fg)


def moe_ffn(x, gain, mod, seq, router_w, router_b, w1, w3, w2, gate_row, final_gain=None):
    t, d = x.shape
    n_experts = router_w.shape[1]
    tg = MOE_ROW_TILE
    w_r = _pad_cols(router_w, LANES)
    b_r = _pad_cols(router_b.reshape(1, n_experts), LANES)
    h, route = norm_router(x, gain, mod, seq, w_r, b_r, n_experts)
    cum = row_cumsum(route, 1, t)
    chosen = route[:, n_experts + 2:n_experts + 4].astype(jnp.int32)
    rank = jnp.take_along_axis(cum[:, :n_experts] - route[:, :n_experts], chosen, axis=1).astype(jnp.int32)
    counts = cum[t - 1, :n_experts].astype(jnp.int32)
    padded = (counts + tg - 1) // tg * tg
    ends = jnp.cumsum(padded)
    pos = (jnp.take(ends - padded, chosen) + rank).T.reshape(-1)
    n_tiles = (TOP_K * t + n_experts * (tg - 1)) // tg
    n_valid = (ends[n_experts - 1] // tg).reshape(1)
    tile_expert = jnp.minimum(jnp.searchsorted(ends, jnp.arange(n_tiles, dtype=jnp.int32) * tg, side="right"),
                              n_experts - 1).astype(jnp.int32)
    src = invert_positions(pos, n_tiles * tg, t)
    act = moe_up(h, w1, w3, src, tile_expert, n_valid)
    y = moe_down(act, w2, tile_expert, n_valid)
    return moe_combine(y, pos, x, route, mod, seq, n_experts, gate_row, final_gain)


def _merge_kernel(ya_ref, yb_ref, yc_ref, wa_ref, wb_ref, wc_ref, ga_ref, gb_ref, gc_ref, o_ref):
    out = ga_ref[...].astype(F32) * jnp.dot(ya_ref[...], wa_ref[0], preferred_element_type=F32)
    out += gb_ref[...].astype(F32) * jnp.dot(yb_ref[...], wb_ref[0], preferred_element_type=F32)
    out += gc_ref[...].astype(F32) * jnp.dot(yc_ref[...], wc_ref[0], preferred_element_type=F32)
    o_ref[...] = out.astype(o_ref.dtype)


def merge_branches(y_a, y_b, y_c, w_branch, gates, seq):
    m, k = y_a.shape
    n = w_branch.shape[2]
    tm = _tile(seq, 1024)
    tn = _tile(n, 1024)
    nj = n // tn
    y_spec = pl.BlockSpec((tm, k), lambda i, j: (i, 0))

    def w_spec(b):
        return pl.BlockSpec((1, k, tn), lambda i, j: (b, 0, j))

    def g_spec(b):
        return pl.BlockSpec((tm, tn), lambda i, j: (i, b * nj + j))

    return pl.pallas_call(
        _merge_kernel,
        grid=(m // tm, nj),
        in_specs=[y_spec, y_spec, y_spec, w_spec(0), w_spec(1), w_spec(2), g_spec(0), g_spec(1), g_spec(2)],
        out_specs=pl.BlockSpec((tm, tn), lambda i, j: (i, j)),
        out_shape=jax.ShapeDtypeStruct((m, n), BF16),
        compiler_params=_cparams("parallel", "parallel"),
        name="merge_branches",
    )(y_a, y_b, y_c, w_branch, w_branch, w_branch, gates, gates, gates)


def _conv_kernel(h_ref, c_ref, b_ref, hp_ref, cp_ref, w_ref, bias_ref, o_ref, *, blocks_per_seq):
    u = c_ref[...].astype(F32) * h_ref[...].astype(F32)
    tm = u.shape[0]
    halo = hp_ref.shape[0]
    up = cp_ref[...].astype(F32) * hp_ref[...].astype(F32)
    first = (pl.program_id(0) % blocks_per_seq) == 0
    up = jnp.where(first, 0.0, up)
    p1 = up[halo - 1:halo, :]
    p2 = up[halo - 2:halo - 1, :]
    row = lax.broadcasted_iota(jnp.int32, (tm, 1), 0)
    u1 = jnp.where(row == 0, p1, pltpu.roll(u, 1, 0))
    u2 = jnp.where(row == 0, p2, jnp.where(row == 1, p1, pltpu.roll(u, 2, 0)))
    y = w_ref[0:1, :] * u2 + w_ref[1:2, :] * u1 + w_ref[2:3, :] * u + bias_ref[...]
    o_ref[...] = (b_ref[...].astype(F32) * y).astype(o_ref.dtype)


def short_conv(p, conv_w, conv_b, seq):
    t = p.shape[0]
    w = conv_w.shape[1]
    tm = _tile(seq, 512)
    halo = 16
    ratio = tm // halo
    prev = lambda i: jnp.maximum(i * ratio - 1, 0)
    return pl.pallas_call(
        functools.partial(_conv_kernel, blocks_per_seq=seq // tm),
        grid=(t // tm,),
        in_specs=[
            pl.BlockSpec((tm, w), lambda i: (i, 0)),
            pl.BlockSpec((tm, w), lambda i: (i, 1)),
            pl.BlockSpec((tm, w), lambda i: (i, 2)),
            pl.BlockSpec((halo, w), lambda i: (prev(i), 0)),
            pl.BlockSpec((halo, w), lambda i: (prev(i), 1)),
            pl.BlockSpec((3, w), lambda i: (0, 0)),
            pl.BlockSpec((1, w), lambda i: (0, 0)),
        ],
        out_specs=pl.BlockSpec((tm, w), lambda i: (i, 0)),
        out_shape=jax.ShapeDtypeStruct((t, w), BF16),
        compiler_params=_cparams("parallel"),
        name="short_conv",
    )(p, p, p, p, p, conv_w, conv_b.reshape(1, w))


def _block_cumsum(x_ref, cum_ref, carry_ref):
    @pl.when(pl.program_id(1) == 0)
    def _():
        carry_ref[...] = jnp.zeros_like(carry_ref)

    x = x_ref[...]
    n = x.shape[0]
    r = lax.broadcasted_iota(jnp.int32, (n, n), 0)
    c = lax.broadcasted_iota(jnp.int32, (n, n), 1)
    tri = (r >= c).astype(BF16)
    x1 = x.astype(BF16)
    r1 = x - x1.astype(F32)
    x2 = r1.astype(BF16)
    x3 = (r1 - x2.astype(F32)).astype(BF16)
    cum = (jnp.dot(tri, x1, preferred_element_type=F32) + jnp.dot(tri, x2, preferred_element_type=F32)
           + jnp.dot(tri, x3, preferred_element_type=F32)) + carry_ref[...]
    cum_ref[...] = cum
    carry_ref[...] = cum[n - 1:n, :]
    return cum


def _cumsum_kernel(x_ref, cum_ref, carry_ref):
    _block_cumsum(x_ref, cum_ref, carry_ref)


def row_cumsum(x, n_seq, seq):
    blk = _tile(seq, 256)
    nb = seq // blk
    return pl.pallas_call(
        _cumsum_kernel,
        grid=(n_seq, nb),
        in_specs=[pl.BlockSpec((blk, LANES), lambda b, i: (b * nb + i, 0))],
        out_specs=pl.BlockSpec((blk, LANES), lambda b, i: (b * nb + i, 0)),
        out_shape=jax.ShapeDtypeStruct((n_seq * seq, LANES), F32),
        scratch_shapes=[pltpu.VMEM((1, LANES), F32)],
        compiler_params=_cparams("parallel", "arbitrary"),
        name="row_cumsum",
    )(x)


FOX_GROUP = 2
LOG2_E = 1.4426950408889634


def _fox_bias_columns(cum_rows, head, value_lane, ones_lane, ones):
    lane = lax.broadcasted_iota(jnp.int32, cum_rows.shape, 1)
    c = jnp.sum(jnp.where(lane == head, cum_rows, 0.0), axis=1, keepdims=True) * LOG2_E
    hi = c.astype(BF16).astype(F32)
    r1 = c - hi
    mid = r1.astype(BF16).astype(F32)
    lo = r1 - mid
    out = jnp.where((lane >= ones_lane) & (lane < ones_lane + 3), ones, 0.0)
    out = jnp.where(lane == value_lane, hi, out)
    out = jnp.where(lane == value_lane + 1, mid, out)
    out = jnp.where(lane == value_lane + 2, lo, out)
    return out.astype(BF16)


def _fox_kernel(q_ref, k_ref, v_ref, cq_ref, ck_ref, o_ref, kaug_ref, vt_ref, qaug_ref, s_ref, m_ref, l_ref,
                acc_ref):
    hp = pl.program_id(1)
    i = pl.program_id(2)
    tq = q_ref.shape[0]
    tk = tq
    seq = k_ref.shape[0]

    @pl.when(i == 0)
    def _():
        def fill(c, carry):
            rows = pl.ds(pl.multiple_of(c * tk, tk), tk)
            for g in range(FOX_GROUP):
                cols = slice(g * HEAD_DIM, (g + 1) * HEAD_DIM)
                kaug_ref[g, rows, 0:HEAD_DIM] = k_ref[rows, cols]
                kaug_ref[g, rows, HEAD_DIM:2 * HEAD_DIM] = _fox_bias_columns(
                    ck_ref[rows, :], hp * FOX_GROUP + g, 0, 3, 1.0)
                vt_ref[g, :, rows] = v_ref[rows, cols].astype(F32).T.astype(BF16)
            return carry

        lax.fori_loop(0, seq // tk, fill, 0)

    for g in range(FOX_GROUP):
        cols = slice(g * HEAD_DIM, (g + 1) * HEAD_DIM)
        qaug_ref[g, :, 0:HEAD_DIM] = q_ref[:, cols]
        qaug_ref[g, :, HEAD_DIM:2 * HEAD_DIM] = _fox_bias_columns(cq_ref[...], hp * FOX_GROUP + g, 3, 0, -1.0)
    m_ref[...] = jnp.full_like(m_ref, NEG_INF)
    l_ref[...] = jnp.zeros_like(l_ref)
    acc_ref[...] = jnp.zeros_like(acc_ref)
    key = lax.broadcasted_iota(jnp.int32, (tk, tq), 0)
    qry = lax.broadcasted_iota(jnp.int32, (tk, tq), 1)

    def score(j, slot):
        start = pl.multiple_of(j * tk, tk)
        for g in range(FOX_GROUP):
            s_ref[slot, g] = lax.dot_general(kaug_ref[g, pl.ds(start, tk), :], qaug_ref[g],
                                             (((1,), (1,)), ((), ())), preferred_element_type=F32)

    def consume(j, slot, on_diagonal):
        start = pl.multiple_of(j * tk, tk)
        for g in range(FOX_GROUP):
            st = s_ref[slot, g]
            if on_diagonal:
                st = jnp.where(key <= qry, st, NEG_INF)
            m_old = m_ref[g]
            m_new = jnp.maximum(m_old, jnp.max(st, axis=0, keepdims=True))
            alpha = jnp.exp2(m_old - m_new)
            pt = jnp.exp2(st - m_new)
            l_ref[g] = alpha * l_ref[g] + jnp.sum(pt, axis=0, keepdims=True)
            acc_ref[g] = alpha * acc_ref[g] + jnp.dot(vt_ref[g, :, pl.ds(start, tk)], pt.astype(BF16),
                                                      preferred_element_type=F32)
            m_ref[g] = m_new

    def two_below_diagonal(p, carry):
        j = 2 * p
        score(j + 1, 1)
        consume(j, 0, False)
        score(j + 2, 0)
        consume(j + 1, 1, False)
        return carry

    score(0, 0)
    lax.fori_loop(0, i // 2, two_below_diagonal, 0)

    @pl.when(i % 2 == 0)
    def _():
        consume(i, 0, True)

    @pl.when(i % 2 == 1)
    def _():
        score(i, 1)
        consume(i - 1, 0, False)
        consume(i, 1, True)

    for g in range(FOX_GROUP):
        o_ref[:, g * HEAD_DIM:(g + 1) * HEAD_DIM] = (acc_ref[g] / l_ref[g]).T.astype(o_ref.dtype)


def fox_attention(q, kv, cum, batch, seq):
    tq = _tile(seq, 512)
    nq = seq // tq
    gw = FOX_GROUP * HEAD_DIM
    n_groups = N_HEADS // FOX_GROUP
    return pl.pallas_call(
        _fox_kernel,
        grid=(batch, n_groups, nq),
        in_specs=[
            pl.BlockSpec((tq, gw), lambda b, h, i: (b * nq + i, h)),
            pl.BlockSpec((seq, gw), lambda b, h, i: (b, h)),
            pl.BlockSpec((seq, gw), lambda b, h, i: (b, n_groups + h)),
            pl.BlockSpec((tq, LANES), lambda b, h, i: (b * nq + i, 0)),
            pl.BlockSpec((seq, LANES), lambda b, h, i: (b, 0)),
        ],
        out_specs=pl.BlockSpec((tq, gw), lambda b, h, i: (b * nq + i, h)),
        out_shape=jax.ShapeDtypeStruct((batch * seq, WIDTH), BF16),
        scratch_shapes=[pltpu.VMEM((FOX_GROUP, seq, 2 * HEAD_DIM), BF16),
                        pltpu.VMEM((FOX_GROUP, HEAD_DIM, seq), BF16),
                        pltpu.VMEM((FOX_GROUP, tq, 2 * HEAD_DIM), BF16),
                        pltpu.VMEM((2, FOX_GROUP, tq, tq), F32),
                        pltpu.VMEM((FOX_GROUP, 1, tq), F32), pltpu.VMEM((FOX_GROUP, 1, tq), F32),
                        pltpu.VMEM((FOX_GROUP, HEAD_DIM, tq), F32)],
        compiler_params=_cparams("parallel", "parallel", "arbitrary"),
        name="fox_attention",
    )(q, kv, kv, cum, cum)


def _hgrn_levels():
    sizes = []
    half = HGRN_CHUNK // 2
    while half >= HGRN_DIAG:
        sizes.append(half)
        half //= 2
    return sizes


def _hgrn_coefficients():
    c = HGRN_CHUNK
    t = np.arange(c)[:, None]
    u = np.arange(c)[None, :]
    slabs = [(u <= t), (u > t)]
    for size in _hgrn_levels():
        ref = (t // (2 * size)) * (2 * size) + size - 1
        upper = (t % (2 * size)) >= size
        slabs.append(np.where(upper, (u > ref) & (u <= t), (u > t) & (u <= ref)))
    slabs.append((u <= t) & (u // HGRN_DIAG == t // HGRN_DIAG))
    coef = np.concatenate(slabs, axis=0).astype(np.float32)
    return np.concatenate([coef, coef], axis=1)


def _hgrn_level_masks():
    c = HGRN_CHUNK
    t = np.arange(c)[:, None]
    s = np.arange(c)[None, :]
    masks = []
    for size in _hgrn_levels():
        same = (t // (2 * size)) == (s // (2 * size))
        masks.append(same & ((t % (2 * size)) >= size) & ((s % (2 * size)) < size))
    masks.append((t // HGRN_DIAG == s // HGRN_DIAG) & (s <= t))
    return np.stack(masks).astype(np.float32)


def _hgrn_placement():
    place = np.zeros((HGRN_DIAG, HEAD_DIM, HGRN_CHUNK), np.float32)
    for j in range(HGRN_DIAG):
        place[j, :, j::HGRN_DIAG] = 1.0
    return place.reshape(HGRN_DIAG * HEAD_DIM, HGRN_CHUNK)


def _hgrn_kernel(q_ref, f_ref, i_ref, g_ref, lb_ref, norm_ref, coef_ref, mask_ref, place_ref, o_ref,
                 st_ref, e_ref, kf_ref, *, layer):
    c = HGRN_CHUNK
    blk_rows = HGRN_DIAG
    n_lev = len(_hgrn_levels())
    b_in_rows = (2 + n_lev) * c

    @pl.when(pl.program_id(1) == 0)
    def _():
        st_ref[...] = jnp.zeros_like(st_ref)

    lbr = lb_ref[...]
    le = jnp.exp(lbr - jnp.max(lbr, axis=0, keepdims=True))
    lp = le / jnp.sum(le, axis=0, keepdims=True)
    lb = jnp.zeros((1, WIDTH), F32)
    for r in range(1, layer + 1):
        lb = lb + lp[r:r + 1, :]

    f = lb + (1.0 - lb) * _sigmoid(f_ref[...].astype(F32))
    g = jnp.log(f)
    kf_ref[...] = 1.0 - f
    g_hi, g_lo = _split_bf16(g)
    e_ref[...] = jnp.dot(coef_ref[...], jnp.concatenate([g_hi, g_lo], axis=0), preferred_element_type=F32)

    def head(h, carry):
        lanes = pl.ds(pl.multiple_of(h * HEAD_DIM, HEAD_DIM), HEAD_DIM)
        q = q_ref[:, lanes].astype(F32)
        k = kf_ref[:, lanes]
        v = i_ref[:, lanes]
        st = st_ref[h]
        qe = (q * jnp.exp(e_ref[0:c, lanes])).astype(BF16)
        out = lax.dot_general(qe, st.astype(BF16), (((1,), (1,)), ((), ())), preferred_element_type=F32)
        attn = jnp.zeros((c, c), F32)
        for lev in range(n_lev):
            pw = jnp.exp(e_ref[(2 + lev) * c:(3 + lev) * c, lanes])
            a = lax.dot_general((q * pw).astype(BF16), (k * pw).astype(BF16), (((1,), (1,)), ((), ())),
                                preferred_element_type=F32)
            attn = attn + jnp.where(mask_ref[lev] > 0.0, a, 0.0)
        b_in = e_ref[b_in_rows:b_in_rows + c, lanes]

        def block_row(ref, base, j):
            rows = [jnp.broadcast_to(ref[pl.ds(base + blk * blk_rows + j, 1), lanes], (blk_rows, HEAD_DIM))
                    for blk in range(c // blk_rows)]
            return jnp.concatenate(rows, axis=0)

        z = []
        for j in range(blk_rows):
            k_j = block_row(kf_ref, 0, j)
            b_j = block_row(e_ref, b_in_rows, j)
            z.append((q * k_j * jnp.exp(jnp.minimum(b_in - b_j, 0.0))).astype(BF16))
        diag = jnp.dot(jnp.concatenate(z, axis=1), place_ref[...], preferred_element_type=F32)
        attn = attn + jnp.where(mask_ref[n_lev] > 0.0, diag, 0.0)
        out = out + jnp.dot(attn.astype(BF16), v, preferred_element_type=F32)
        ke = (k * jnp.exp(e_ref[c:2 * c, lanes])).astype(BF16)
        decay = jnp.exp(e_ref[c - 1:c, lanes])
        st_ref[h] = st * decay + lax.dot_general(v, ke, (((0,), (0,)), ((), ())), preferred_element_type=F32)
        out = out * lax.rsqrt(jnp.mean(out * out, axis=-1, keepdims=True) + EPS)
        out = out * norm_ref[:, lanes] * _silu(g_ref[:, lanes].astype(F32))
        o_ref[:, lanes] = out.astype(o_ref.dtype)
        return carry

    lax.fori_loop(0, N_HEADS, head, 0, unroll=4)


def hgrn_mixer(p, lower_bounds, norm, layer, batch, seq):
    c = HGRN_CHUNK
    nc = seq // c
    depth = lower_bounds.shape[0]
    coef = jnp.asarray(_hgrn_coefficients(), BF16)
    masks = jnp.asarray(_hgrn_level_masks(), F32)
    place = jnp.asarray(_hgrn_placement(), BF16)
    n_slab = coef.shape[0] // c

    def col_spec(j):
        return pl.BlockSpec((c, WIDTH), lambda b, n: (b * nc + n, j))

    return pl.pallas_call(
        functools.partial(_hgrn_kernel, layer=layer),
        grid=(batch, nc),
        in_specs=[col_spec(0), col_spec(1), col_spec(2), col_spec(3),
                  pl.BlockSpec((depth, WIDTH), lambda b, n: (0, 0)),
                  pl.BlockSpec((1, WIDTH), lambda b, n: (0, 0)),
                  pl.BlockSpec(coef.shape, lambda b, n: (0, 0)),
                  pl.BlockSpec(masks.shape, lambda b, n: (0, 0, 0)),
                  pl.BlockSpec(place.shape, lambda b, n: (0, 0))],
        out_specs=pl.BlockSpec((c, WIDTH), lambda b, n: (b * nc + n, 0)),
        out_shape=jax.ShapeDtypeStruct((batch * seq, WIDTH), BF16),
        scratch_shapes=[pltpu.VMEM((N_HEADS, HEAD_DIM, HEAD_DIM), F32),
                        pltpu.VMEM((n_slab * c, WIDTH), F32),
                        pltpu.VMEM((c, WIDTH), F32)],
        compiler_params=_cparams("parallel", "arbitrary"),
        name="hgrn_mixer",
    )(p, p, p, p, lower_bounds, norm.reshape(1, WIDTH), coef, masks, place)


def _pad_cols(w, n):
    return jnp.pad(w, ((0, 0), (0, n - w.shape[1])))


def _round_up(n, m):
    return ((n + m - 1) // m) * m


def kernel(x, c, norm_mix, norm_ffn, w_ada, b_ada, w_in, b_gate, fox_b_f, hgrn_lower_bounds, hgrn_norm,
           conv_w, conv_b, w_branch, w_o, ffn_w1, ffn_w3, ffn_w2, router_w, router_b,
           expert_w1, expert_w3, expert_w2, norm_final):
    batch, seq, d = x.shape
    depth = w_ada.shape[0]
    t = batch * seq

    o_fox = 0
    o_ff = 3 * WIDTH
    o_hgrn = o_ff + N_HEADS
    o_conv = o_hgrn + 4 * WIDTH
    o_gate = o_conv + 3 * WIDTH

    mod_all = ada_modulation(c, w_ada, b_ada)
    xt = x.reshape(t, d)
    for layer in range(depth):
        mod = mod_all[layer]
        wl = w_in[layer]
        w_q = wl[:, o_fox:o_fox + WIDTH].astype(BF16)
        w_kv = wl[:, o_fox + WIDTH:o_ff].astype(BF16)
        w_ff = _pad_cols(wl[:, o_ff:o_hgrn], LANES).astype(BF16)
        w_hgrn = wl[:, o_hgrn:o_conv].astype(BF16)
        w_conv = wl[:, o_conv:o_gate].astype(BF16)
        w_gate = wl[:, o_gate:].astype(BF16)
        b_ff = _pad_cols(fox_b_f[layer].reshape(1, N_HEADS), LANES)

        h, lsf = norm_forget(xt, norm_mix[layer], mod, seq, w_ff, b_ff)
        q = project(h, w_q, seq, scale=HEAD_DIM ** -0.5 * LOG2_E)
        kv = project(h, w_kv, seq)
        p_hgrn = project(h, w_hgrn, seq)
        p_conv = project(h, w_conv, seq)
        gates = gate_project(h, w_gate, b_gate[layer], seq)

        y_a = hgrn_mixer(p_hgrn, hgrn_lower_bounds, hgrn_norm[layer], layer, batch, seq)
        y_b = short_conv(p_conv, conv_w[layer], conv_b[layer], seq)
        cum = row_cumsum(lsf, batch, seq)
        y_c = fox_attention(q, kv, cum, batch, seq)

        merged = merge_branches(y_a, y_b, y_c, w_branch[layer].astype(BF16), gates, seq)
        xt = residual_project(merged, w_o[layer].astype(BF16), xt, mod, seq, 2, d // 2)

        i = layer // 2
        if layer % 2 == 0:
            dff = ffn_w1.shape[2]
            dff_pad = _round_up(dff, 1024)
            w1 = _pad_cols(ffn_w1[i], dff_pad).astype(BF16)
            w3 = _pad_cols(ffn_w3[i], dff_pad).astype(BF16)
            w2 = jnp.pad(ffn_w2[i], ((0, dff_pad - dff), (0, 0))).astype(BF16)
            h2 = norm_only(xt, norm_ffn[layer], mod, seq)
            act = glu(h2, w1, w3, seq)
            xt = residual_project(act, w2, xt, mod, seq, 5, dff_pad // 4)
        else:
            last = layer == depth - 1
            xt = moe_ffn(xt, norm_ffn[layer], mod, seq, router_w[i], router_b[i],
                         expert_w1[i].astype(BF16), expert_w3[i].astype(BF16), expert_w2[i].astype(BF16), 5,
                         final_gain=norm_final if last else None)
    if depth % 2:
        xt = final_norm(xt, norm_final)
    return xt.reshape(batch, seq, d)
```

```python
import functools

import jax
import jax.numpy as jnp
import numpy as np
from jax import lax
from jax.experimental import pallas as pl
from jax.experimental.pallas import tpu as pltpu

F32 = jnp.float32
BF16 = jnp.bfloat16

N_HEADS = 8
HEAD_DIM = 128
WIDTH = N_HEADS * HEAD_DIM
N_BRANCH = 3
TOP_K = 2
EPS = 1e-6
NEG_INF = -1e30
LANES = 128
HGRN_CHUNK = 128
HGRN_DIAG = 16
NORM_ROWS = 512
VMEM_LIMIT = 56 * 1024 * 1024


def _cparams(*sem):
    return pltpu.CompilerParams(dimension_semantics=sem, vmem_limit_bytes=VMEM_LIMIT)


def _tile(n, pref):
    t = min(n, pref)
    while n % t:
        t //= 2
    return t


def _sigmoid(z):
    return 1.0 / (1.0 + jnp.exp(-z))


def _silu(z):
    return z * _sigmoid(z)


def _ada_kernel(ct_ref, w_ref, b_ref, o_ref, act_ref, *, batch):
    k = pl.program_id(2)
    tk = w_ref.shape[1]
    tn = o_ref.shape[2]

    @pl.when((pl.program_id(0) == 0) & (pl.program_id(1) == 0) & (k == 0))
    def _():
        act = _silu(ct_ref[...])
        for b in range(batch):
            act_ref[b] = jnp.broadcast_to(act[:, b:b + 1], act_ref.shape[1:])

    @pl.when(k == 0)
    def _():
        row = lax.broadcasted_iota(jnp.int32, o_ref.shape[1:], 0)
        o_ref[0] = jnp.where(row < batch, b_ref[0], 0.0)

    rows = pl.ds(pl.multiple_of(k * tk, tk), tk)
    for b in range(batch):
        a = act_ref[b, rows, :]
        for jb in range(tn // LANES):
            cols = slice(jb * LANES, (jb + 1) * LANES)
            o_ref[0, b:b + 1, cols] += jnp.sum(w_ref[0, :, cols] * a, axis=0, keepdims=True)


def ada_modulation(c, w_ada, b_ada):
    depth, d, n = w_ada.shape
    b = c.shape[0]
    rows = _round_up(b, 8)
    tn = _tile(n, 2048)
    tk = _tile(d, 1024)
    out = pl.pallas_call(
        functools.partial(_ada_kernel, batch=b),
        grid=(depth, n // tn, d // tk),
        in_specs=[
            pl.BlockSpec((d, b), lambda l, j, k: (0, 0)),
            pl.BlockSpec((1, tk, tn), lambda l, j, k: (l, k, j)),
            pl.BlockSpec((1, 1, tn), lambda l, j, k: (l, 0, j)),
        ],
        out_specs=pl.BlockSpec((1, rows, tn), lambda l, j, k: (l, 0, j)),
        out_shape=jax.ShapeDtypeStruct((depth, rows, n), F32),
        scratch_shapes=[pltpu.VMEM((b, d, LANES), F32)],
        compiler_params=_cparams("arbitrary", "arbitrary", "arbitrary"),
        name="ada_modulation",
    )(c.T, w_ada, b_ada.reshape(depth, 1, n))
    return out[:, :b].reshape(depth, b, 6, d)


def _norm_mod(x_ref, g_ref, mod_ref, shift_row, scale_row):
    x = x_ref[...]
    y = x * lax.rsqrt(jnp.mean(x * x, axis=-1, keepdims=True) + EPS) * g_ref[...]
    return y * (1.0 + mod_ref[0, scale_row:scale_row + 1, :]) + mod_ref[0, shift_row:shift_row + 1, :]


def _log_sigmoid(z):
    return jnp.minimum(z, 0.0) - jnp.log(1.0 + jnp.exp(-jnp.abs(z)))


def _norm_forget_kernel(x_ref, g_ref, mod_ref, wf_ref, bf_ref, h_ref, lsf_ref, *, shift_row, scale_row):
    h = _norm_mod(x_ref, g_ref, mod_ref, shift_row, scale_row).astype(BF16)
    h_ref[...] = h
    z = jnp.dot(h, wf_ref[...], preferred_element_type=F32) + bf_ref[...]
    lsf_ref[...] = _log_sigmoid(z)


def _split_bf16(v):
    hi = v.astype(BF16)
    lo = (v - hi.astype(F32)).astype(BF16)
    return hi, lo


def _norm_router_kernel(x_ref, g_ref, mod_ref, wr_ref, br_ref, h_ref, route_ref, *,
                        shift_row, scale_row, n_experts):
    h = _norm_mod(x_ref, g_ref, mod_ref, shift_row, scale_row)
    h_ref[...] = h
    h_hi, h_lo = _split_bf16(h)
    w_hi, w_lo = _split_bf16(wr_ref[...])
    logits = (jnp.dot(h_hi, w_hi, preferred_element_type=F32)
              + jnp.dot(h_hi, w_lo, preferred_element_type=F32)
              + jnp.dot(h_lo, w_hi, preferred_element_type=F32)) + br_ref[...]
    lane = lax.broadcasted_iota(jnp.int32, logits.shape, 1)
    lg = jnp.where(lane < n_experts, logits, -jnp.inf)
    m1 = jnp.max(lg, axis=1, keepdims=True)
    i1 = jnp.min(jnp.where(lg == m1, lane, LANES), axis=1, keepdims=True)
    lg2 = jnp.where(lane == i1, -jnp.inf, lg)
    m2 = jnp.max(lg2, axis=1, keepdims=True)
    i2 = jnp.min(jnp.where(lg2 == m2, lane, LANES), axis=1, keepdims=True)
    e = jnp.exp(m2 - m1)
    w1 = 1.0 / (1.0 + e)
    w2 = e / (1.0 + e)
    rec = jnp.where((lane == i1) | (lane == i2), 1.0, 0.0)
    rec = jnp.where(lane == n_experts, w1, rec)
    rec = jnp.where(lane == n_experts + 1, w2, rec)
    rec = jnp.where(lane == n_experts + 2, i1.astype(F32), rec)
    rec = jnp.where(lane == n_experts + 3, i2.astype(F32), rec)
    route_ref[...] = rec


def _norm_only_kernel(x_ref, g_ref, mod_ref, h_ref, *, shift_row, scale_row):
    h_ref[...] = _norm_mod(x_ref, g_ref, mod_ref, shift_row, scale_row).astype(BF16)


def _norm_call(body, x, gain, mod, seq, extra_in, extra_specs, extra_out, extra_out_specs, tm, h_dtype=BF16):
    t, d = x.shape
    per_seq = seq // tm
    in_specs = [
        pl.BlockSpec((tm, d), lambda i: (i, 0)),
        pl.BlockSpec((1, d), lambda i: (0, 0)),
        pl.BlockSpec((1, 6, d), lambda i: (i // per_seq, 0, 0)),
    ] + extra_specs
    out_shape = [jax.ShapeDtypeStruct((t, d), h_dtype)] + extra_out
    out_specs = [pl.BlockSpec((tm, d), lambda i: (i, 0))] + extra_out_specs
    return pl.pallas_call(
        body,
        grid=(t // tm,),
        in_specs=in_specs,
        out_specs=out_specs,
        out_shape=out_shape,
        compiler_params=_cparams("parallel"),
        name="norm_mod",
    )(x, gain.reshape(1, d), mod, *extra_in)


def norm_forget(x, gain, mod, seq, w_f, b_f):
    t, d = x.shape
    tm = _tile(seq, NORM_ROWS)
    body = functools.partial(_norm_forget_kernel, shift_row=0, scale_row=1)
    return _norm_call(
        body, x, gain, mod, seq, [w_f, b_f],
        [pl.BlockSpec((d, LANES), lambda i: (0, 0)), pl.BlockSpec((1, LANES), lambda i: (0, 0))],
        [jax.ShapeDtypeStruct((t, LANES), F32)], [pl.BlockSpec((tm, LANES), lambda i: (i, 0))], tm)


def norm_router(x, gain, mod, seq, w_r, b_r, n_experts):
    t, d = x.shape
    assert n_experts + 4 <= LANES
    tm = _tile(seq, NORM_ROWS)
    body = functools.partial(_norm_router_kernel, shift_row=3, scale_row=4, n_experts=n_experts)
    return _norm_call(
        body, x, gain, mod, seq, [w_r, b_r],
        [pl.BlockSpec((d, LANES), lambda i: (0, 0)), pl.BlockSpec((1, LANES), lambda i: (0, 0))],
        [jax.ShapeDtypeStruct((t, LANES), F32)], [pl.BlockSpec((tm, LANES), lambda i: (i, 0))], tm,
        h_dtype=F32)


def norm_only(x, gain, mod, seq):
    tm = _tile(seq, NORM_ROWS)
    body = functools.partial(_norm_only_kernel, shift_row=3, scale_row=4)
    return _norm_call(body, x, gain, mod, seq, [], [], [], [], tm)[0]


def _final_norm_kernel(x_ref, g_ref, o_ref):
    x = x_ref[...]
    o_ref[...] = x * lax.rsqrt(jnp.mean(x * x, axis=-1, keepdims=True) + EPS) * g_ref[...]


def final_norm(x, gain):
    t, d = x.shape
    tm = _tile(t, NORM_ROWS)
    return pl.pallas_call(
        _final_norm_kernel,
        grid=(t // tm,),
        in_specs=[pl.BlockSpec((tm, d), lambda i: (i, 0)), pl.BlockSpec((1, d), lambda i: (0, 0))],
        out_specs=pl.BlockSpec((tm, d), lambda i: (i, 0)),
        out_shape=jax.ShapeDtypeStruct((t, d), F32),
        compiler_params=_cparams("parallel"),
        name="final_norm",
    )(x, gain.reshape(1, d))


def _proj_kernel(a_ref, w_ref, o_ref, *, scale):
    acc = jnp.dot(a_ref[...], w_ref[...], preferred_element_type=F32)
    if scale != 1.0:
        acc = acc * scale
    o_ref[...] = acc.astype(o_ref.dtype)


def _gate_proj_kernel(a_ref, w_ref, b_ref, o_ref):
    acc = jnp.dot(a_ref[...], w_ref[...], preferred_element_type=F32)
    o_ref[...] = _sigmoid(acc + b_ref[...]).astype(o_ref.dtype)


def _mm_tiles(m, n, seq):
    return _tile(seq, 1024), _tile(n, 1024)


def project(a, w, seq, scale=1.0):
    m, k = a.shape
    n = w.shape[1]
    tm, tn = _mm_tiles(m, n, seq)
    return pl.pallas_call(
        functools.partial(_proj_kernel, scale=scale),
        grid=(m // tm, n // tn),
        in_specs=[pl.BlockSpec((tm, k), lambda i, j: (i, 0)), pl.BlockSpec((k, tn), lambda i, j: (0, j))],
        out_specs=pl.BlockSpec((tm, tn), lambda i, j: (i, j)),
        out_shape=jax.ShapeDtypeStruct((m, n), BF16),
        compiler_params=_cparams("parallel", "parallel"),
        name="project",
    )(a, w)


def gate_project(a, w, bias, seq):
    m, k = a.shape
    n = w.shape[1]
    tm, tn = _mm_tiles(m, n, seq)
    return pl.pallas_call(
        _gate_proj_kernel,
        grid=(m // tm, n // tn),
        in_specs=[pl.BlockSpec((tm, k), lambda i, j: (i, 0)), pl.BlockSpec((k, tn), lambda i, j: (0, j)),
                  pl.BlockSpec((1, tn), lambda i, j: (0, j))],
        out_specs=pl.BlockSpec((tm, tn), lambda i, j: (i, j)),
        out_shape=jax.ShapeDtypeStruct((m, n), BF16),
        compiler_params=_cparams("parallel", "parallel"),
        name="gate_project",
    )(a, w, bias.reshape(1, n))


def _residual_kernel(a_ref, w_ref, x_ref, mod_ref, o_ref, acc_ref, *, gate_row):
    k = pl.program_id(2)

    @pl.when(k == 0)
    def _():
        acc_ref[...] = jnp.zeros_like(acc_ref)

    acc_ref[...] += jnp.dot(a_ref[...], w_ref[...], preferred_element_type=F32)

    @pl.when(k == pl.num_programs(2) - 1)
    def _():
        o_ref[...] = x_ref[...] + mod_ref[0, gate_row:gate_row + 1, :] * acc_ref[...]


def residual_project(a, w, x, mod, seq, gate_row, tk_pref):
    m, kdim = a.shape
    n = w.shape[1]
    tm, tn = _mm_tiles(m, n, seq)
    tk = _tile(kdim, tk_pref)
    per_seq = seq // tm
    return pl.pallas_call(
        functools.partial(_residual_kernel, gate_row=gate_row),
        grid=(m // tm, n // tn, kdim // tk),
        in_specs=[
            pl.BlockSpec((tm, tk), lambda i, j, k: (i, k)),
            pl.BlockSpec((tk, tn), lambda i, j, k: (k, j)),
            pl.BlockSpec((tm, tn), lambda i, j, k: (i, j)),
            pl.BlockSpec((1, 6, tn), lambda i, j, k: (i // per_seq, 0, j)),
        ],
        out_specs=pl.BlockSpec((tm, tn), lambda i, j, k: (i, j)),
        out_shape=jax.ShapeDtypeStruct((m, n), F32),
        scratch_shapes=[pltpu.VMEM((tm, tn), F32)],
        compiler_params=_cparams("parallel", "parallel", "arbitrary"),
        name="residual_project",
    )(a, w, x, mod)


def _glu_kernel(h_ref, w1_ref, w3_ref, o_ref):
    h = h_ref[...]
    a = jnp.dot(h, w1_ref[...], preferred_element_type=F32)
    b = jnp.dot(h, w3_ref[...], preferred_element_type=F32)
    o_ref[...] = (_silu(a) * b).astype(o_ref.dtype)


def glu(h, w1, w3, seq):
    m, k = h.shape
    n = w1.shape[1]
    tm = _tile(seq, 1024)
    tn = _tile(n, 512)
    return pl.pallas_call(
        _glu_kernel,
        grid=(m // tm, n // tn),
        in_specs=[pl.BlockSpec((tm, k), lambda i, j: (i, 0)),
                  pl.BlockSpec((k, tn), lambda i, j: (0, j)),
                  pl.BlockSpec((k, tn), lambda i, j: (0, j))],
        out_specs=pl.BlockSpec((tm, tn), lambda i, j: (i, j)),
        out_shape=jax.ShapeDtypeStruct((m, n), BF16),
        compiler_params=_cparams("parallel", "parallel"),
        name="glu",
    )(h, w1, w3)


MOE_ROW_TILE = 512


def _invert_kernel(pos_ref, src_ref, *, n_tokens):
    def clear(p, carry):
        src_ref[p] = 0
        return carry

    lax.fori_loop(0, src_ref.shape[0], clear, 0, unroll=8)

    def place(t, carry):
        src_ref[pos_ref[t]] = t
        src_ref[pos_ref[n_tokens + t]] = t
        return carry

    lax.fori_loop(0, n_tokens, place, 0, unroll=8)


def invert_positions(pos, n_slots, n_tokens):
    return pl.pallas_call(
        functools.partial(_invert_kernel, n_tokens=n_tokens),
        in_specs=[pl.BlockSpec(memory_space=pltpu.SMEM)],
        out_specs=pl.BlockSpec(memory_space=pltpu.SMEM),
        out_shape=jax.ShapeDtypeStruct((n_slots,), jnp.int32),
        name="invert_positions",
    )(pos)


def _moe_up_kernel(src_ref, texp_ref, nv_ref, h_hbm, w1_ref, w3_ref, o_ref, buf, xs, sem):
    i = pl.program_id(0)
    j = pl.program_id(1)
    nv = nv_ref[0]
    tm = xs.shape[0]
    slot = i % 2

    def row_copy(tile, r, s):
        tok = src_ref[tile * tm + r]
        return pltpu.make_async_copy(h_hbm.at[pl.ds(tok, 1), :], buf.at[s, pl.ds(r, 1), :], sem.at[s])

    def start_gather(tile, s):
        def body(r, carry):
            row_copy(tile, r, s).start()
            return carry
        lax.fori_loop(0, tm, body, 0, unroll=8)

    def wait_gather(tile, s):
        def body(r, carry):
            row_copy(tile, r, s).wait()
            return carry
        lax.fori_loop(0, tm, body, 0, unroll=8)

    @pl.when((j == 0) & (i < nv))
    def _():
        @pl.when(i == 0)
        def _():
            start_gather(0, 0)

        @pl.when(i + 1 < nv)
        def _():
            start_gather(i + 1, 1 - slot)

        wait_gather(i, slot)
        xs[...] = buf[slot].astype(BF16)

    @pl.when(i < nv)
    def _():
        x = xs[...]
        a = jnp.dot(x, w1_ref[0, 0], preferred_element_type=F32)
        b = jnp.dot(x, w3_ref[0, 0], preferred_element_type=F32)
        o_ref[...] = (_silu(a) * b).astype(o_ref.dtype)

    @pl.when(i >= nv)
    def _():
        o_ref[...] = jnp.zeros_like(o_ref)


MOE_UP_COLS = 256


def moe_up_weight_tiles(w):
    e, d, f = w.shape
    tn = _tile(f, MOE_UP_COLS)
    return w.reshape(e, d, f // tn, tn).transpose(0, 2, 1, 3).astype(BF16)


def moe_up(h, w1, w3, src, tile_expert, n_valid):
    d = h.shape[1]
    nj, tn = w1.shape[1], w1.shape[3]
    f = nj * tn
    tm = MOE_ROW_TILE
    n_tiles = src.shape[0] // tm

    def w_map(i, j, src, texp, nv):
        ie = jnp.minimum(i, nv[0] - 1)
        return (texp[ie], jnp.where(i < nv[0], j, nj - 1), 0, 0)

    grid_spec = pltpu.PrefetchScalarGridSpec(
        num_scalar_prefetch=3,
        grid=(n_tiles, nj),
        in_specs=[pl.BlockSpec(memory_space=pl.ANY),
                  pl.BlockSpec((1, 1, d, tn), w_map),
                  pl.BlockSpec((1, 1, d, tn), w_map)],
        out_specs=pl.BlockSpec((tm, tn), lambda i, j, src, texp, nv: (i, j)),
        scratch_shapes=[pltpu.VMEM((2, tm, d), F32), pltpu.VMEM((tm, d), BF16),
                        pltpu.SemaphoreType.DMA((2,))],
    )
    return pl.pallas_call(
        _moe_up_kernel,
        grid_spec=grid_spec,
        out_shape=jax.ShapeDtypeStruct((n_tiles * tm, f), BF16),
        compiler_params=_cparams("arbitrary", "arbitrary"),
        name="moe_up",
    )(src, tile_expert, n_valid, h, w1, w3)


def _moe_down_kernel(texp_ref, nv_ref, a_ref, w_ref, o_ref):
    @pl.when(pl.program_id(1) < nv_ref[0])
    def _():
        o_ref[...] = jnp.dot(a_ref[...], w_ref[0], preferred_element_type=F32)

    @pl.when(pl.program_id(1) >= nv_ref[0])
    def _():
        o_ref[...] = jnp.zeros_like(o_ref)


def moe_down(act, w2, tile_expert, n_valid):
    m, f = act.shape
    d = w2.shape[2]
    tm = MOE_ROW_TILE
    tn = _tile(d, 1024)
    nj = d // tn

    def row(i, nv):
        return jnp.minimum(i, nv[0] - 1)

    grid_spec = pltpu.PrefetchScalarGridSpec(
        num_scalar_prefetch=2,
        grid=(nj, m // tm),
        in_specs=[pl.BlockSpec((tm, f), lambda j, i, texp, nv: (row(i, nv), 0)),
                  pl.BlockSpec((1, f, tn), lambda j, i, texp, nv: (texp[row(i, nv)], 0, j))],
        out_specs=pl.BlockSpec((tm, tn), lambda j, i, texp, nv: (i, j)),
    )
    return pl.pallas_call(
        _moe_down_kernel,
        grid_spec=grid_spec,
        out_shape=jax.ShapeDtypeStruct((m, d), F32),
        compiler_params=_cparams("arbitrary", "arbitrary"),
        name="moe_down",
    )(tile_expert, n_valid, act, w2)


def _moe_combine_kernel(pos_ref, y_hbm, x_ref, route_ref, mod_ref, fg_ref, o_ref, buf, sem, *,
                        n_tokens, n_experts, gate_row, final_norm):
    i = pl.program_id(0)
    tm = x_ref.shape[0]
    slot = i % 2

    def row_copy(tile, r, choice, s):
        p = pos_ref[choice * n_tokens + tile * tm + r]
        return pltpu.make_async_copy(y_hbm.at[pl.ds(p, 1), :], buf.at[s, choice, pl.ds(r, 1), :], sem.at[s])

    def start_gather(tile, s):
        def body(r, carry):
            row_copy(tile, r, 0, s).start()
            row_copy(tile, r, 1, s).start()
            return carry
        lax.fori_loop(0, tm, body, 0, unroll=8)

    def wait_gather(tile, s):
        def body(r, carry):
            row_copy(tile, r, 0, s).wait()
            row_copy(tile, r, 1, s).wait()
            return carry
        lax.fori_loop(0, tm, body, 0, unroll=8)

    @pl.when(i == 0)
    def _():
        start_gather(0, 0)

    @pl.when(i + 1 < pl.num_programs(0))
    def _():
        start_gather(i + 1, 1 - slot)

    wait_gather(i, slot)
    rec = route_ref[...]
    lane = lax.broadcasted_iota(jnp.int32, rec.shape, 1)
    w1 = jnp.sum(jnp.where(lane == n_experts, rec, 0.0), axis=1, keepdims=True)
    w2 = jnp.sum(jnp.where(lane == n_experts + 1, rec, 0.0), axis=1, keepdims=True)
    y = w1 * buf[slot, 0] + w2 * buf[slot, 1]
    out = x_ref[...] + mod_ref[0, gate_row:gate_row + 1, :] * y
    if final_norm:
        out = out * lax.rsqrt(jnp.mean(out * out, axis=-1, keepdims=True) + EPS) * fg_ref[...]
    o_ref[...] = out


def moe_combine(y, pos, x, route, mod, seq, n_experts, gate_row, final_gain=None):
    t, d = x.shape
    tm = _tile(seq, 256)
    per_seq = seq // tm
    final_norm = final_gain is not None
    fg = (final_gain if final_norm else jnp.ones((d,), F32)).reshape(1, d)
    grid_spec = pltpu.PrefetchScalarGridSpec(
        num_scalar_prefetch=1,
        grid=(t // tm,),
        in_specs=[pl.BlockSpec(memory_space=pl.ANY),
                  pl.BlockSpec((tm, d), lambda i, pos: (i, 0)),
                  pl.BlockSpec((tm, LANES), lambda i, pos: (i, 0)),
                  pl.BlockSpec((1, 6, d), lambda i, pos: (i // per_seq, 0, 0)),
                  pl.BlockSpec((1, d), lambda i, pos: (0, 0))],
        out_specs=pl.BlockSpec((tm, d), lambda i, pos: (i, 0)),
        scratch_shapes=[pltpu.VMEM((2, TOP_K, tm, d), F32), pltpu.SemaphoreType.DMA((2,))],
    )
    return pl.pallas_call(
        functools.partial(_moe_combine_kernel, n_tokens=t, n_experts=n_experts, gate_row=gate_row,
                          final_norm=final_norm),
        grid_spec=grid_spec,
        out_shape=jax.ShapeDtypeStruct((t, d), F32),
        compiler_params=_cparams("arbitrary"),
        name="moe_combine",
    )(pos, y, x, route, mod, fg)


def moe_ffn(x, gain, mod, seq, router_w, router_b, w1, w3, w2, gate_row, final_gain=None):
    t, d = x.shape
    n_experts = router_w.shape[1]
    tg = MOE_ROW_TILE
    w_r = _pad_cols(router_w, LANES)
    b_r = _pad_cols(router_b.reshape(1, n_experts), LANES)
    h, route = norm_router(x, gain, mod, seq, w_r, b_r, n_experts)
    cum = row_cumsum(route, 1, t)
    chosen = route[:, n_experts + 2:n_experts + 4].astype(jnp.int32)
    rank = jnp.take_along_axis(cum[:, :n_experts] - route[:, :n_experts], chosen, axis=1).astype(jnp.int32)
    counts = cum[t - 1, :n_experts].astype(jnp.int32)
    padded = (counts + tg - 1) // tg * tg
    ends = jnp.cumsum(padded)
    pos = (jnp.take(ends - padded, chosen) + rank).T.reshape(-1)
    n_tiles = (TOP_K * t + n_experts * (tg - 1)) // tg
    n_valid = (ends[n_experts - 1] // tg).reshape(1)
    tile_expert = jnp.minimum(jnp.searchsorted(ends, jnp.arange(n_tiles, dtype=jnp.int32) * tg, side="right"),
                              n_experts - 1).astype(jnp.int32)
    src = invert_positions(pos, n_tiles * tg, t)
    act = moe_up(h, w1, w3, src, tile_expert, n_valid)
    y = moe_down(act, w2, tile_expert, n_valid)
    return moe_combine(y, pos, x, route, mod, seq, n_experts, gate_row, final_gain)


def _merge_kernel(ya_ref, yb_ref, yc_ref, wa_ref, wb_ref, wc_ref, ga_ref, gb_ref, gc_ref, o_ref):
    out = ga_ref[...].astype(F32) * jnp.dot(ya_ref[...], wa_ref[0], preferred_element_type=F32)
    out += gb_ref[...].astype(F32) * jnp.dot(yb_ref[...], wb_ref[0], preferred_element_type=F32)
    out += gc_ref[...].astype(F32) * jnp.dot(yc_ref[...], wc_ref[0], preferred_element_type=F32)
    o_ref[...] = out.astype(o_ref.dtype)


def merge_branches(y_a, y_b, y_c, w_branch, gates, seq):
    m, k = y_a.shape
    n = w_branch.shape[2]
    tm = _tile(seq, 1024)
    tn = _tile(n, 1024)
    nj = n // tn
    y_spec = pl.BlockSpec((tm, k), lambda i, j: (i, 0))

    def w_spec(b):
        return pl.BlockSpec((1, k, tn), lambda i, j: (b, 0, j))

    def g_spec(b):
        return pl.BlockSpec((tm, tn), lambda i, j: (i, b * nj + j))

    return pl.pallas_call(
        _merge_kernel,
        grid=(m // tm, nj),
        in_specs=[y_spec, y_spec, y_spec, w_spec(0), w_spec(1), w_spec(2), g_spec(0), g_spec(1), g_spec(2)],
        out_specs=pl.BlockSpec((tm, tn), lambda i, j: (i, j)),
        out_shape=jax.ShapeDtypeStruct((m, n), BF16),
        compiler_params=_cparams("parallel", "parallel"),
        name="merge_branches",
    )(y_a, y_b, y_c, w_branch, w_branch, w_branch, gates, gates, gates)


def _conv_kernel(h_ref, c_ref, b_ref, hp_ref, cp_ref, w_ref, bias_ref, o_ref, *, blocks_per_seq):
    u = c_ref[...].astype(F32) * h_ref[...].astype(F32)
    tm = u.shape[0]
    halo = hp_ref.shape[0]
    up = cp_ref[...].astype(F32) * hp_ref[...].astype(F32)
    first = (pl.program_id(0) % blocks_per_seq) == 0
    up = jnp.where(first, 0.0, up)
    p1 = up[halo - 1:halo, :]
    p2 = up[halo - 2:halo - 1, :]
    row = lax.broadcasted_iota(jnp.int32, (tm, 1), 0)
    u1 = jnp.where(row == 0, p1, pltpu.roll(u, 1, 0))
    u2 = jnp.where(row == 0, p2, jnp.where(row == 1, p1, pltpu.roll(u, 2, 0)))
    y = w_ref[0:1, :] * u2 + w_ref[1:2, :] * u1 + w_ref[2:3, :] * u + bias_ref[...]
    o_ref[...] = (b_ref[...].astype(F32) * y).astype(o_ref.dtype)


def short_conv(p, conv_w, conv_b, seq):
    t = p.shape[0]
    w = conv_w.shape[1]
    tm = _tile(seq, 512)
    halo = 16
    ratio = tm // halo
    prev = lambda i: jnp.maximum(i * ratio - 1, 0)
    return pl.pallas_call(
        functools.partial(_conv_kernel, blocks_per_seq=seq // tm),
        grid=(t // tm,),
        in_specs=[
            pl.BlockSpec((tm, w), lambda i: (i, 0)),
            pl.BlockSpec((tm, w), lambda i: (i, 1)),
            pl.BlockSpec((tm, w), lambda i: (i, 2)),
            pl.BlockSpec((halo, w), lambda i: (prev(i), 0)),
            pl.BlockSpec((halo, w), lambda i: (prev(i), 1)),
            pl.BlockSpec((3, w), lambda i: (0, 0)),
            pl.BlockSpec((1, w), lambda i: (0, 0)),
        ],
        out_specs=pl.BlockSpec((tm, w), lambda i: (i, 0)),
        out_shape=jax.ShapeDtypeStruct((t, w), BF16),
        compiler_params=_cparams("parallel"),
        name="short_conv",
    )(p, p, p, p, p, conv_w, conv_b.reshape(1, w))


def _block_cumsum(x_ref, cum_ref, carry_ref):
    @pl.when(pl.program_id(1) == 0)
    def _():
        carry_ref[...] = jnp.zeros_like(carry_ref)

    x = x_ref[...]
    n = x.shape[0]
    r = lax.broadcasted_iota(jnp.int32, (n, n), 0)
    c = lax.broadcasted_iota(jnp.int32, (n, n), 1)
    tri = (r >= c).astype(BF16)
    x1 = x.astype(BF16)
    r1 = x - x1.astype(F32)
    x2 = r1.astype(BF16)
    x3 = (r1 - x2.astype(F32)).astype(BF16)
    cum = (jnp.dot(tri, x1, preferred_element_type=F32) + jnp.dot(tri, x2, preferred_element_type=F32)
           + jnp.dot(tri, x3, preferred_element_type=F32)) + carry_ref[...]
    cum_ref[...] = cum
    carry_ref[...] = cum[n - 1:n, :]
    return cum


def _cumsum_kernel(x_ref, cum_ref, carry_ref):
    _block_cumsum(x_ref, cum_ref, carry_ref)


def row_cumsum(x, n_seq, seq):
    blk = _tile(seq, 256)
    nb = seq // blk
    return pl.pallas_call(
        _cumsum_kernel,
        grid=(n_seq, nb),
        in_specs=[pl.BlockSpec((blk, LANES), lambda b, i: (b * nb + i, 0))],
        out_specs=pl.BlockSpec((blk, LANES), lambda b, i: (b * nb + i, 0)),
        out_shape=jax.ShapeDtypeStruct((n_seq * seq, LANES), F32),
        scratch_shapes=[pltpu.VMEM((1, LANES), F32)],
        compiler_params=_cparams("parallel", "arbitrary"),
        name="row_cumsum",
    )(x)


FOX_GROUP = 2
LOG2_E = 1.4426950408889634


def _fox_bias_columns(cum_rows, head, value_lane, ones_lane, ones):
    lane = lax.broadcasted_iota(jnp.int32, cum_rows.shape, 1)
    c = jnp.sum(jnp.where(lane == head, cum_rows, 0.0), axis=1, keepdims=True) * LOG2_E
    hi = c.astype(BF16).astype(F32)
    r1 = c - hi
    mid = r1.astype(BF16).astype(F32)
    lo = r1 - mid
    out = jnp.where((lane >= ones_lane) & (lane < ones_lane + 3), ones, 0.0)
    out = jnp.where(lane == value_lane, hi, out)
    out = jnp.where(lane == value_lane + 1, mid, out)
    out = jnp.where(lane == value_lane + 2, lo, out)
    return out.astype(BF16)


def _fox_kernel(q_ref, k_ref, v_ref, cq_ref, ck_ref, o_ref, kaug_ref, vt_ref, qaug_ref, s_ref, m_ref, l_ref,
                acc_ref):
    hp = pl.program_id(1)
    i = pl.program_id(2)
    tq = q_ref.shape[0]
    tk = tq
    seq = k_ref.shape[0]

    @pl.when(i == 0)
    def _():
        def fill(c, carry):
            rows = pl.ds(pl.multiple_of(c * tk, tk), tk)
            for g in range(FOX_GROUP):
                cols = slice(g * HEAD_DIM, (g + 1) * HEAD_DIM)
                kaug_ref[g, rows, 0:HEAD_DIM] = k_ref[rows, cols]
                kaug_ref[g, rows, HEAD_DIM:2 * HEAD_DIM] = _fox_bias_columns(
                    ck_ref[rows, :], hp * FOX_GROUP + g, 0, 3, 1.0)
                vt_ref[g, :, rows] = v_ref[rows, cols].astype(F32).T.astype(BF16)
            return carry

        lax.fori_loop(0, seq // tk, fill, 0)

    for g in range(FOX_GROUP):
        cols = slice(g * HEAD_DIM, (g + 1) * HEAD_DIM)
        qaug_ref[g, :, 0:HEAD_DIM] = q_ref[:, cols]
        qaug_ref[g, :, HEAD_DIM:2 * HEAD_DIM] = _fox_bias_columns(cq_ref[...], hp * FOX_GROUP + g, 3, 0, -1.0)
    m_ref[...] = jnp.full_like(m_ref, NEG_INF)
    l_ref[...] = jnp.zeros_like(l_ref)
    acc_ref[...] = jnp.zeros_like(acc_ref)
    key = lax.broadcasted_iota(jnp.int32, (tk, tq), 0)
    qry = lax.broadcasted_iota(jnp.int32, (tk, tq), 1)

    def score(j, slot):
        start = pl.multiple_of(j * tk, tk)
        for g in range(FOX_GROUP):
            s_ref[slot, g] = lax.dot_general(kaug_ref[g, pl.ds(start, tk), :], qaug_ref[g],
                                             (((1,), (1,)), ((), ())), preferred_element_type=F32)

    def consume(j, slot, on_diagonal):
        start = pl.multiple_of(j * tk, tk)
        for g in range(FOX_GROUP):
            st = s_ref[slot, g]
            if on_diagonal:
                st = jnp.where(key <= qry, st, NEG_INF)
            m_old = m_ref[g]
            m_new = jnp.maximum(m_old, jnp.max(st, axis=0, keepdims=True))
            alpha = jnp.exp2(m_old - m_new)
            pt = jnp.exp2(st - m_new)
            l_ref[g] = alpha * l_ref[g] + jnp.sum(pt, axis=0, keepdims=True)
            acc_ref[g] = alpha * acc_ref[g] + jnp.dot(vt_ref[g, :, pl.ds(start, tk)], pt.astype(BF16),
                                                      preferred_element_type=F32)
            m_ref[g] = m_new

    def two_below_diagonal(p, carry):
        j = 2 * p
        score(j + 1, 1)
        consume(j, 0, False)
        score(j + 2, 0)
        consume(j + 1, 1, False)
        return carry

    score(0, 0)
    lax.fori_loop(0, i // 2, two_below_diagonal, 0)

    @pl.when(i % 2 == 0)
    def _():
        consume(i, 0, True)

    @pl.when(i % 2 == 1)
    def _():
        score(i, 1)
        consume(i - 1, 0, False)
        consume(i, 1, True)

    for g in range(FOX_GROUP):
        o_ref[:, g * HEAD_DIM:(g + 1) * HEAD_DIM] = (acc_ref[g] / l_ref[g]).T.astype(o_ref.dtype)


def fox_attention(q, kv, cum, batch, seq):
    tq = _tile(seq, 512)
    nq = seq // tq
    gw = FOX_GROUP * HEAD_DIM
    n_groups = N_HEADS // FOX_GROUP
    return pl.pallas_call(
        _fox_kernel,
        grid=(batch, n_groups, nq),
        in_specs=[
            pl.BlockSpec((tq, gw), lambda b, h, i: (b * nq + i, h)),
            pl.BlockSpec((seq, gw), lambda b, h, i: (b, h)),
            pl.BlockSpec((seq, gw), lambda b, h, i: (b, n_groups + h)),
            pl.BlockSpec((tq, LANES), lambda b, h, i: (b * nq + i, 0)),
            pl.BlockSpec((seq, LANES), lambda b, h, i: (b, 0)),
        ],
        out_specs=pl.BlockSpec((tq, gw), lambda b, h, i: (b * nq + i, h)),
        out_shape=jax.ShapeDtypeStruct((batch * seq, WIDTH), BF16),
        scratch_shapes=[pltpu.VMEM((FOX_GROUP, seq, 2 * HEAD_DIM), BF16),
                        pltpu.VMEM((FOX_GROUP, HEAD_DIM, seq), BF16),
                        pltpu.VMEM((FOX_GROUP, tq, 2 * HEAD_DIM), BF16),
                        pltpu.VMEM((2, FOX_GROUP, tq, tq), F32),
                        pltpu.VMEM((FOX_GROUP, 1, tq), F32), pltpu.VMEM((FOX_GROUP, 1, tq), F32),
                        pltpu.VMEM((FOX_GROUP, HEAD_DIM, tq), F32)],
        compiler_params=_cparams("parallel", "parallel", "arbitrary"),
        name="fox_attention",
    )(q, kv, kv, cum, cum)


def _hgrn_levels():
    sizes = []
    half = HGRN_CHUNK // 2
    while half >= HGRN_DIAG:
        sizes.append(half)
        half //= 2
    return sizes


def _hgrn_coefficients():
    c = HGRN_CHUNK
    t = np.arange(c)[:, None]
    u = np.arange(c)[None, :]
    slabs = [(u <= t), (u > t)]
    for size in _hgrn_levels():
        ref = (t // (2 * size)) * (2 * size) + size - 1
        upper = (t % (2 * size)) >= size
        slabs.append(np.where(upper, (u > ref) & (u <= t), (u > t) & (u <= ref)))
    slabs.append((u <= t) & (u // HGRN_DIAG == t // HGRN_DIAG))
    coef = np.concatenate(slabs, axis=0).astype(np.float32)
    return np.concatenate([coef, coef], axis=1)


def _hgrn_level_masks():
    c = HGRN_CHUNK
    t = np.arange(c)[:, None]
    s = np.arange(c)[None, :]
    masks = []
    for size in _hgrn_levels():
        same = (t // (2 * size)) == (s // (2 * size))
        masks.append(same & ((t % (2 * size)) >= size) & ((s % (2 * size)) < size))
    masks.append((t // HGRN_DIAG == s // HGRN_DIAG) & (s <= t))
    return np.stack(masks).astype(np.float32)


def _hgrn_placement():
    place = np.zeros((HGRN_DIAG, HEAD_DIM, HGRN_CHUNK), np.float32)
    for j in range(HGRN_DIAG):
        place[j, :, j::HGRN_DIAG] = 1.0
    return place.reshape(HGRN_DIAG * HEAD_DIM, HGRN_CHUNK)


def _hgrn_kernel(q_ref, f_ref, i_ref, g_ref, lb_ref, norm_ref, coef_ref, mask_ref, place_ref, o_ref,
                 st_ref, e_ref, kf_ref, *, layer):
    c = HGRN_CHUNK
    blk_rows = HGRN_DIAG
    n_lev = len(_hgrn_levels())
    b_in_rows = (2 + n_lev) * c

    @pl.when(pl.program_id(1) == 0)
    def _():
        st_ref[...] = jnp.zeros_like(st_ref)

    lbr = lb_ref[...]
    le = jnp.exp(lbr - jnp.max(lbr, axis=0, keepdims=True))
    lp = le / jnp.sum(le, axis=0, keepdims=True)
    lb = jnp.zeros((1, WIDTH), F32)
    for r in range(1, layer + 1):
        lb = lb + lp[r:r + 1, :]

    f = lb + (1.0 - lb) * _sigmoid(f_ref[...].astype(F32))
    g = jnp.log(f)
    kf_ref[...] = 1.0 - f
    g_hi, g_lo = _split_bf16(g)
    e_ref[...] = jnp.dot(coef_ref[...], jnp.concatenate([g_hi, g_lo], axis=0), preferred_element_type=F32)

    def head(h, carry):
        lanes = pl.ds(pl.multiple_of(h * HEAD_DIM, HEAD_DIM), HEAD_DIM)
        q = q_ref[:, lanes].astype(F32)
        k = kf_ref[:, lanes]
        v = i_ref[:, lanes]
        st = st_ref[h]
        qe = (q * jnp.exp(e_ref[0:c, lanes])).astype(BF16)
        out = lax.dot_general(qe, st.astype(BF16), (((1,), (1,)), ((), ())), preferred_element_type=F32)
        attn = jnp.zeros((c, c), F32)
        for lev in range(n_lev):
            pw = jnp.exp(e_ref[(2 + lev) * c:(3 + lev) * c, lanes])
            a = lax.dot_general((q * pw).astype(BF16), (k * pw).astype(BF16), (((1,), (1,)), ((), ())),
                                preferred_element_type=F32)
            attn = attn + jnp.where(mask_ref[lev] > 0.0, a, 0.0)
        b_in = e_ref[b_in_rows:b_in_rows + c, lanes]

        def block_row(ref, base, j):
            rows = [jnp.broadcast_to(ref[pl.ds(base + blk * blk_rows + j, 1), lanes], (blk_rows, HEAD_DIM))
                    for blk in range(c // blk_rows)]
            return jnp.concatenate(rows, axis=0)

        z = []
        for j in range(blk_rows):
            k_j = block_row(kf_ref, 0, j)
            b_j = block_row(e_ref, b_in_rows, j)
            z.append((q * k_j * jnp.exp(jnp.minimum(b_in - b_j, 0.0))).astype(BF16))
        diag = jnp.dot(jnp.concatenate(z, axis=1), place_ref[...], preferred_element_type=F32)
        attn = attn + jnp.where(mask_ref[n_lev] > 0.0, diag, 0.0)
        out = out + jnp.dot(attn.astype(BF16), v, preferred_element_type=F32)
        ke = (k * jnp.exp(e_ref[c:2 * c, lanes])).astype(BF16)
        decay = jnp.exp(e_ref[c - 1:c, lanes])
        st_ref[h] = st * decay + lax.dot_general(v, ke, (((0,), (0,)), ((), ())), preferred_element_type=F32)
        out = out * lax.rsqrt(jnp.mean(out * out, axis=-1, keepdims=True) + EPS)
        out = out * norm_ref[:, lanes] * _silu(g_ref[:, lanes].astype(F32))
        o_ref[:, lanes] = out.astype(o_ref.dtype)
        return carry

    lax.fori_loop(0, N_HEADS, head, 0, unroll=4)


def hgrn_mixer(p, lower_bounds, norm, layer, batch, seq):
    c = HGRN_CHUNK
    nc = seq // c
    depth = lower_bounds.shape[0]
    coef = jnp.asarray(_hgrn_coefficients(), BF16)
    masks = jnp.asarray(_hgrn_level_masks(), F32)
    place = jnp.asarray(_hgrn_placement(), BF16)
    n_slab = coef.shape[0] // c

    def col_spec(j):
        return pl.BlockSpec((c, WIDTH), lambda b, n: (b * nc + n, j))

    return pl.pallas_call(
        functools.partial(_hgrn_kernel, layer=layer),
        grid=(batch, nc),
        in_specs=[col_spec(0), col_spec(1), col_spec(2), col_spec(3),
                  pl.BlockSpec((depth, WIDTH), lambda b, n: (0, 0)),
                  pl.BlockSpec((1, WIDTH), lambda b, n: (0, 0)),
                  pl.BlockSpec(coef.shape, lambda b, n: (0, 0)),
                  pl.BlockSpec(masks.shape, lambda b, n: (0, 0, 0)),
                  pl.BlockSpec(place.shape, lambda b, n: (0, 0))],
        out_specs=pl.BlockSpec((c, WIDTH), lambda b, n: (b * nc + n, 0)),
        out_shape=jax.ShapeDtypeStruct((batch * seq, WIDTH), BF16),
        scratch_shapes=[pltpu.VMEM((N_HEADS, HEAD_DIM, HEAD_DIM), F32),
                        pltpu.VMEM((n_slab * c, WIDTH), F32),
                        pltpu.VMEM((c, WIDTH), F32)],
        compiler_params=_cparams("parallel", "arbitrary"),
        name="hgrn_mixer",
    )(p, p, p, p, lower_bounds, norm.reshape(1, WIDTH), coef, masks, place)


def _pad_cols(w, n):
    return jnp.pad(w, ((0, 0), (0, n - w.shape[1])))


def _round_up(n, m):
    return ((n + m - 1) // m) * m


def kernel(x, c, norm_mix, norm_ffn, w_ada, b_ada, w_in, b_gate, fox_b_f, hgrn_lower_bounds, hgrn_norm,
           conv_w, conv_b, w_branch, w_o, ffn_w1, ffn_w3, ffn_w2, router_w, router_b,
           expert_w1, expert_w3, expert_w2, norm_final):
    batch, seq, d = x.shape
    depth = w_ada.shape[0]
    t = batch * seq

    o_fox = 0
    o_ff = 3 * WIDTH
    o_hgrn = o_ff + N_HEADS
    o_conv = o_hgrn + 4 * WIDTH
    o_gate = o_conv + 3 * WIDTH

    mod_all = ada_modulation(c, w_ada, b_ada)
    xt = x.reshape(t, d)
    for layer in range(depth):
        mod = mod_all[layer]
        wl = w_in[layer]
        w_q = wl[:, o_fox:o_fox + WIDTH].astype(BF16)
        w_kv = wl[:, o_fox + WIDTH:o_ff].astype(BF16)
        w_ff = _pad_cols(wl[:, o_ff:o_hgrn], LANES).astype(BF16)
        w_hgrn = wl[:, o_hgrn:o_conv].astype(BF16)
        w_conv = wl[:, o_conv:o_gate].astype(BF16)
        w_gate = wl[:, o_gate:].astype(BF16)
        b_ff = _pad_cols(fox_b_f[layer].reshape(1, N_HEADS), LANES)

        h, lsf = norm_forget(xt, norm_mix[layer], mod, seq, w_ff, b_ff)
        q = project(h, w_q, seq, scale=HEAD_DIM ** -0.5 * LOG2_E)
        kv = project(h, w_kv, seq)
        p_hgrn = project(h, w_hgrn, seq)
        p_conv = project(h, w_conv, seq)
        gates = gate_project(h, w_gate, b_gate[layer], seq)

        y_a = hgrn_mixer(p_hgrn, hgrn_lower_bounds, hgrn_norm[layer], layer, batch, seq)
        y_b = short_conv(p_conv, conv_w[layer], conv_b[layer], seq)
        cum = row_cumsum(lsf, batch, seq)
        y_c = fox_attention(q, kv, cum, batch, seq)

        merged = merge_branches(y_a, y_b, y_c, w_branch[layer].astype(BF16), gates, seq)
        xt = residual_project(merged, w_o[layer].astype(BF16), xt, mod, seq, 2, d // 2)

        i = layer // 2
        if layer % 2 == 0:
            dff = ffn_w1.shape[2]
            dff_pad = _round_up(dff, 1024)
            w1 = _pad_cols(ffn_w1[i], dff_pad).astype(BF16)
            w3 = _pad_cols(ffn_w3[i], dff_pad).astype(BF16)
            w2 = jnp.pad(ffn_w2[i], ((0, dff_pad - dff), (0, 0))).astype(BF16)
            h2 = norm_only(xt, norm_ffn[layer], mod, seq)
            act = glu(h2, w1, w3, seq)
            xt = residual_project(act, w2, xt, mod, seq, 5, dff_pad // 4)
        else:
            last = layer == depth - 1
            xt = moe_ffn(xt, norm_ffn[layer], mod, seq, router_w[i], router_b[i],
                         moe_up_weight_tiles(expert_w1[i]), moe_up_weight_tiles(expert_w3[i]),
                         expert_w2[i].astype(BF16), 5,
                         final_gain=norm_final if last else None)
    if depth % 2:
        xt = final_norm(xt, norm_final)
    return xt.reshape(batch, seq, d)
```

```python
import functools

import jax
import jax.numpy as jnp
import numpy as np
from jax import lax
from jax.experimental import pallas as pl
from jax.experimental.pallas import tpu as pltpu

F32 = jnp.float32
BF16 = jnp.bfloat16
FP8 = jnp.float8_e4m3fn
FP8_TARGET = 256.0

N_HEADS = 8
HEAD_DIM = 128
WIDTH = N_HEADS * HEAD_DIM
N_BRANCH = 3
TOP_K = 2
EPS = 1e-6
NEG_INF = -1e30
LANES = 128
HGRN_CHUNK = 128
HGRN_DIAG = 16
NORM_ROWS = 512
VMEM_LIMIT = 56 * 1024 * 1024


def _cparams(*sem):
    return pltpu.CompilerParams(dimension_semantics=sem, vmem_limit_bytes=VMEM_LIMIT)


def _tile(n, pref):
    t = min(n, pref)
    while n % t:
        t //= 2
    return t


def _sigmoid(z):
    return 1.0 / (1.0 + jnp.exp(-z))


def _silu(z):
    return z * _sigmoid(z)


def _ada_kernel(ct_ref, wa_ref, wb_ref, b_ref, o_ref, act_ref, *, batch):
    k = pl.program_id(2)
    half = wa_ref.shape[1]
    tn = o_ref.shape[2]

    @pl.when((pl.program_id(0) == 0) & (pl.program_id(1) == 0) & (k == 0))
    def _():
        act = _silu(ct_ref[...])
        for b in range(batch):
            act_ref[b] = jnp.broadcast_to(act[:, b:b + 1], act_ref.shape[1:])

    @pl.when(k == 0)
    def _():
        row = lax.broadcasted_iota(jnp.int32, o_ref.shape[1:], 0)
        o_ref[0] = jnp.where(row < batch, b_ref[0], 0.0)

    for part, w_ref in enumerate((wa_ref, wb_ref)):
        rows = pl.ds(pl.multiple_of((2 * k + part) * half, half), half)
        for b in range(batch):
            a = act_ref[b, rows, :]
            for jb in range(tn // LANES):
                cols = slice(jb * LANES, (jb + 1) * LANES)
                o_ref[0, b:b + 1, cols] += jnp.sum(w_ref[0, :, cols] * a, axis=0, keepdims=True)


def ada_modulation(c, w_ada, b_ada):
    depth, d, n = w_ada.shape
    b = c.shape[0]
    rows = _round_up(b, 8)
    tn = _tile(n, 2048)
    tk = _tile(d, 1024)
    out = pl.pallas_call(
        functools.partial(_ada_kernel, batch=b),
        grid=(depth, n // tn, d // tk),
        in_specs=[
            pl.BlockSpec((d, b), lambda l, j, k: (0, 0)),
            pl.BlockSpec((1, tk // 2, tn), lambda l, j, k: (l, 2 * k, j)),
            pl.BlockSpec((1, tk // 2, tn), lambda l, j, k: (l, 2 * k + 1, j)),
            pl.BlockSpec((1, 1, tn), lambda l, j, k: (l, 0, j)),
        ],
        out_specs=pl.BlockSpec((1, rows, tn), lambda l, j, k: (l, 0, j)),
        out_shape=jax.ShapeDtypeStruct((depth, rows, n), F32),
        scratch_shapes=[pltpu.VMEM((b, d, LANES), F32)],
        compiler_params=_cparams("arbitrary", "arbitrary", "arbitrary"),
        name="ada_modulation",
    )(c.T, w_ada, w_ada, b_ada.reshape(depth, 1, n))
    return out[:, :b].reshape(depth, b, 6, d)


def _norm_mod(x_ref, g_ref, mod_ref, shift_row, scale_row):
    x = x_ref[...]
    y = x * lax.rsqrt(jnp.mean(x * x, axis=-1, keepdims=True) + EPS) * g_ref[...]
    return y * (1.0 + mod_ref[0, scale_row:scale_row + 1, :]) + mod_ref[0, shift_row:shift_row + 1, :]


def _log_sigmoid(z):
    return jnp.minimum(z, 0.0) - jnp.log(1.0 + jnp.exp(-jnp.abs(z)))


def _norm_forget_kernel(x_ref, g_ref, mod_ref, wf_ref, bf_ref, h_ref, lsf_ref, *, shift_row, scale_row):
    h = _norm_mod(x_ref, g_ref, mod_ref, shift_row, scale_row).astype(BF16)
    h_ref[...] = h
    z = jnp.dot(h, wf_ref[...], preferred_element_type=F32) + bf_ref[...]
    lsf_ref[...] = _log_sigmoid(z)


def _split_bf16(v):
    hi = v.astype(BF16)
    lo = (v - hi.astype(F32)).astype(BF16)
    return hi, lo


def _norm_router_kernel(x_ref, g_ref, mod_ref, wr_ref, br_ref, h_ref, route_ref, *,
                        shift_row, scale_row, n_experts):
    h = _norm_mod(x_ref, g_ref, mod_ref, shift_row, scale_row)
    h_ref[...] = h
    h_hi, h_lo = _split_bf16(h)
    w_hi, w_lo = _split_bf16(wr_ref[...])
    logits = (jnp.dot(h_hi, w_hi, preferred_element_type=F32)
              + jnp.dot(h_hi, w_lo, preferred_element_type=F32)
              + jnp.dot(h_lo, w_hi, preferred_element_type=F32)) + br_ref[...]
    lane = lax.broadcasted_iota(jnp.int32, logits.shape, 1)
    lg = jnp.where(lane < n_experts, logits, -jnp.inf)
    m1 = jnp.max(lg, axis=1, keepdims=True)
    i1 = jnp.min(jnp.where(lg == m1, lane, LANES), axis=1, keepdims=True)
    lg2 = jnp.where(lane == i1, -jnp.inf, lg)
    m2 = jnp.max(lg2, axis=1, keepdims=True)
    i2 = jnp.min(jnp.where(lg2 == m2, lane, LANES), axis=1, keepdims=True)
    e = jnp.exp(m2 - m1)
    w1 = 1.0 / (1.0 + e)
    w2 = e / (1.0 + e)
    rec = jnp.where((lane == i1) | (lane == i2), 1.0, 0.0)
    rec = jnp.where(lane == n_experts, w1, rec)
    rec = jnp.where(lane == n_experts + 1, w2, rec)
    rec = jnp.where(lane == n_experts + 2, i1.astype(F32), rec)
    rec = jnp.where(lane == n_experts + 3, i2.astype(F32), rec)
    route_ref[...] = rec


def _norm_only_kernel(x_ref, g_ref, mod_ref, h_ref, *, shift_row, scale_row):
    h_ref[...] = _norm_mod(x_ref, g_ref, mod_ref, shift_row, scale_row).astype(BF16)


def _norm_call(body, x, gain, mod, seq, extra_in, extra_specs, extra_out, extra_out_specs, tm, h_dtype=BF16):
    t, d = x.shape
    per_seq = seq // tm
    in_specs = [
        pl.BlockSpec((tm, d), lambda i: (i, 0)),
        pl.BlockSpec((1, d), lambda i: (0, 0)),
        pl.BlockSpec((1, 6, d), lambda i: (i // per_seq, 0, 0)),
    ] + extra_specs
    out_shape = [jax.ShapeDtypeStruct((t, d), h_dtype)] + extra_out
    out_specs = [pl.BlockSpec((tm, d), lambda i: (i, 0))] + extra_out_specs
    return pl.pallas_call(
        body,
        grid=(t // tm,),
        in_specs=in_specs,
        out_specs=out_specs,
        out_shape=out_shape,
        compiler_params=_cparams("parallel"),
        name="norm_mod",
    )(x, gain.reshape(1, d), mod, *extra_in)


def norm_forget(x, gain, mod, seq, w_f, b_f):
    t, d = x.shape
    tm = _tile(seq, NORM_ROWS)
    body = functools.partial(_norm_forget_kernel, shift_row=0, scale_row=1)
    return _norm_call(
        body, x, gain, mod, seq, [w_f, b_f],
        [pl.BlockSpec((d, LANES), lambda i: (0, 0)), pl.BlockSpec((1, LANES), lambda i: (0, 0))],
        [jax.ShapeDtypeStruct((t, LANES), F32)], [pl.BlockSpec((tm, LANES), lambda i: (i, 0))], tm)


def norm_router(x, gain, mod, seq, w_r, b_r, n_experts):
    t, d = x.shape
    assert n_experts + 4 <= LANES
    tm = _tile(seq, NORM_ROWS)
    body = functools.partial(_norm_router_kernel, shift_row=3, scale_row=4, n_experts=n_experts)
    return _norm_call(
        body, x, gain, mod, seq, [w_r, b_r],
        [pl.BlockSpec((d, LANES), lambda i: (0, 0)), pl.BlockSpec((1, LANES), lambda i: (0, 0))],
        [jax.ShapeDtypeStruct((t, LANES), F32)], [pl.BlockSpec((tm, LANES), lambda i: (i, 0))], tm,
        h_dtype=F32)


def norm_only(x, gain, mod, seq):
    tm = _tile(seq, NORM_ROWS)
    body = functools.partial(_norm_only_kernel, shift_row=3, scale_row=4)
    return _norm_call(body, x, gain, mod, seq, [], [], [], [], tm)[0]


def _final_norm_kernel(x_ref, g_ref, o_ref):
    x = x_ref[...]
    o_ref[...] = x * lax.rsqrt(jnp.mean(x * x, axis=-1, keepdims=True) + EPS) * g_ref[...]


def final_norm(x, gain):
    t, d = x.shape
    tm = _tile(t, NORM_ROWS)
    return pl.pallas_call(
        _final_norm_kernel,
        grid=(t // tm,),
        in_specs=[pl.BlockSpec((tm, d), lambda i: (i, 0)), pl.BlockSpec((1, d), lambda i: (0, 0))],
        out_specs=pl.BlockSpec((tm, d), lambda i: (i, 0)),
        out_shape=jax.ShapeDtypeStruct((t, d), F32),
        compiler_params=_cparams("parallel"),
        name="final_norm",
    )(x, gain.reshape(1, d))


def _proj_kernel(a_ref, w_ref, o_ref, *, scale):
    acc = jnp.dot(a_ref[...], w_ref[...], preferred_element_type=F32)
    if scale != 1.0:
        acc = acc * scale
    o_ref[...] = acc.astype(o_ref.dtype)


def _pow2_scale(amax):
    return jnp.exp2(jnp.floor(jnp.log2(FP8_TARGET / jnp.maximum(amax, 1e-30))))


def fp8_weight(w):
    scale = _pow2_scale(jnp.max(jnp.abs(w)))
    return (w * scale).astype(FP8), scale


def _gate_proj_kernel(a_ref, w_ref, b_ref, ws_ref, o_ref, a8_ref, inv_ref):
    @pl.when(pl.program_id(1) == 0)
    def _():
        a = a_ref[...].astype(F32)
        s = _pow2_scale(jnp.max(jnp.abs(a), axis=1, keepdims=True))
        a8_ref[...] = (a * s).astype(FP8)
        inv_ref[...] = 1.0 / (s * ws_ref[...])

    acc = jnp.dot(a8_ref[...], w_ref[...], preferred_element_type=F32)
    o_ref[...] = _sigmoid(acc * inv_ref[...] + b_ref[...]).astype(o_ref.dtype)


def _mm_tiles(m, n, seq):
    return _tile(seq, 1024), _tile(n, 1024)


def project(a, w, seq, scale=1.0):
    m, k = a.shape
    n = w.shape[1]
    tm, tn = _mm_tiles(m, n, seq)
    return pl.pallas_call(
        functools.partial(_proj_kernel, scale=scale),
        grid=(m // tm, n // tn),
        in_specs=[pl.BlockSpec((tm, k), lambda i, j: (i, 0)), pl.BlockSpec((k, tn), lambda i, j: (0, j))],
        out_specs=pl.BlockSpec((tm, tn), lambda i, j: (i, j)),
        out_shape=jax.ShapeDtypeStruct((m, n), BF16),
        compiler_params=_cparams("parallel", "parallel"),
        name="project",
    )(a, w)


def gate_project(a, w, bias, seq):
    m, k = a.shape
    n = w.shape[1]
    tm, tn = _mm_tiles(m, n, seq)
    w8, w_scale = fp8_weight(w)
    return pl.pallas_call(
        _gate_proj_kernel,
        grid=(m // tm, n // tn),
        in_specs=[pl.BlockSpec((tm, k), lambda i, j: (i, 0)), pl.BlockSpec((k, tn), lambda i, j: (0, j)),
                  pl.BlockSpec((1, tn), lambda i, j: (0, j)), pl.BlockSpec((1, 1), lambda i, j: (0, 0))],
        out_specs=pl.BlockSpec((tm, tn), lambda i, j: (i, j)),
        out_shape=jax.ShapeDtypeStruct((m, n), BF16),
        scratch_shapes=[pltpu.VMEM((tm, k), FP8), pltpu.VMEM((tm, 1), F32)],
        compiler_params=_cparams("parallel", "arbitrary"),
        name="gate_project",
    )(a, w8, bias.reshape(1, n), w_scale.reshape(1, 1))


def _residual_kernel(a_ref, w_ref, x_ref, mod_ref, o_ref, acc_ref, *, gate_row):
    k = pl.program_id(2)

    @pl.when(k == 0)
    def _():
        acc_ref[...] = jnp.zeros_like(acc_ref)

    acc_ref[...] += jnp.dot(a_ref[...], w_ref[...], preferred_element_type=F32)

    @pl.when(k == pl.num_programs(2) - 1)
    def _():
        o_ref[...] = x_ref[...] + mod_ref[0, gate_row:gate_row + 1, :] * acc_ref[...]


def residual_project(a, w, x, mod, seq, gate_row, tk_pref):
    m, kdim = a.shape
    n = w.shape[1]
    tm, tn = _mm_tiles(m, n, seq)
    tk = _tile(kdim, tk_pref)
    per_seq = seq // tm
    return pl.pallas_call(
        functools.partial(_residual_kernel, gate_row=gate_row),
        grid=(m // tm, n // tn, kdim // tk),
        in_specs=[
            pl.BlockSpec((tm, tk), lambda i, j, k: (i, k)),
            pl.BlockSpec((tk, tn), lambda i, j, k: (k, j)),
            pl.BlockSpec((tm, tn), lambda i, j, k: (i, j)),
            pl.BlockSpec((1, 6, tn), lambda i, j, k: (i // per_seq, 0, j)),
        ],
        out_specs=pl.BlockSpec((tm, tn), lambda i, j, k: (i, j)),
        out_shape=jax.ShapeDtypeStruct((m, n), F32),
        scratch_shapes=[pltpu.VMEM((tm, tn), F32)],
        compiler_params=_cparams("parallel", "parallel", "arbitrary"),
        name="residual_project",
    )(a, w, x, mod)


def _glu_kernel(h_ref, w1_ref, w3_ref, o_ref):
    h = h_ref[...]
    a = jnp.dot(h, w1_ref[...], preferred_element_type=F32)
    b = jnp.dot(h, w3_ref[...], preferred_element_type=F32)
    o_ref[...] = (_silu(a) * b).astype(o_ref.dtype)


def glu(h, w1, w3, seq):
    m, k = h.shape
    n = w1.shape[1]
    tm = _tile(seq, 1024)
    tn = _tile(n, 512)
    return pl.pallas_call(
        _glu_kernel,
        grid=(m // tm, n // tn),
        in_specs=[pl.BlockSpec((tm, k), lambda i, j: (i, 0)),
                  pl.BlockSpec((k, tn), lambda i, j: (0, j)),
                  pl.BlockSpec((k, tn), lambda i, j: (0, j))],
        out_specs=pl.BlockSpec((tm, tn), lambda i, j: (i, j)),
        out_shape=jax.ShapeDtypeStruct((m, n), BF16),
        compiler_params=_cparams("parallel", "parallel"),
        name="glu",
    )(h, w1, w3)


MOE_ROW_TILE = 512


def _invert_kernel(pos_ref, src_ref, *, n_tokens):
    def clear(p, carry):
        src_ref[p] = 0
        return carry

    lax.fori_loop(0, src_ref.shape[0], clear, 0, unroll=8)

    def place(t, carry):
        src_ref[pos_ref[t]] = t
        src_ref[pos_ref[n_tokens + t]] = t
        return carry

    lax.fori_loop(0, n_tokens, place, 0, unroll=8)


def invert_positions(pos, n_slots, n_tokens):
    return pl.pallas_call(
        functools.partial(_invert_kernel, n_tokens=n_tokens),
        in_specs=[pl.BlockSpec(memory_space=pltpu.SMEM)],
        out_specs=pl.BlockSpec(memory_space=pltpu.SMEM),
        out_shape=jax.ShapeDtypeStruct((n_slots,), jnp.int32),
        name="invert_positions",
    )(pos)


def _moe_up_kernel(src_ref, texp_ref, nv_ref, h_hbm, w1_ref, w3_ref, ws_ref, o_ref, buf, xs, inv, sem):
    i = pl.program_id(0)
    j = pl.program_id(1)
    nv = nv_ref[0]
    tm = xs.shape[0]
    slot = i % 2

    def row_copy(tile, r, s):
        tok = src_ref[tile * tm + r]
        return pltpu.make_async_copy(h_hbm.at[pl.ds(tok, 1), :], buf.at[s, pl.ds(r, 1), :], sem.at[s])

    def start_gather(tile, s):
        def body(r, carry):
            row_copy(tile, r, s).start()
            return carry
        lax.fori_loop(0, tm, body, 0, unroll=8)

    def wait_gather(tile, s):
        def body(r, carry):
            row_copy(tile, r, s).wait()
            return carry
        lax.fori_loop(0, tm, body, 0, unroll=8)

    @pl.when((j == 0) & (i < nv))
    def _():
        @pl.when(i == 0)
        def _():
            start_gather(0, 0)

        @pl.when(i + 1 < nv)
        def _():
            start_gather(i + 1, 1 - slot)

        wait_gather(i, slot)
        rows = buf[slot]
        s = _pow2_scale(jnp.max(jnp.abs(rows), axis=1, keepdims=True))
        xs[...] = (rows * s).astype(FP8)
        inv[...] = 1.0 / s

    @pl.when(i < nv)
    def _():
        x = xs[...]
        a = jnp.dot(x, w1_ref[0], preferred_element_type=F32) * (inv[...] / ws_ref[:, 0:1])
        b = jnp.dot(x, w3_ref[0], preferred_element_type=F32) * (inv[...] / ws_ref[:, 1:2])
        o_ref[...] = (_silu(a) * b).astype(o_ref.dtype)

    @pl.when(i >= nv)
    def _():
        o_ref[...] = jnp.zeros_like(o_ref)


MOE_UP_COLS = 256


def moe_up(h, w1, w3, w_scale, src, tile_expert, n_valid):
    d = h.shape[1]
    f = w1.shape[2]
    tm = MOE_ROW_TILE
    n_tiles = src.shape[0] // tm
    tn = _tile(f, MOE_UP_COLS)
    nj = f // tn

    def w_map(i, j, src, texp, nv):
        ie = jnp.minimum(i, nv[0] - 1)
        return (texp[ie], 0, jnp.where(i < nv[0], j, nj - 1))

    grid_spec = pltpu.PrefetchScalarGridSpec(
        num_scalar_prefetch=3,
        grid=(n_tiles, nj),
        in_specs=[pl.BlockSpec(memory_space=pl.ANY),
                  pl.BlockSpec((1, d, tn), w_map),
                  pl.BlockSpec((1, d, tn), w_map),
                  pl.BlockSpec((1, 2), lambda i, j, src, texp, nv: (0, 0))],
        out_specs=pl.BlockSpec((tm, tn), lambda i, j, src, texp, nv: (i, j)),
        scratch_shapes=[pltpu.VMEM((2, tm, d), F32), pltpu.VMEM((tm, d), FP8), pltpu.VMEM((tm, 1), F32),
                        pltpu.SemaphoreType.DMA((2,))],
    )
    return pl.pallas_call(
        _moe_up_kernel,
        grid_spec=grid_spec,
        out_shape=jax.ShapeDtypeStruct((n_tiles * tm, f), BF16),
        compiler_params=_cparams("arbitrary", "arbitrary"),
        name="moe_up",
    )(src, tile_expert, n_valid, h, w1, w3, w_scale)


def _moe_down_kernel(texp_ref, nv_ref, a_ref, w_ref, o_ref):
    @pl.when(pl.program_id(1) < nv_ref[0])
    def _():
        o_ref[...] = jnp.dot(a_ref[...], w_ref[0], preferred_element_type=F32)

    @pl.when(pl.program_id(1) >= nv_ref[0])
    def _():
        o_ref[...] = jnp.zeros_like(o_ref)


def moe_down(act, w2, tile_expert, n_valid):
    m, f = act.shape
    d = w2.shape[2]
    tm = MOE_ROW_TILE
    tn = _tile(d, 1024)
    nj = d // tn

    def row(i, nv):
        return jnp.minimum(i, nv[0] - 1)

    grid_spec = pltpu.PrefetchScalarGridSpec(
        num_scalar_prefetch=2,
        grid=(nj, m // tm),
        in_specs=[pl.BlockSpec((tm, f), lambda j, i, texp, nv: (row(i, nv), 0)),
                  pl.BlockSpec((1, f, tn), lambda j, i, texp, nv: (texp[row(i, nv)], 0, j))],
        out_specs=pl.BlockSpec((tm, tn), lambda j, i, texp, nv: (i, j)),
    )
    return pl.pallas_call(
        _moe_down_kernel,
        grid_spec=grid_spec,
        out_shape=jax.ShapeDtypeStruct((m, d), F32),
        compiler_params=_cparams("arbitrary", "arbitrary"),
        name="moe_down",
    )(tile_expert, n_valid, act, w2)


def _moe_combine_kernel(pos_ref, y_hbm, x_ref, route_ref, mod_ref, fg_ref, o_ref, buf, sem, *,
                        n_tokens, n_experts, gate_row, final_norm):
    i = pl.program_id(0)
    tm = x_ref.shape[0]
    slot = i % 2

    def row_copy(tile, r, choice, s):
        p = pos_ref[choice * n_tokens + tile * tm + r]
        return pltpu.make_async_copy(y_hbm.at[pl.ds(p, 1), :], buf.at[s, choice, pl.ds(r, 1), :], sem.at[s])

    def start_gather(tile, s):
        def body(r, carry):
            row_copy(tile, r, 0, s).start()
            row_copy(tile, r, 1, s).start()
            return carry
        lax.fori_loop(0, tm, body, 0, unroll=8)

    def wait_gather(tile, s):
        def body(r, carry):
            row_copy(tile, r, 0, s).wait()
            row_copy(tile, r, 1, s).wait()
            return carry
        lax.fori_loop(0, tm, body, 0, unroll=8)

    @pl.when(i == 0)
    def _():
        start_gather(0, 0)

    @pl.when(i + 1 < pl.num_programs(0))
    def _():
        start_gather(i + 1, 1 - slot)

    wait_gather(i, slot)
    rec = route_ref[...]
    lane = lax.broadcasted_iota(jnp.int32, rec.shape, 1)
    w1 = jnp.sum(jnp.where(lane == n_experts, rec, 0.0), axis=1, keepdims=True)
    w2 = jnp.sum(jnp.where(lane == n_experts + 1, rec, 0.0), axis=1, keepdims=True)
    y = w1 * buf[slot, 0] + w2 * buf[slot, 1]
    out = x_ref[...] + mod_ref[0, gate_row:gate_row + 1, :] * y
    if final_norm:
        out = out * lax.rsqrt(jnp.mean(out * out, axis=-1, keepdims=True) + EPS) * fg_ref[...]
    o_ref[...] = out


def moe_combine(y, pos, x, route, mod, seq, n_experts, gate_row, final_gain=None):
    t, d = x.shape
    tm = _tile(seq, 256)
    per_seq = seq // tm
    final_norm = final_gain is not None
    fg = (final_gain if final_norm else jnp.ones((d,), F32)).reshape(1, d)
    grid_spec = pltpu.PrefetchScalarGridSpec(
        num_scalar_prefetch=1,
        grid=(t // tm,),
        in_specs=[pl.BlockSpec(memory_space=pl.ANY),
                  pl.BlockSpec((tm, d), lambda i, pos: (i, 0)),
                  pl.BlockSpec((tm, LANES), lambda i, pos: (i, 0)),
                  pl.BlockSpec((1, 6, d), lambda i, pos: (i // per_seq, 0, 0)),
                  pl.BlockSpec((1, d), lambda i, pos: (0, 0))],
        out_specs=pl.BlockSpec((tm, d), lambda i, pos: (i, 0)),
        scratch_shapes=[pltpu.VMEM((2, TOP_K, tm, d), F32), pltpu.SemaphoreType.DMA((2,))],
    )
    return pl.pallas_call(
        functools.partial(_moe_combine_kernel, n_tokens=t, n_experts=n_experts, gate_row=gate_row,
                          final_norm=final_norm),
        grid_spec=grid_spec,
        out_shape=jax.ShapeDtypeStruct((t, d), F32),
        compiler_params=_cparams("arbitrary"),
        name="moe_combine",
    )(pos, y, x, route, mod, fg)


def moe_ffn(x, gain, mod, seq, router_w, router_b, w1, w3, w2, gate_row, final_gain=None):
    t, d = x.shape
    n_experts = router_w.shape[1]
    tg = MOE_ROW_TILE
    w_r = _pad_cols(router_w, LANES)
    b_r = _pad_cols(router_b.reshape(1, n_experts), LANES)
    h, route = norm_router(x, gain, mod, seq, w_r, b_r, n_experts)
    cum = row_cumsum(route, 1, t)
    chosen = route[:, n_experts + 2:n_experts + 4].astype(jnp.int32)
    rank = jnp.take_along_axis(cum[:, :n_experts] - route[:, :n_experts], chosen, axis=1).astype(jnp.int32)
    counts = cum[t - 1, :n_experts].astype(jnp.int32)
    padded = (counts + tg - 1) // tg * tg
    ends = jnp.cumsum(padded)
    pos = (jnp.take(ends - padded, chosen) + rank).T.reshape(-1)
    n_tiles = (TOP_K * t + n_experts * (tg - 1)) // tg
    n_valid = (ends[n_experts - 1] // tg).reshape(1)
    tile_expert = jnp.minimum(jnp.searchsorted(ends, jnp.arange(n_tiles, dtype=jnp.int32) * tg, side="right"),
                              n_experts - 1).astype(jnp.int32)
    src = invert_positions(pos, n_tiles * tg, t)
    w1_8, s1 = fp8_weight(w1)
    w3_8, s3 = fp8_weight(w3)
    act = moe_up(h, w1_8, w3_8, jnp.stack([s1, s3]).reshape(1, 2), src, tile_expert, n_valid)
    y = moe_down(act, w2, tile_expert, n_valid)
    return moe_combine(y, pos, x, route, mod, seq, n_experts, gate_row, final_gain)


def _merge_kernel(ya_ref, yb_ref, yc_ref, wa_ref, wb_ref, wc_ref, ga_ref, gb_ref, gc_ref, o_ref):
    out = ga_ref[...].astype(F32) * jnp.dot(ya_ref[...], wa_ref[0], preferred_element_type=F32)
    out += gb_ref[...].astype(F32) * jnp.dot(yb_ref[...], wb_ref[0], preferred_element_type=F32)
    out += gc_ref[...].astype(F32) * jnp.dot(yc_ref[...], wc_ref[0], preferred_element_type=F32)
    o_ref[...] = out.astype(o_ref.dtype)


def merge_branches(y_a, y_b, y_c, w_branch, gates, seq):
    m, k = y_a.shape
    n = w_branch.shape[2]
    tm = _tile(seq, 1024)
    tn = _tile(n, 1024)
    nj = n // tn
    y_spec = pl.BlockSpec((tm, k), lambda i, j: (i, 0))

    def w_spec(b):
        return pl.BlockSpec((1, k, tn), lambda i, j: (b, 0, j))

    def g_spec(b):
        return pl.BlockSpec((tm, tn), lambda i, j: (i, b * nj + j))

    return pl.pallas_call(
        _merge_kernel,
        grid=(m // tm, nj),
        in_specs=[y_spec, y_spec, y_spec, w_spec(0), w_spec(1), w_spec(2), g_spec(0), g_spec(1), g_spec(2)],
        out_specs=pl.BlockSpec((tm, tn), lambda i, j: (i, j)),
        out_shape=jax.ShapeDtypeStruct((m, n), BF16),
        compiler_params=_cparams("parallel", "parallel"),
        name="merge_branches",
    )(y_a, y_b, y_c, w_branch, w_branch, w_branch, gates, gates, gates)


def _conv_kernel(h_ref, c_ref, b_ref, hp_ref, cp_ref, w_ref, bias_ref, o_ref, *, blocks_per_seq):
    u = c_ref[...].astype(F32) * h_ref[...].astype(F32)
    tm = u.shape[0]
    halo = hp_ref.shape[0]
    up = cp_ref[...].astype(F32) * hp_ref[...].astype(F32)
    first = (pl.program_id(0) % blocks_per_seq) == 0
    up = jnp.where(first, 0.0, up)
    p1 = up[halo - 1:halo, :]
    p2 = up[halo - 2:halo - 1, :]
    row = lax.broadcasted_iota(jnp.int32, (tm, 1), 0)
    u1 = jnp.where(row == 0, p1, pltpu.roll(u, 1, 0))
    u2 = jnp.where(row == 0, p2, jnp.where(row == 1, p1, pltpu.roll(u, 2, 0)))
    y = w_ref[0:1, :] * u2 + w_ref[1:2, :] * u1 + w_ref[2:3, :] * u + bias_ref[...]
    o_ref[...] = (b_ref[...].astype(F32) * y).astype(o_ref.dtype)


def short_conv(p, conv_w, conv_b, seq):
    t = p.shape[0]
    w = conv_w.shape[1]
    tm = _tile(seq, 512)
    halo = 16
    ratio = tm // halo
    prev = lambda i: jnp.maximum(i * ratio - 1, 0)
    return pl.pallas_call(
        functools.partial(_conv_kernel, blocks_per_seq=seq // tm),
        grid=(t // tm,),
        in_specs=[
            pl.BlockSpec((tm, w), lambda i: (i, 0)),
            pl.BlockSpec((tm, w), lambda i: (i, 1)),
            pl.BlockSpec((tm, w), lambda i: (i, 2)),
            pl.BlockSpec((halo, w), lambda i: (prev(i), 0)),
            pl.BlockSpec((halo, w), lambda i: (prev(i), 1)),
            pl.BlockSpec((3, w), lambda i: (0, 0)),
            pl.BlockSpec((1, w), lambda i: (0, 0)),
        ],
        out_specs=pl.BlockSpec((tm, w), lambda i: (i, 0)),
        out_shape=jax.ShapeDtypeStruct((t, w), BF16),
        compiler_params=_cparams("parallel"),
        name="short_conv",
    )(p, p, p, p, p, conv_w, conv_b.reshape(1, w))


def _block_cumsum(x_ref, cum_ref, carry_ref):
    @pl.when(pl.program_id(1) == 0)
    def _():
        carry_ref[...] = jnp.zeros_like(carry_ref)

    x = x_ref[...]
    n = x.shape[0]
    r = lax.broadcasted_iota(jnp.int32, (n, n), 0)
    c = lax.broadcasted_iota(jnp.int32, (n, n), 1)
    tri = (r >= c).astype(BF16)
    x1 = x.astype(BF16)
    r1 = x - x1.astype(F32)
    x2 = r1.astype(BF16)
    x3 = (r1 - x2.astype(F32)).astype(BF16)
    cum = (jnp.dot(tri, x1, preferred_element_type=F32) + jnp.dot(tri, x2, preferred_element_type=F32)
           + jnp.dot(tri, x3, preferred_element_type=F32)) + carry_ref[...]
    cum_ref[...] = cum
    carry_ref[...] = cum[n - 1:n, :]
    return cum


def _cumsum_kernel(x_ref, cum_ref, carry_ref):
    _block_cumsum(x_ref, cum_ref, carry_ref)


def row_cumsum(x, n_seq, seq):
    blk = _tile(seq, 256)
    nb = seq // blk
    return pl.pallas_call(
        _cumsum_kernel,
        grid=(n_seq, nb),
        in_specs=[pl.BlockSpec((blk, LANES), lambda b, i: (b * nb + i, 0))],
        out_specs=pl.BlockSpec((blk, LANES), lambda b, i: (b * nb + i, 0)),
        out_shape=jax.ShapeDtypeStruct((n_seq * seq, LANES), F32),
        scratch_shapes=[pltpu.VMEM((1, LANES), F32)],
        compiler_params=_cparams("parallel", "arbitrary"),
        name="row_cumsum",
    )(x)


FOX_GROUP = 2
LOG2_E = 1.4426950408889634


def _fox_bias_columns(cum_rows, head, value_lane, ones_lane, ones):
    lane = lax.broadcasted_iota(jnp.int32, cum_rows.shape, 1)
    c = jnp.sum(jnp.where(lane == head, cum_rows, 0.0), axis=1, keepdims=True) * LOG2_E
    hi = c.astype(BF16).astype(F32)
    r1 = c - hi
    mid = r1.astype(BF16).astype(F32)
    lo = r1 - mid
    out = jnp.where((lane >= ones_lane) & (lane < ones_lane + 3), ones, 0.0)
    out = jnp.where(lane == value_lane, hi, out)
    out = jnp.where(lane == value_lane + 1, mid, out)
    out = jnp.where(lane == value_lane + 2, lo, out)
    return out.astype(BF16)


def _fox_kernel(q_ref, k_ref, v_ref, cq_ref, ck_ref, o_ref, kaug_ref, vt_ref, qaug_ref, s_ref, m_ref, l_ref,
                acc_ref):
    hp = pl.program_id(1)
    i = pl.program_id(2)
    tq = q_ref.shape[0]
    tk = tq
    seq = k_ref.shape[0]

    @pl.when(i == 0)
    def _():
        def fill(c, carry):
            rows = pl.ds(pl.multiple_of(c * tk, tk), tk)
            for g in range(FOX_GROUP):
                cols = slice(g * HEAD_DIM, (g + 1) * HEAD_DIM)
                kaug_ref[g, rows, 0:HEAD_DIM] = k_ref[rows, cols]
                kaug_ref[g, rows, HEAD_DIM:2 * HEAD_DIM] = _fox_bias_columns(
                    ck_ref[rows, :], hp * FOX_GROUP + g, 0, 3, 1.0)
                vt_ref[g, :, rows] = v_ref[rows, cols].astype(F32).T.astype(BF16)
            return carry

        lax.fori_loop(0, seq // tk, fill, 0)

    for g in range(FOX_GROUP):
        cols = slice(g * HEAD_DIM, (g + 1) * HEAD_DIM)
        qaug_ref[g, :, 0:HEAD_DIM] = q_ref[:, cols]
        qaug_ref[g, :, HEAD_DIM:2 * HEAD_DIM] = _fox_bias_columns(cq_ref[...], hp * FOX_GROUP + g, 3, 0, -1.0)
    m_ref[...] = jnp.full_like(m_ref, NEG_INF)
    l_ref[...] = jnp.zeros_like(l_ref)
    acc_ref[...] = jnp.zeros_like(acc_ref)
    key = lax.broadcasted_iota(jnp.int32, (tk, tq), 0)
    qry = lax.broadcasted_iota(jnp.int32, (tk, tq), 1)

    def score(j, slot):
        start = pl.multiple_of(j * tk, tk)
        for g in range(FOX_GROUP):
            s_ref[slot, g] = lax.dot_general(kaug_ref[g, pl.ds(start, tk), :], qaug_ref[g],
                                             (((1,), (1,)), ((), ())), preferred_element_type=F32)

    def consume(j, slot, on_diagonal):
        start = pl.multiple_of(j * tk, tk)
        for g in range(FOX_GROUP):
            st = s_ref[slot, g]
            if on_diagonal:
                st = jnp.where(key <= qry, st, NEG_INF)
            m_old = m_ref[g]
            m_new = jnp.maximum(m_old, jnp.max(st, axis=0, keepdims=True))
            alpha = jnp.exp2(m_old - m_new)
            pt = jnp.exp2(st - m_new)
            l_ref[g] = alpha * l_ref[g] + jnp.sum(pt, axis=0, keepdims=True)
            acc_ref[g] = alpha * acc_ref[g] + jnp.dot(vt_ref[g, :, pl.ds(start, tk)], pt.astype(BF16),
                                                      preferred_element_type=F32)
            m_ref[g] = m_new

    def two_below_diagonal(p, carry):
        j = 2 * p
        score(j + 1, 1)
        consume(j, 0, False)
        score(j + 2, 0)
        consume(j + 1, 1, False)
        return carry

    score(0, 0)
    lax.fori_loop(0, i // 2, two_below_diagonal, 0)

    @pl.when(i % 2 == 0)
    def _():
        consume(i, 0, True)

    @pl.when(i % 2 == 1)
    def _():
        score(i, 1)
        consume(i - 1, 0, False)
        consume(i, 1, True)

    for g in range(FOX_GROUP):
        o_ref[:, g * HEAD_DIM:(g + 1) * HEAD_DIM] = (acc_ref[g] / l_ref[g]).T.astype(o_ref.dtype)


def fox_attention(q, kv, cum, batch, seq):
    tq = _tile(seq, 512)
    nq = seq // tq
    gw = FOX_GROUP * HEAD_DIM
    n_groups = N_HEADS // FOX_GROUP
    return pl.pallas_call(
        _fox_kernel,
        grid=(batch, n_groups, nq),
        in_specs=[
            pl.BlockSpec((tq, gw), lambda b, h, i: (b * nq + i, h)),
            pl.BlockSpec((seq, gw), lambda b, h, i: (b, h)),
            pl.BlockSpec((seq, gw), lambda b, h, i: (b, n_groups + h)),
            pl.BlockSpec((tq, LANES), lambda b, h, i: (b * nq + i, 0)),
            pl.BlockSpec((seq, LANES), lambda b, h, i: (b, 0)),
        ],
        out_specs=pl.BlockSpec((tq, gw), lambda b, h, i: (b * nq + i, h)),
        out_shape=jax.ShapeDtypeStruct((batch * seq, WIDTH), BF16),
        scratch_shapes=[pltpu.VMEM((FOX_GROUP, seq, 2 * HEAD_DIM), BF16),
                        pltpu.VMEM((FOX_GROUP, HEAD_DIM, seq), BF16),
                        pltpu.VMEM((FOX_GROUP, tq, 2 * HEAD_DIM), BF16),
                        pltpu.VMEM((2, FOX_GROUP, tq, tq), F32),
                        pltpu.VMEM((FOX_GROUP, 1, tq), F32), pltpu.VMEM((FOX_GROUP, 1, tq), F32),
                        pltpu.VMEM((FOX_GROUP, HEAD_DIM, tq), F32)],
        compiler_params=_cparams("parallel", "parallel", "arbitrary"),
        name="fox_attention",
    )(q, kv, kv, cum, cum)


def _hgrn_levels():
    sizes = []
    half = HGRN_CHUNK // 2
    while half >= HGRN_DIAG:
        sizes.append(half)
        half //= 2
    return sizes


def _hgrn_coefficients():
    c = HGRN_CHUNK
    t = np.arange(c)[:, None]
    u = np.arange(c)[None, :]
    slabs = [(u <= t), (u > t)]
    for size in _hgrn_levels():
        ref = (t // (2 * size)) * (2 * size) + size - 1
        upper = (t % (2 * size)) >= size
        slabs.append(np.where(upper, (u > ref) & (u <= t), (u > t) & (u <= ref)))
    slabs.append((u <= t) & (u // HGRN_DIAG == t // HGRN_DIAG))
    coef = np.concatenate(slabs, axis=0).astype(np.float32)
    return np.concatenate([coef, coef], axis=1)


def _hgrn_level_masks():
    c = HGRN_CHUNK
    t = np.arange(c)[:, None]
    s = np.arange(c)[None, :]
    masks = []
    for size in _hgrn_levels():
        same = (t // (2 * size)) == (s // (2 * size))
        masks.append(same & ((t % (2 * size)) >= size) & ((s % (2 * size)) < size))
    masks.append((t // HGRN_DIAG == s // HGRN_DIAG) & (s <= t))
    return np.stack(masks).astype(np.float32)


def _hgrn_placement():
    place = np.zeros((HGRN_DIAG, HEAD_DIM, HGRN_CHUNK), np.float32)
    for j in range(HGRN_DIAG):
        place[j, :, j::HGRN_DIAG] = 1.0
    return place.reshape(HGRN_DIAG * HEAD_DIM, HGRN_CHUNK)


def _hgrn_kernel(q_ref, f_ref, i_ref, g_ref, lb_ref, norm_ref, coef_ref, mask_ref, place_ref, o_ref,
                 st_ref, e_ref, kf_ref, *, layer):
    c = HGRN_CHUNK
    blk_rows = HGRN_DIAG
    n_lev = len(_hgrn_levels())
    b_in_rows = (2 + n_lev) * c

    @pl.when(pl.program_id(1) == 0)
    def _():
        st_ref[...] = jnp.zeros_like(st_ref)

    lbr = lb_ref[...]
    le = jnp.exp(lbr - jnp.max(lbr, axis=0, keepdims=True))
    lp = le / jnp.sum(le, axis=0, keepdims=True)
    lb = jnp.zeros((1, WIDTH), F32)
    for r in range(1, layer + 1):
        lb = lb + lp[r:r + 1, :]

    f = lb + (1.0 - lb) * _sigmoid(f_ref[...].astype(F32))
    g = jnp.log(f)
    kf_ref[...] = 1.0 - f
    g_hi, g_lo = _split_bf16(g)
    e_ref[...] = jnp.dot(coef_ref[...], jnp.concatenate([g_hi, g_lo], axis=0), preferred_element_type=F32)

    def head(h, carry):
        lanes = pl.ds(pl.multiple_of(h * HEAD_DIM, HEAD_DIM), HEAD_DIM)
        q = q_ref[:, lanes].astype(F32)
        k = kf_ref[:, lanes]
        v = i_ref[:, lanes]
        st = st_ref[h]
        qe = (q * jnp.exp(e_ref[0:c, lanes])).astype(BF16)
        out = lax.dot_general(qe, st.astype(BF16), (((1,), (1,)), ((), ())), preferred_element_type=F32)
        attn = jnp.zeros((c, c), F32)
        for lev in range(n_lev):
            pw = jnp.exp(e_ref[(2 + lev) * c:(3 + lev) * c, lanes])
            a = lax.dot_general((q * pw).astype(BF16), (k * pw).astype(BF16), (((1,), (1,)), ((), ())),
                                preferred_element_type=F32)
            attn = attn + jnp.where(mask_ref[lev] > 0.0, a, 0.0)
        b_in = e_ref[b_in_rows:b_in_rows + c, lanes]

        def block_row(ref, base, j):
            rows = [jnp.broadcast_to(ref[pl.ds(base + blk * blk_rows + j, 1), lanes], (blk_rows, HEAD_DIM))
                    for blk in range(c // blk_rows)]
            return jnp.concatenate(rows, axis=0)

        z = []
        for j in range(blk_rows):
            k_j = block_row(kf_ref, 0, j)
            b_j = block_row(e_ref, b_in_rows, j)
            z.append((q * k_j * jnp.exp(jnp.minimum(b_in - b_j, 0.0))).astype(BF16))
        diag = jnp.dot(jnp.concatenate(z, axis=1), place_ref[...], preferred_element_type=F32)
        attn = attn + jnp.where(mask_ref[n_lev] > 0.0, diag, 0.0)
        out = out + jnp.dot(attn.astype(BF16), v, preferred_element_type=F32)
        ke = (k * jnp.exp(e_ref[c:2 * c, lanes])).astype(BF16)
        decay = jnp.exp(e_ref[c - 1:c, lanes])
        st_ref[h] = st * decay + lax.dot_general(v, ke, (((0,), (0,)), ((), ())), preferred_element_type=F32)
        out = out * lax.rsqrt(jnp.mean(out * out, axis=-1, keepdims=True) + EPS)
        out = out * norm_ref[:, lanes] * _silu(g_ref[:, lanes].astype(F32))
        o_ref[:, lanes] = out.astype(o_ref.dtype)
        return carry

    lax.fori_loop(0, N_HEADS, head, 0, unroll=4)


def hgrn_mixer(p, lower_bounds, norm, layer, batch, seq):
    c = HGRN_CHUNK
    nc = seq // c
    depth = lower_bounds.shape[0]
    coef = jnp.asarray(_hgrn_coefficients(), BF16)
    masks = jnp.asarray(_hgrn_level_masks(), F32)
    place = jnp.asarray(_hgrn_placement(), BF16)
    n_slab = coef.shape[0] // c

    def col_spec(j):
        return pl.BlockSpec((c, WIDTH), lambda b, n: (b * nc + n, j))

    return pl.pallas_call(
        functools.partial(_hgrn_kernel, layer=layer),
        grid=(batch, nc),
        in_specs=[col_spec(0), col_spec(1), col_spec(2), col_spec(3),
                  pl.BlockSpec((depth, WIDTH), lambda b, n: (0, 0)),
                  pl.BlockSpec((1, WIDTH), lambda b, n: (0, 0)),
                  pl.BlockSpec(coef.shape, lambda b, n: (0, 0)),
                  pl.BlockSpec(masks.shape, lambda b, n: (0, 0, 0)),
                  pl.BlockSpec(place.shape, lambda b, n: (0, 0))],
        out_specs=pl.BlockSpec((c, WIDTH), lambda b, n: (b * nc + n, 0)),
        out_shape=jax.ShapeDtypeStruct((batch * seq, WIDTH), BF16),
        scratch_shapes=[pltpu.VMEM((N_HEADS, HEAD_DIM, HEAD_DIM), F32),
                        pltpu.VMEM((n_slab * c, WIDTH), F32),
                        pltpu.VMEM((c, WIDTH), F32)],
        compiler_params=_cparams("parallel", "arbitrary"),
        name="hgrn_mixer",
    )(p, p, p, p, lower_bounds, norm.reshape(1, WIDTH), coef, masks, place)


def _pad_cols(w, n):
    return jnp.pad(w, ((0, 0), (0, n - w.shape[1])))


def _round_up(n, m):
    return ((n + m - 1) // m) * m


def kernel(x, c, norm_mix, norm_ffn, w_ada, b_ada, w_in, b_gate, fox_b_f, hgrn_lower_bounds, hgrn_norm,
           conv_w, conv_b, w_branch, w_o, ffn_w1, ffn_w3, ffn_w2, router_w, router_b,
           expert_w1, expert_w3, expert_w2, norm_final):
    batch, seq, d = x.shape
    depth = w_ada.shape[0]
    t = batch * seq

    o_fox = 0
    o_ff = 3 * WIDTH
    o_hgrn = o_ff + N_HEADS
    o_conv = o_hgrn + 4 * WIDTH
    o_gate = o_conv + 3 * WIDTH

    mod_all = ada_modulation(c, w_ada, b_ada)
    xt = x.reshape(t, d)
    for layer in range(depth):
        mod = mod_all[layer]
        wl = w_in[layer]
        w_q = wl[:, o_fox:o_fox + WIDTH].astype(BF16)
        w_kv = wl[:, o_fox + WIDTH:o_ff].astype(BF16)
        w_ff = _pad_cols(wl[:, o_ff:o_hgrn], LANES).astype(BF16)
        w_hgrn = wl[:, o_hgrn:o_conv].astype(BF16)
        w_conv = wl[:, o_conv:o_gate].astype(BF16)
        w_gate = wl[:, o_gate:]
        b_ff = _pad_cols(fox_b_f[layer].reshape(1, N_HEADS), LANES)

        h, lsf = norm_forget(xt, norm_mix[layer], mod, seq, w_ff, b_ff)
        q = project(h, w_q, seq, scale=HEAD_DIM ** -0.5 * LOG2_E)
        kv = project(h, w_kv, seq)
        p_hgrn = project(h, w_hgrn, seq)
        p_conv = project(h, w_conv, seq)
        gates = gate_project(h, w_gate, b_gate[layer], seq)

        y_a = hgrn_mixer(p_hgrn, hgrn_lower_bounds, hgrn_norm[layer], layer, batch, seq)
        y_b = short_conv(p_conv, conv_w[layer], conv_b[layer], seq)
        cum = row_cumsum(lsf, batch, seq)
        y_c = fox_attention(q, kv, cum, batch, seq)

        merged = merge_branches(y_a, y_b, y_c, w_branch[layer].astype(BF16), gates, seq)
        xt = residual_project(merged, w_o[layer].astype(BF16), xt, mod, seq, 2, d // 2)

        i = layer // 2
        if layer % 2 == 0:
            dff = ffn_w1.shape[2]
            dff_pad = _round_up(dff, 1024)
            w1 = _pad_cols(ffn_w1[i], dff_pad).astype(BF16)
            w3 = _pad_cols(ffn_w3[i], dff_pad).astype(BF16)
            w2 = jnp.pad(ffn_w2[i], ((0, dff_pad - dff), (0, 0))).astype(BF16)
            h2 = norm_only(xt, norm_ffn[layer], mod, seq)
            act = glu(h2, w1, w3, seq)
            xt = residual_project(act, w2, xt, mod, seq, 5, dff_pad // 4)
        else:
            last = layer == depth - 1
            xt = moe_ffn(xt, norm_ffn[layer], mod, seq, router_w[i], router_b[i],
                         expert_w1[i], expert_w3[i], expert_w2[i].astype(BF16), 5,
                         final_gain=norm_final if last else None)
    if depth % 2:
        xt = final_norm(xt, norm_final)
    return xt.reshape(batch, seq, d)
```

```python
import functools

import jax
import jax.numpy as jnp
import numpy as np
from jax import lax
from jax.experimental import pallas as pl
from jax.experimental.pallas import tpu as pltpu

F32 = jnp.float32
BF16 = jnp.bfloat16
FP8 = jnp.float8_e4m3fn
FP8_TARGET = 256.0

N_HEADS = 8
HEAD_DIM = 128
WIDTH = N_HEADS * HEAD_DIM
N_BRANCH = 3
TOP_K = 2
EPS = 1e-6
NEG_INF = -1e30
LANES = 128
HGRN_CHUNK = 128
HGRN_DIAG = 16
NORM_ROWS = 512
VMEM_LIMIT = 56 * 1024 * 1024


def _cparams(*sem):
    return pltpu.CompilerParams(dimension_semantics=sem, vmem_limit_bytes=VMEM_LIMIT)


def _tile(n, pref):
    t = min(n, pref)
    while n % t:
        t //= 2
    return t


def _sigmoid(z):
    return 1.0 / (1.0 + jnp.exp(-z))


def _silu(z):
    return z * _sigmoid(z)


def _ada_kernel(ct_ref, w_ref, b_ref, o_ref, act_ref, *, batch):
    k = pl.program_id(2)
    tk = w_ref.shape[1]
    tn = o_ref.shape[2]

    @pl.when((pl.program_id(0) == 0) & (pl.program_id(1) == 0) & (k == 0))
    def _():
        act = _silu(ct_ref[...])
        for b in range(batch):
            act_ref[b] = jnp.broadcast_to(act[:, b:b + 1], act_ref.shape[1:])

    @pl.when(k == 0)
    def _():
        row = lax.broadcasted_iota(jnp.int32, o_ref.shape[1:], 0)
        o_ref[0] = jnp.where(row < batch, b_ref[0], 0.0)

    rows = pl.ds(pl.multiple_of(k * tk, tk), tk)
    for b in range(batch):
        a = act_ref[b, rows, :]
        for jb in range(tn // LANES):
            cols = slice(jb * LANES, (jb + 1) * LANES)
            o_ref[0, b:b + 1, cols] += jnp.sum(w_ref[0, :, cols] * a, axis=0, keepdims=True)


def ada_modulation(c, w_ada, b_ada):
    depth, d, n = w_ada.shape
    b = c.shape[0]
    rows = _round_up(b, 8)
    tn = _tile(n, 2048)
    tk = _tile(d, 1024)
    out = pl.pallas_call(
        functools.partial(_ada_kernel, batch=b),
        grid=(depth, n // tn, d // tk),
        in_specs=[
            pl.BlockSpec((d, b), lambda l, j, k: (0, 0)),
            pl.BlockSpec((1, tk, tn), lambda l, j, k: (l, k, j)),
            pl.BlockSpec((1, 1, tn), lambda l, j, k: (l, 0, j)),
        ],
        out_specs=pl.BlockSpec((1, rows, tn), lambda l, j, k: (l, 0, j)),
        out_shape=jax.ShapeDtypeStruct((depth, rows, n), F32),
        scratch_shapes=[pltpu.VMEM((b, d, LANES), F32)],
        compiler_params=_cparams("arbitrary", "arbitrary", "arbitrary"),
        name="ada_modulation",
    )(c.T, w_ada, b_ada.reshape(depth, 1, n))
    return out[:, :b].reshape(depth, b, 6, d)


def _norm_mod(x_ref, g_ref, mod_ref, shift_row, scale_row):
    x = x_ref[...]
    y = x * lax.rsqrt(jnp.mean(x * x, axis=-1, keepdims=True) + EPS) * g_ref[...]
    return y * (1.0 + mod_ref[0, scale_row:scale_row + 1, :]) + mod_ref[0, shift_row:shift_row + 1, :]


def _log_sigmoid(z):
    return jnp.minimum(z, 0.0) - jnp.log(1.0 + jnp.exp(-jnp.abs(z)))


def _norm_forget_kernel(x_ref, g_ref, mod_ref, wf_ref, bf_ref, h_ref, lsf_ref, *, shift_row, scale_row):
    h = _norm_mod(x_ref, g_ref, mod_ref, shift_row, scale_row).astype(BF16)
    h_ref[...] = h
    z = jnp.dot(h, wf_ref[...], preferred_element_type=F32) + bf_ref[...]
    lsf_ref[...] = _log_sigmoid(z)


def _split_bf16(v):
    hi = v.astype(BF16)
    lo = (v - hi.astype(F32)).astype(BF16)
    return hi, lo


def _norm_router_kernel(x_ref, g_ref, mod_ref, wr_ref, br_ref, h_ref, route_ref, *,
                        shift_row, scale_row, n_experts):
    h = _norm_mod(x_ref, g_ref, mod_ref, shift_row, scale_row)
    h_ref[...] = h
    h_hi, h_lo = _split_bf16(h)
    w_hi, w_lo = _split_bf16(wr_ref[...])
    logits = (jnp.dot(h_hi, w_hi, preferred_element_type=F32)
              + jnp.dot(h_hi, w_lo, preferred_element_type=F32)
              + jnp.dot(h_lo, w_hi, preferred_element_type=F32)) + br_ref[...]
    lane = lax.broadcasted_iota(jnp.int32, logits.shape, 1)
    lg = jnp.where(lane < n_experts, logits, -jnp.inf)
    m1 = jnp.max(lg, axis=1, keepdims=True)
    i1 = jnp.min(jnp.where(lg == m1, lane, LANES), axis=1, keepdims=True)
    lg2 = jnp.where(lane == i1, -jnp.inf, lg)
    m2 = jnp.max(lg2, axis=1, keepdims=True)
    i2 = jnp.min(jnp.where(lg2 == m2, lane, LANES), axis=1, keepdims=True)
    e = jnp.exp(m2 - m1)
    w1 = 1.0 / (1.0 + e)
    w2 = e / (1.0 + e)
    rec = jnp.where((lane == i1) | (lane == i2), 1.0, 0.0)
    rec = jnp.where(lane == n_experts, w1, rec)
    rec = jnp.where(lane == n_experts + 1, w2, rec)
    rec = jnp.where(lane == n_experts + 2, i1.astype(F32), rec)
    rec = jnp.where(lane == n_experts + 3, i2.astype(F32), rec)
    route_ref[...] = rec


def _norm_only_kernel(x_ref, g_ref, mod_ref, h_ref, *, shift_row, scale_row):
    h_ref[...] = _norm_mod(x_ref, g_ref, mod_ref, shift_row, scale_row).astype(BF16)


def _norm_call(body, x, gain, mod, seq, extra_in, extra_specs, extra_out, extra_out_specs, tm, h_dtype=BF16):
    t, d = x.shape
    per_seq = seq // tm
    in_specs = [
        pl.BlockSpec((tm, d), lambda i: (i, 0)),
        pl.BlockSpec((1, d), lambda i: (0, 0)),
        pl.BlockSpec((1, 6, d), lambda i: (i // per_seq, 0, 0)),
    ] + extra_specs
    out_shape = [jax.ShapeDtypeStruct((t, d), h_dtype)] + extra_out
    out_specs = [pl.BlockSpec((tm, d), lambda i: (i, 0))] + extra_out_specs
    return pl.pallas_call(
        body,
        grid=(t // tm,),
        in_specs=in_specs,
        out_specs=out_specs,
        out_shape=out_shape,
        compiler_params=_cparams("parallel"),
        name="norm_mod",
    )(x, gain.reshape(1, d), mod, *extra_in)


def norm_forget(x, gain, mod, seq, w_f, b_f):
    t, d = x.shape
    tm = _tile(seq, NORM_ROWS)
    body = functools.partial(_norm_forget_kernel, shift_row=0, scale_row=1)
    return _norm_call(
        body, x, gain, mod, seq, [w_f, b_f],
        [pl.BlockSpec((d, LANES), lambda i: (0, 0)), pl.BlockSpec((1, LANES), lambda i: (0, 0))],
        [jax.ShapeDtypeStruct((t, LANES), F32)], [pl.BlockSpec((tm, LANES), lambda i: (i, 0))], tm)


def norm_router(x, gain, mod, seq, w_r, b_r, n_experts):
    t, d = x.shape
    assert n_experts + 4 <= LANES
    tm = _tile(seq, NORM_ROWS)
    body = functools.partial(_norm_router_kernel, shift_row=3, scale_row=4, n_experts=n_experts)
    return _norm_call(
        body, x, gain, mod, seq, [w_r, b_r],
        [pl.BlockSpec((d, LANES), lambda i: (0, 0)), pl.BlockSpec((1, LANES), lambda i: (0, 0))],
        [jax.ShapeDtypeStruct((t, LANES), F32)], [pl.BlockSpec((tm, LANES), lambda i: (i, 0))], tm,
        h_dtype=F32)


def norm_only(x, gain, mod, seq):
    tm = _tile(seq, NORM_ROWS)
    body = functools.partial(_norm_only_kernel, shift_row=3, scale_row=4)
    return _norm_call(body, x, gain, mod, seq, [], [], [], [], tm)[0]


def _final_norm_kernel(x_ref, g_ref, o_ref):
    x = x_ref[...]
    o_ref[...] = x * lax.rsqrt(jnp.mean(x * x, axis=-1, keepdims=True) + EPS) * g_ref[...]


def final_norm(x, gain):
    t, d = x.shape
    tm = _tile(t, NORM_ROWS)
    return pl.pallas_call(
        _final_norm_kernel,
        grid=(t // tm,),
        in_specs=[pl.BlockSpec((tm, d), lambda i: (i, 0)), pl.BlockSpec((1, d), lambda i: (0, 0))],
        out_specs=pl.BlockSpec((tm, d), lambda i: (i, 0)),
        out_shape=jax.ShapeDtypeStruct((t, d), F32),
        compiler_params=_cparams("parallel"),
        name="final_norm",
    )(x, gain.reshape(1, d))


def _proj_kernel(a_ref, w_ref, s_ref, o_ref):
    acc = jnp.dot(a_ref[...], w_ref[...], preferred_element_type=F32)
    o_ref[...] = (acc * s_ref[...]).astype(o_ref.dtype)


def _pow2_scale(amax):
    return jnp.exp2(jnp.floor(jnp.log2(FP8_TARGET / jnp.maximum(amax, 1e-30))))


def fp8_weight(w):
    scale = _pow2_scale(jnp.max(jnp.abs(w)))
    return (w * scale).astype(FP8), scale


def _gate_proj_kernel(a_ref, w_ref, b_ref, ws_ref, o_ref, a8_ref, inv_ref):
    @pl.when(pl.program_id(1) == 0)
    def _():
        a = a_ref[...].astype(F32)
        s = _pow2_scale(jnp.max(jnp.abs(a), axis=1, keepdims=True))
        a8_ref[...] = (a * s).astype(FP8)
        inv_ref[...] = 1.0 / (s * ws_ref[...])

    acc = jnp.dot(a8_ref[...], w_ref[...], preferred_element_type=F32)
    o_ref[...] = _sigmoid(acc * inv_ref[...] + b_ref[...]).astype(o_ref.dtype)


def _mm_tiles(m, n, seq):
    return _tile(seq, 1024), _tile(n, 1024)


def project(a, w, col_scale, seq):
    m, k = a.shape
    n = w.shape[1]
    tm, tn = _mm_tiles(m, n, seq)
    return pl.pallas_call(
        _proj_kernel,
        grid=(m // tm, n // tn),
        in_specs=[pl.BlockSpec((tm, k), lambda i, j: (i, 0)), pl.BlockSpec((k, tn), lambda i, j: (0, j)),
                  pl.BlockSpec((1, tn), lambda i, j: (0, j))],
        out_specs=pl.BlockSpec((tm, tn), lambda i, j: (i, j)),
        out_shape=jax.ShapeDtypeStruct((m, n), BF16),
        compiler_params=_cparams("parallel", "parallel"),
        name="project",
    )(a, w, col_scale.reshape(1, n))


def gate_project(a, w, bias, seq):
    m, k = a.shape
    n = w.shape[1]
    tm, tn = _mm_tiles(m, n, seq)
    w8, w_scale = fp8_weight(w)
    return pl.pallas_call(
        _gate_proj_kernel,
        grid=(m // tm, n // tn),
        in_specs=[pl.BlockSpec((tm, k), lambda i, j: (i, 0)), pl.BlockSpec((k, tn), lambda i, j: (0, j)),
                  pl.BlockSpec((1, tn), lambda i, j: (0, j)), pl.BlockSpec((1, 1), lambda i, j: (0, 0))],
        out_specs=pl.BlockSpec((tm, tn), lambda i, j: (i, j)),
        out_shape=jax.ShapeDtypeStruct((m, n), BF16),
        scratch_shapes=[pltpu.VMEM((tm, k), FP8), pltpu.VMEM((tm, 1), F32)],
        compiler_params=_cparams("parallel", "arbitrary"),
        name="gate_project",
    )(a, w8, bias.reshape(1, n), w_scale.reshape(1, 1))


def _residual_kernel(a_ref, w_ref, x_ref, mod_ref, o_ref, acc_ref, *, gate_row):
    k = pl.program_id(2)

    @pl.when(k == 0)
    def _():
        acc_ref[...] = jnp.zeros_like(acc_ref)

    acc_ref[...] += jnp.dot(a_ref[...], w_ref[...], preferred_element_type=F32)

    @pl.when(k == pl.num_programs(2) - 1)
    def _():
        o_ref[...] = x_ref[...] + mod_ref[0, gate_row:gate_row + 1, :] * acc_ref[...]


def residual_project(a, w, x, mod, seq, gate_row, tk_pref):
    m, kdim = a.shape
    n = w.shape[1]
    tm, tn = _mm_tiles(m, n, seq)
    tk = _tile(kdim, tk_pref)
    per_seq = seq // tm
    return pl.pallas_call(
        functools.partial(_residual_kernel, gate_row=gate_row),
        grid=(m // tm, n // tn, kdim // tk),
        in_specs=[
            pl.BlockSpec((tm, tk), lambda i, j, k: (i, k)),
            pl.BlockSpec((tk, tn), lambda i, j, k: (k, j)),
            pl.BlockSpec((tm, tn), lambda i, j, k: (i, j)),
            pl.BlockSpec((1, 6, tn), lambda i, j, k: (i // per_seq, 0, j)),
        ],
        out_specs=pl.BlockSpec((tm, tn), lambda i, j, k: (i, j)),
        out_shape=jax.ShapeDtypeStruct((m, n), F32),
        scratch_shapes=[pltpu.VMEM((tm, tn), F32)],
        compiler_params=_cparams("parallel", "parallel", "arbitrary"),
        name="residual_project",
    )(a, w, x, mod)


def _glu_kernel(h_ref, w1_ref, w3_ref, o_ref):
    h = h_ref[...]
    a = jnp.dot(h, w1_ref[...], preferred_element_type=F32)
    b = jnp.dot(h, w3_ref[...], preferred_element_type=F32)
    o_ref[...] = (_silu(a) * b).astype(o_ref.dtype)


def glu(h, w1, w3, seq):
    m, k = h.shape
    n = w1.shape[1]
    tm = _tile(seq, 1024)
    tn = _tile(n, 512)
    return pl.pallas_call(
        _glu_kernel,
        grid=(m // tm, n // tn),
        in_specs=[pl.BlockSpec((tm, k), lambda i, j: (i, 0)),
                  pl.BlockSpec((k, tn), lambda i, j: (0, j)),
                  pl.BlockSpec((k, tn), lambda i, j: (0, j))],
        out_specs=pl.BlockSpec((tm, tn), lambda i, j: (i, j)),
        out_shape=jax.ShapeDtypeStruct((m, n), BF16),
        compiler_params=_cparams("parallel", "parallel"),
        name="glu",
    )(h, w1, w3)


MOE_ROW_TILE = 512


def _invert_kernel(pos_ref, src_ref, *, n_tokens):
    def clear(p, carry):
        src_ref[p] = 0
        return carry

    lax.fori_loop(0, src_ref.shape[0], clear, 0, unroll=8)

    def place(t, carry):
        src_ref[pos_ref[t]] = t
        src_ref[pos_ref[n_tokens + t]] = t
        return carry

    lax.fori_loop(0, n_tokens, place, 0, unroll=8)


def invert_positions(pos, n_slots, n_tokens):
    return pl.pallas_call(
        functools.partial(_invert_kernel, n_tokens=n_tokens),
        in_specs=[pl.BlockSpec(memory_space=pltpu.SMEM)],
        out_specs=pl.BlockSpec(memory_space=pltpu.SMEM),
        out_shape=jax.ShapeDtypeStruct((n_slots,), jnp.int32),
        name="invert_positions",
    )(pos)


def _moe_up_kernel(src_ref, texp_ref, nv_ref, h_hbm, w1_ref, w3_ref, ws_ref, o_ref, buf, xs, inv, sem):
    i = pl.program_id(0)
    j = pl.program_id(1)
    nv = nv_ref[0]
    tm = xs.shape[0]
    slot = i % 2

    def row_copy(tile, r, s):
        tok = src_ref[tile * tm + r]
        return pltpu.make_async_copy(h_hbm.at[pl.ds(tok, 1), :], buf.at[s, pl.ds(r, 1), :], sem.at[s])

    def start_gather(tile, s):
        def body(r, carry):
            row_copy(tile, r, s).start()
            return carry
        lax.fori_loop(0, tm, body, 0, unroll=8)

    def wait_gather(tile, s):
        def body(r, carry):
            row_copy(tile, r, s).wait()
            return carry
        lax.fori_loop(0, tm, body, 0, unroll=8)

    @pl.when((j == 0) & (i < nv))
    def _():
        @pl.when(i == 0)
        def _():
            start_gather(0, 0)

        @pl.when(i + 1 < nv)
        def _():
            start_gather(i + 1, 1 - slot)

        wait_gather(i, slot)
        rows = buf[slot]
        s = _pow2_scale(jnp.max(jnp.abs(rows), axis=1, keepdims=True))
        xs[...] = (rows * s).astype(FP8)
        inv[...] = 1.0 / s

    @pl.when(i < nv)
    def _():
        x = xs[...]
        a = jnp.dot(x, w1_ref[0], preferred_element_type=F32) * (inv[...] / ws_ref[:, 0:1])
        b = jnp.dot(x, w3_ref[0], preferred_element_type=F32) * (inv[...] / ws_ref[:, 1:2])
        o_ref[...] = (_silu(a) * b).astype(o_ref.dtype)

    @pl.when(i >= nv)
    def _():
        o_ref[...] = jnp.zeros_like(o_ref)


MOE_UP_COLS = 256


def moe_up(h, w1, w3, w_scale, src, tile_expert, n_valid):
    d = h.shape[1]
    f = w1.shape[2]
    tm = MOE_ROW_TILE
    n_tiles = src.shape[0] // tm
    tn = _tile(f, MOE_UP_COLS)
    nj = f // tn

    def w_map(i, j, src, texp, nv):
        ie = jnp.minimum(i, nv[0] - 1)
        return (texp[ie], 0, jnp.where(i < nv[0], j, nj - 1))

    grid_spec = pltpu.PrefetchScalarGridSpec(
        num_scalar_prefetch=3,
        grid=(n_tiles, nj),
        in_specs=[pl.BlockSpec(memory_space=pl.ANY),
                  pl.BlockSpec((1, d, tn), w_map),
                  pl.BlockSpec((1, d, tn), w_map),
                  pl.BlockSpec((1, 2), lambda i, j, src, texp, nv: (0, 0))],
        out_specs=pl.BlockSpec((tm, tn), lambda i, j, src, texp, nv: (i, j)),
        scratch_shapes=[pltpu.VMEM((2, tm, d), F32), pltpu.VMEM((tm, d), FP8), pltpu.VMEM((tm, 1), F32),
                        pltpu.SemaphoreType.DMA((2,))],
    )
    return pl.pallas_call(
        _moe_up_kernel,
        grid_spec=grid_spec,
        out_shape=jax.ShapeDtypeStruct((n_tiles * tm, f), BF16),
        compiler_params=_cparams("arbitrary", "arbitrary"),
        name="moe_up",
    )(src, tile_expert, n_valid, h, w1, w3, w_scale)


def _moe_down_kernel(texp_ref, nv_ref, a_ref, w_ref, ws_ref, o_ref, a8_ref, inv_ref):
    valid = pl.program_id(0) < nv_ref[0]

    @pl.when(valid & (pl.program_id(1) == 0))
    def _():
        a = a_ref[...].astype(F32)
        s = _pow2_scale(jnp.max(jnp.abs(a), axis=1, keepdims=True))
        a8_ref[...] = (a * s).astype(FP8)
        inv_ref[...] = 1.0 / (s * ws_ref[...])

    @pl.when(valid)
    def _():
        o_ref[...] = jnp.dot(a8_ref[...], w_ref[0], preferred_element_type=F32) * inv_ref[...]

    @pl.when(jnp.logical_not(valid))
    def _():
        o_ref[...] = jnp.zeros_like(o_ref)


def moe_down(act, w2, w_scale, tile_expert, n_valid):
    m, f = act.shape
    d = w2.shape[2]
    tm = MOE_ROW_TILE
    tn = _tile(d, 1024)
    nj = d // tn

    def row(i, nv):
        return jnp.minimum(i, nv[0] - 1)

    def col(i, j, nv):
        return jnp.where(i < nv[0], j, nj - 1)

    grid_spec = pltpu.PrefetchScalarGridSpec(
        num_scalar_prefetch=2,
        grid=(m // tm, nj),
        in_specs=[pl.BlockSpec((tm, f), lambda i, j, texp, nv: (row(i, nv), 0)),
                  pl.BlockSpec((1, f, tn), lambda i, j, texp, nv: (texp[row(i, nv)], 0, col(i, j, nv))),
                  pl.BlockSpec((1, 1), lambda i, j, texp, nv: (0, 0))],
        out_specs=pl.BlockSpec((tm, tn), lambda i, j, texp, nv: (i, j)),
        scratch_shapes=[pltpu.VMEM((tm, f), FP8), pltpu.VMEM((tm, 1), F32)],
    )
    return pl.pallas_call(
        _moe_down_kernel,
        grid_spec=grid_spec,
        out_shape=jax.ShapeDtypeStruct((m, d), F32),
        compiler_params=_cparams("arbitrary", "arbitrary"),
        name="moe_down",
    )(tile_expert, n_valid, act, w2, w_scale)


def _moe_combine_kernel(pos_ref, y_hbm, x_ref, route_ref, mod_ref, fg_ref, o_ref, buf, sem, *,
                        n_tokens, n_experts, gate_row, final_norm):
    i = pl.program_id(0)
    tm = x_ref.shape[0]
    slot = i % 2

    def row_copy(tile, r, choice, s):
        p = pos_ref[choice * n_tokens + tile * tm + r]
        return pltpu.make_async_copy(y_hbm.at[pl.ds(p, 1), :], buf.at[s, choice, pl.ds(r, 1), :], sem.at[s])

    def start_gather(tile, s):
        def body(r, carry):
            row_copy(tile, r, 0, s).start()
            row_copy(tile, r, 1, s).start()
            return carry
        lax.fori_loop(0, tm, body, 0, unroll=8)

    def wait_gather(tile, s):
        def body(r, carry):
            row_copy(tile, r, 0, s).wait()
            row_copy(tile, r, 1, s).wait()
            return carry
        lax.fori_loop(0, tm, body, 0, unroll=8)

    @pl.when(i == 0)
    def _():
        start_gather(0, 0)

    @pl.when(i + 1 < pl.num_programs(0))
    def _():
        start_gather(i + 1, 1 - slot)

    wait_gather(i, slot)
    rec = route_ref[...]
    lane = lax.broadcasted_iota(jnp.int32, rec.shape, 1)
    w1 = jnp.sum(jnp.where(lane == n_experts, rec, 0.0), axis=1, keepdims=True)
    w2 = jnp.sum(jnp.where(lane == n_experts + 1, rec, 0.0), axis=1, keepdims=True)
    y = w1 * buf[slot, 0] + w2 * buf[slot, 1]
    out = x_ref[...] + mod_ref[0, gate_row:gate_row + 1, :] * y
    if final_norm:
        out = out * lax.rsqrt(jnp.mean(out * out, axis=-1, keepdims=True) + EPS) * fg_ref[...]
    o_ref[...] = out


def moe_combine(y, pos, x, route, mod, seq, n_experts, gate_row, final_gain=None):
    t, d = x.shape
    tm = _tile(seq, 256)
    per_seq = seq // tm
    final_norm = final_gain is not None
    fg = (final_gain if final_norm else jnp.ones((d,), F32)).reshape(1, d)
    grid_spec = pltpu.PrefetchScalarGridSpec(
        num_scalar_prefetch=1,
        grid=(t // tm,),
        in_specs=[pl.BlockSpec(memory_space=pl.ANY),
                  pl.BlockSpec((tm, d), lambda i, pos: (i, 0)),
                  pl.BlockSpec((tm, LANES), lambda i, pos: (i, 0)),
                  pl.BlockSpec((1, 6, d), lambda i, pos: (i // per_seq, 0, 0)),
                  pl.BlockSpec((1, d), lambda i, pos: (0, 0))],
        out_specs=pl.BlockSpec((tm, d), lambda i, pos: (i, 0)),
        scratch_shapes=[pltpu.VMEM((2, TOP_K, tm, d), F32), pltpu.SemaphoreType.DMA((2,))],
    )
    return pl.pallas_call(
        functools.partial(_moe_combine_kernel, n_tokens=t, n_experts=n_experts, gate_row=gate_row,
                          final_norm=final_norm),
        grid_spec=grid_spec,
        out_shape=jax.ShapeDtypeStruct((t, d), F32),
        compiler_params=_cparams("arbitrary"),
        name="moe_combine",
    )(pos, y, x, route, mod, fg)


def moe_ffn(x, gain, mod, seq, router_w, router_b, w1, w3, w2, gate_row, final_gain=None):
    t, d = x.shape
    n_experts = router_w.shape[1]
    tg = MOE_ROW_TILE
    w_r = _pad_cols(router_w, LANES)
    b_r = _pad_cols(router_b.reshape(1, n_experts), LANES)
    h, route = norm_router(x, gain, mod, seq, w_r, b_r, n_experts)
    cum = row_cumsum(route, 1, t)
    chosen = route[:, n_experts + 2:n_experts + 4].astype(jnp.int32)
    rank = jnp.take_along_axis(cum[:, :n_experts] - route[:, :n_experts], chosen, axis=1).astype(jnp.int32)
    counts = cum[t - 1, :n_experts].astype(jnp.int32)
    padded = (counts + tg - 1) // tg * tg
    ends = jnp.cumsum(padded)
    pos = (jnp.take(ends - padded, chosen) + rank).T.reshape(-1)
    n_tiles = (TOP_K * t + n_experts * (tg - 1)) // tg
    n_valid = (ends[n_experts - 1] // tg).reshape(1)
    tile_expert = jnp.minimum(jnp.searchsorted(ends, jnp.arange(n_tiles, dtype=jnp.int32) * tg, side="right"),
                              n_experts - 1).astype(jnp.int32)
    src = invert_positions(pos, n_tiles * tg, t)
    w1_8, s1 = fp8_weight(w1)
    w3_8, s3 = fp8_weight(w3)
    act = moe_up(h, w1_8, w3_8, jnp.stack([s1, s3]).reshape(1, 2), src, tile_expert, n_valid)
    w2_8, s2 = fp8_weight(w2)
    y = moe_down(act, w2_8, s2.reshape(1, 1), tile_expert, n_valid)
    return moe_combine(y, pos, x, route, mod, seq, n_experts, gate_row, final_gain)


def _merge_kernel(ya_ref, yb_ref, yc_ref, wa_ref, wb_ref, wc_ref, ga_ref, gb_ref, gc_ref, o_ref):
    out = ga_ref[...].astype(F32) * jnp.dot(ya_ref[...], wa_ref[0], preferred_element_type=F32)
    out += gb_ref[...].astype(F32) * jnp.dot(yb_ref[...], wb_ref[0], preferred_element_type=F32)
    out += gc_ref[...].astype(F32) * jnp.dot(yc_ref[...], wc_ref[0], preferred_element_type=F32)
    o_ref[...] = out.astype(o_ref.dtype)


def merge_branches(y_a, y_b, y_c, w_branch, gates, seq):
    m, k = y_a.shape
    n = w_branch.shape[2]
    tm = _tile(seq, 1024)
    tn = _tile(n, 1024)
    nj = n // tn
    y_spec = pl.BlockSpec((tm, k), lambda i, j: (i, 0))

    def w_spec(b):
        return pl.BlockSpec((1, k, tn), lambda i, j: (b, 0, j))

    def g_spec(b):
        return pl.BlockSpec((tm, tn), lambda i, j: (i, b * nj + j))

    return pl.pallas_call(
        _merge_kernel,
        grid=(m // tm, nj),
        in_specs=[y_spec, y_spec, y_spec, w_spec(0), w_spec(1), w_spec(2), g_spec(0), g_spec(1), g_spec(2)],
        out_specs=pl.BlockSpec((tm, tn), lambda i, j: (i, j)),
        out_shape=jax.ShapeDtypeStruct((m, n), BF16),
        compiler_params=_cparams("parallel", "parallel"),
        name="merge_branches",
    )(y_a, y_b, y_c, w_branch, w_branch, w_branch, gates, gates, gates)


def _conv_kernel(h_ref, c_ref, b_ref, hp_ref, cp_ref, w_ref, bias_ref, o_ref, *, blocks_per_seq):
    u = c_ref[...].astype(F32) * h_ref[...].astype(F32)
    tm = u.shape[0]
    halo = hp_ref.shape[0]
    up = cp_ref[...].astype(F32) * hp_ref[...].astype(F32)
    first = (pl.program_id(0) % blocks_per_seq) == 0
    up = jnp.where(first, 0.0, up)
    p1 = up[halo - 1:halo, :]
    p2 = up[halo - 2:halo - 1, :]
    row = lax.broadcasted_iota(jnp.int32, (tm, 1), 0)
    u1 = jnp.where(row == 0, p1, pltpu.roll(u, 1, 0))
    u2 = jnp.where(row == 0, p2, jnp.where(row == 1, p1, pltpu.roll(u, 2, 0)))
    y = w_ref[0:1, :] * u2 + w_ref[1:2, :] * u1 + w_ref[2:3, :] * u + bias_ref[...]
    o_ref[...] = (b_ref[...].astype(F32) * y).astype(o_ref.dtype)


def short_conv(p, col0, conv_w, conv_b, seq):
    t = p.shape[0]
    w = conv_w.shape[1]
    tm = _tile(seq, 512)
    halo = 16
    ratio = tm // halo
    prev = lambda i: jnp.maximum(i * ratio - 1, 0)
    c0 = col0 // w
    return pl.pallas_call(
        functools.partial(_conv_kernel, blocks_per_seq=seq // tm),
        grid=(t // tm,),
        in_specs=[
            pl.BlockSpec((tm, w), lambda i: (i, c0)),
            pl.BlockSpec((tm, w), lambda i: (i, c0 + 1)),
            pl.BlockSpec((tm, w), lambda i: (i, c0 + 2)),
            pl.BlockSpec((halo, w), lambda i: (prev(i), c0)),
            pl.BlockSpec((halo, w), lambda i: (prev(i), c0 + 1)),
            pl.BlockSpec((3, w), lambda i: (0, 0)),
            pl.BlockSpec((1, w), lambda i: (0, 0)),
        ],
        out_specs=pl.BlockSpec((tm, w), lambda i: (i, 0)),
        out_shape=jax.ShapeDtypeStruct((t, w), BF16),
        compiler_params=_cparams("parallel"),
        name="short_conv",
    )(p, p, p, p, p, conv_w, conv_b.reshape(1, w))


def _block_cumsum(x_ref, cum_ref, carry_ref):
    @pl.when(pl.program_id(1) == 0)
    def _():
        carry_ref[...] = jnp.zeros_like(carry_ref)

    x = x_ref[...]
    n = x.shape[0]
    r = lax.broadcasted_iota(jnp.int32, (n, n), 0)
    c = lax.broadcasted_iota(jnp.int32, (n, n), 1)
    tri = (r >= c).astype(BF16)
    x1 = x.astype(BF16)
    r1 = x - x1.astype(F32)
    x2 = r1.astype(BF16)
    x3 = (r1 - x2.astype(F32)).astype(BF16)
    cum = (jnp.dot(tri, x1, preferred_element_type=F32) + jnp.dot(tri, x2, preferred_element_type=F32)
           + jnp.dot(tri, x3, preferred_element_type=F32)) + carry_ref[...]
    cum_ref[...] = cum
    carry_ref[...] = cum[n - 1:n, :]
    return cum


def _cumsum_kernel(x_ref, cum_ref, carry_ref):
    _block_cumsum(x_ref, cum_ref, carry_ref)


def row_cumsum(x, n_seq, seq):
    blk = _tile(seq, 256)
    nb = seq // blk
    return pl.pallas_call(
        _cumsum_kernel,
        grid=(n_seq, nb),
        in_specs=[pl.BlockSpec((blk, LANES), lambda b, i: (b * nb + i, 0))],
        out_specs=pl.BlockSpec((blk, LANES), lambda b, i: (b * nb + i, 0)),
        out_shape=jax.ShapeDtypeStruct((n_seq * seq, LANES), F32),
        scratch_shapes=[pltpu.VMEM((1, LANES), F32)],
        compiler_params=_cparams("parallel", "arbitrary"),
        name="row_cumsum",
    )(x)


FOX_GROUP = 2
LOG2_E = 1.4426950408889634


def _fox_bias_columns(cum_rows, head, value_lane, ones_lane, ones):
    lane = lax.broadcasted_iota(jnp.int32, cum_rows.shape, 1)
    c = jnp.sum(jnp.where(lane == head, cum_rows, 0.0), axis=1, keepdims=True) * LOG2_E
    hi = c.astype(BF16).astype(F32)
    r1 = c - hi
    mid = r1.astype(BF16).astype(F32)
    lo = r1 - mid
    out = jnp.where((lane >= ones_lane) & (lane < ones_lane + 3), ones, 0.0)
    out = jnp.where(lane == value_lane, hi, out)
    out = jnp.where(lane == value_lane + 1, mid, out)
    out = jnp.where(lane == value_lane + 2, lo, out)
    return out.astype(BF16)


def _fox_kernel(q_ref, k_ref, v_ref, cq_ref, ck_ref, o_ref, kaug_ref, vt_ref, qaug_ref, s_ref, m_ref, l_ref,
                acc_ref):
    hp = pl.program_id(1)
    i = pl.program_id(2)
    tq = q_ref.shape[0]
    tk = tq
    seq = k_ref.shape[0]

    @pl.when(i == 0)
    def _():
        def fill(c, carry):
            rows = pl.ds(pl.multiple_of(c * tk, tk), tk)
            for g in range(FOX_GROUP):
                cols = slice(g * HEAD_DIM, (g + 1) * HEAD_DIM)
                kaug_ref[g, rows, 0:HEAD_DIM] = k_ref[rows, cols]
                kaug_ref[g, rows, HEAD_DIM:2 * HEAD_DIM] = _fox_bias_columns(
                    ck_ref[rows, :], hp * FOX_GROUP + g, 0, 3, 1.0)
                vt_ref[g, :, rows] = v_ref[rows, cols].astype(F32).T.astype(BF16)
            return carry

        lax.fori_loop(0, seq // tk, fill, 0)

    for g in range(FOX_GROUP):
        cols = slice(g * HEAD_DIM, (g + 1) * HEAD_DIM)
        qaug_ref[g, :, 0:HEAD_DIM] = q_ref[:, cols]
        qaug_ref[g, :, HEAD_DIM:2 * HEAD_DIM] = _fox_bias_columns(cq_ref[...], hp * FOX_GROUP + g, 3, 0, -1.0)
    m_ref[...] = jnp.full_like(m_ref, NEG_INF)
    l_ref[...] = jnp.zeros_like(l_ref)
    acc_ref[...] = jnp.zeros_like(acc_ref)
    key = lax.broadcasted_iota(jnp.int32, (tk, tq), 0)
    qry = lax.broadcasted_iota(jnp.int32, (tk, tq), 1)

    def score(j, slot):
        start = pl.multiple_of(j * tk, tk)
        for g in range(FOX_GROUP):
            s_ref[slot, g] = lax.dot_general(kaug_ref[g, pl.ds(start, tk), :], qaug_ref[g],
                                             (((1,), (1,)), ((), ())), preferred_element_type=F32)

    def consume(j, slot, on_diagonal):
        start = pl.multiple_of(j * tk, tk)
        for g in range(FOX_GROUP):
            st = s_ref[slot, g]
            if on_diagonal:
                st = jnp.where(key <= qry, st, NEG_INF)
            m_old = m_ref[g]
            m_new = jnp.maximum(m_old, jnp.max(st, axis=0, keepdims=True))
            alpha = jnp.exp2(m_old - m_new)
            pt = jnp.exp2(st - m_new)
            l_ref[g] = alpha * l_ref[g] + jnp.sum(pt, axis=0, keepdims=True)
            acc_ref[g] = alpha * acc_ref[g] + jnp.dot(vt_ref[g, :, pl.ds(start, tk)], pt.astype(BF16),
                                                      preferred_element_type=F32)
            m_ref[g] = m_new

    def two_below_diagonal(p, carry):
        j = 2 * p
        score(j + 1, 1)
        consume(j, 0, False)
        score(j + 2, 0)
        consume(j + 1, 1, False)
        return carry

    score(0, 0)
    lax.fori_loop(0, i // 2, two_below_diagonal, 0)

    @pl.when(i % 2 == 0)
    def _():
        consume(i, 0, True)

    @pl.when(i % 2 == 1)
    def _():
        score(i, 1)
        consume(i - 1, 0, False)
        consume(i, 1, True)

    for g in range(FOX_GROUP):
        o_ref[:, g * HEAD_DIM:(g + 1) * HEAD_DIM] = (acc_ref[g] / l_ref[g]).T.astype(o_ref.dtype)


def fox_attention(p, col0, cum, batch, seq):
    tq = _tile(seq, 512)
    nq = seq // tq
    gw = FOX_GROUP * HEAD_DIM
    n_groups = N_HEADS // FOX_GROUP
    q0 = col0 // gw
    return pl.pallas_call(
        _fox_kernel,
        grid=(batch, n_groups, nq),
        in_specs=[
            pl.BlockSpec((tq, gw), lambda b, h, i: (b * nq + i, q0 + h)),
            pl.BlockSpec((seq, gw), lambda b, h, i: (b, q0 + n_groups + h)),
            pl.BlockSpec((seq, gw), lambda b, h, i: (b, q0 + 2 * n_groups + h)),
            pl.BlockSpec((tq, LANES), lambda b, h, i: (b * nq + i, 0)),
            pl.BlockSpec((seq, LANES), lambda b, h, i: (b, 0)),
        ],
        out_specs=pl.BlockSpec((tq, gw), lambda b, h, i: (b * nq + i, h)),
        out_shape=jax.ShapeDtypeStruct((batch * seq, WIDTH), BF16),
        scratch_shapes=[pltpu.VMEM((FOX_GROUP, seq, 2 * HEAD_DIM), BF16),
                        pltpu.VMEM((FOX_GROUP, HEAD_DIM, seq), BF16),
                        pltpu.VMEM((FOX_GROUP, tq, 2 * HEAD_DIM), BF16),
                        pltpu.VMEM((2, FOX_GROUP, tq, tq), F32),
                        pltpu.VMEM((FOX_GROUP, 1, tq), F32), pltpu.VMEM((FOX_GROUP, 1, tq), F32),
                        pltpu.VMEM((FOX_GROUP, HEAD_DIM, tq), F32)],
        compiler_params=_cparams("parallel", "parallel", "arbitrary"),
        name="fox_attention",
    )(p, p, p, cum, cum)


def _hgrn_levels():
    sizes = []
    half = HGRN_CHUNK // 2
    while half >= HGRN_DIAG:
        sizes.append(half)
        half //= 2
    return sizes


def _hgrn_coefficients():
    c = HGRN_CHUNK
    t = np.arange(c)[:, None]
    u = np.arange(c)[None, :]
    slabs = [(u <= t), (u > t)]
    for size in _hgrn_levels():
        ref = (t // (2 * size)) * (2 * size) + size - 1
        upper = (t % (2 * size)) >= size
        slabs.append(np.where(upper, (u > ref) & (u <= t), (u > t) & (u <= ref)))
    slabs.append((u <= t) & (u // HGRN_DIAG == t // HGRN_DIAG))
    coef = np.concatenate(slabs, axis=0).astype(np.float32)
    return np.concatenate([coef, coef], axis=1)


def _hgrn_level_masks():
    c = HGRN_CHUNK
    t = np.arange(c)[:, None]
    s = np.arange(c)[None, :]
    masks = []
    for size in _hgrn_levels():
        same = (t // (2 * size)) == (s // (2 * size))
        masks.append(same & ((t % (2 * size)) >= size) & ((s % (2 * size)) < size))
    masks.append((t // HGRN_DIAG == s // HGRN_DIAG) & (s <= t))
    return np.stack(masks).astype(np.float32)


def _hgrn_placement():
    place = np.zeros((HGRN_DIAG, HEAD_DIM, HGRN_CHUNK), np.float32)
    for j in range(HGRN_DIAG):
        place[j, :, j::HGRN_DIAG] = 1.0
    return place.reshape(HGRN_DIAG * HEAD_DIM, HGRN_CHUNK)


def _hgrn_kernel(q_ref, f_ref, i_ref, g_ref, lb_ref, norm_ref, coef_ref, mask_ref, place_ref, o_ref,
                 st_ref, e_ref, kf_ref, *, layer):
    c = HGRN_CHUNK
    blk_rows = HGRN_DIAG
    n_lev = len(_hgrn_levels())
    b_in_rows = (2 + n_lev) * c

    @pl.when(pl.program_id(1) == 0)
    def _():
        st_ref[...] = jnp.zeros_like(st_ref)

    lbr = lb_ref[...]
    le = jnp.exp(lbr - jnp.max(lbr, axis=0, keepdims=True))
    lp = le / jnp.sum(le, axis=0, keepdims=True)
    lb = jnp.zeros((1, WIDTH), F32)
    for r in range(1, layer + 1):
        lb = lb + lp[r:r + 1, :]

    f = lb + (1.0 - lb) * _sigmoid(f_ref[...].astype(F32))
    g = jnp.log(f)
    kf_ref[...] = 1.0 - f
    g_hi, g_lo = _split_bf16(g)
    e_ref[...] = jnp.dot(coef_ref[...], jnp.concatenate([g_hi, g_lo], axis=0), preferred_element_type=F32)

    def head(h, carry):
        lanes = pl.ds(pl.multiple_of(h * HEAD_DIM, HEAD_DIM), HEAD_DIM)
        q = q_ref[:, lanes].astype(F32)
        k = kf_ref[:, lanes]
        v = i_ref[:, lanes]
        st = st_ref[h]
        qe = (q * jnp.exp(e_ref[0:c, lanes])).astype(BF16)
        out = lax.dot_general(qe, st.astype(BF16), (((1,), (1,)), ((), ())), preferred_element_type=F32)
        attn = jnp.zeros((c, c), F32)
        for lev in range(n_lev):
            pw = jnp.exp(e_ref[(2 + lev) * c:(3 + lev) * c, lanes])
            a = lax.dot_general((q * pw).astype(BF16), (k * pw).astype(BF16), (((1,), (1,)), ((), ())),
                                preferred_element_type=F32)
            attn = attn + jnp.where(mask_ref[lev] > 0.0, a, 0.0)
        b_in = e_ref[b_in_rows:b_in_rows + c, lanes]

        def block_row(ref, base, j):
            rows = [jnp.broadcast_to(ref[pl.ds(base + blk * blk_rows + j, 1), lanes], (blk_rows, HEAD_DIM))
                    for blk in range(c // blk_rows)]
            return jnp.concatenate(rows, axis=0)

        z = []
        for j in range(blk_rows):
            k_j = block_row(kf_ref, 0, j)
            b_j = block_row(e_ref, b_in_rows, j)
            z.append((q * k_j * jnp.exp(jnp.minimum(b_in - b_j, 0.0))).astype(BF16))
        diag = jnp.dot(jnp.concatenate(z, axis=1), place_ref[...], preferred_element_type=F32)
        attn = attn + jnp.where(mask_ref[n_lev] > 0.0, diag, 0.0)
        out = out + jnp.dot(attn.astype(BF16), v, preferred_element_type=F32)
        ke = (k * jnp.exp(e_ref[c:2 * c, lanes])).astype(BF16)
        decay = jnp.exp(e_ref[c - 1:c, lanes])
        st_ref[h] = st * decay + lax.dot_general(v, ke, (((0,), (0,)), ((), ())), preferred_element_type=F32)
        out = out * lax.rsqrt(jnp.mean(out * out, axis=-1, keepdims=True) + EPS)
        out = out * norm_ref[:, lanes] * _silu(g_ref[:, lanes].astype(F32))
        o_ref[:, lanes] = out.astype(o_ref.dtype)
        return carry

    lax.fori_loop(0, N_HEADS, head, 0, unroll=4)


def hgrn_mixer(p, col0, lower_bounds, norm, layer, batch, seq):
    c = HGRN_CHUNK
    nc = seq // c
    depth = lower_bounds.shape[0]
    coef = jnp.asarray(_hgrn_coefficients(), BF16)
    masks = jnp.asarray(_hgrn_level_masks(), F32)
    place = jnp.asarray(_hgrn_placement(), BF16)
    n_slab = coef.shape[0] // c

    def col_spec(j):
        return pl.BlockSpec((c, WIDTH), lambda b, n: (b * nc + n, col0 // WIDTH + j))

    return pl.pallas_call(
        functools.partial(_hgrn_kernel, layer=layer),
        grid=(batch, nc),
        in_specs=[col_spec(0), col_spec(1), col_spec(2), col_spec(3),
                  pl.BlockSpec((depth, WIDTH), lambda b, n: (0, 0)),
                  pl.BlockSpec((1, WIDTH), lambda b, n: (0, 0)),
                  pl.BlockSpec(coef.shape, lambda b, n: (0, 0)),
                  pl.BlockSpec(masks.shape, lambda b, n: (0, 0, 0)),
                  pl.BlockSpec(place.shape, lambda b, n: (0, 0))],
        out_specs=pl.BlockSpec((c, WIDTH), lambda b, n: (b * nc + n, 0)),
        out_shape=jax.ShapeDtypeStruct((batch * seq, WIDTH), BF16),
        scratch_shapes=[pltpu.VMEM((N_HEADS, HEAD_DIM, HEAD_DIM), F32),
                        pltpu.VMEM((n_slab * c, WIDTH), F32),
                        pltpu.VMEM((c, WIDTH), F32)],
        compiler_params=_cparams("parallel", "arbitrary"),
        name="hgrn_mixer",
    )(p, p, p, p, lower_bounds, norm.reshape(1, WIDTH), coef, masks, place)


def _pad_cols(w, n):
    return jnp.pad(w, ((0, 0), (0, n - w.shape[1])))


def _round_up(n, m):
    return ((n + m - 1) // m) * m


def kernel(x, c, norm_mix, norm_ffn, w_ada, b_ada, w_in, b_gate, fox_b_f, hgrn_lower_bounds, hgrn_norm,
           conv_w, conv_b, w_branch, w_o, ffn_w1, ffn_w3, ffn_w2, router_w, router_b,
           expert_w1, expert_w3, expert_w2, norm_final):
    batch, seq, d = x.shape
    depth = w_ada.shape[0]
    t = batch * seq

    o_fox = 0
    o_ff = 3 * WIDTH
    o_hgrn = o_ff + N_HEADS
    o_conv = o_hgrn + 4 * WIDTH
    o_gate = o_conv + 3 * WIDTH

    mod_all = ada_modulation(c, w_ada, b_ada)
    xt = x.reshape(t, d)
    for layer in range(depth):
        mod = mod_all[layer]
        wl = w_in[layer]
        w_mix = jnp.concatenate([wl[:, o_fox:o_ff], wl[:, o_hgrn:o_gate]], axis=1).astype(BF16)
        c_hgrn = 3 * WIDTH
        c_conv = c_hgrn + 4 * WIDTH
        col_scale = jnp.ones((w_mix.shape[1],), F32).at[:WIDTH].set(HEAD_DIM ** -0.5 * LOG2_E)
        w_ff = _pad_cols(wl[:, o_ff:o_hgrn], LANES).astype(BF16)
        w_gate = wl[:, o_gate:]
        b_ff = _pad_cols(fox_b_f[layer].reshape(1, N_HEADS), LANES)

        h, lsf = norm_forget(xt, norm_mix[layer], mod, seq, w_ff, b_ff)
        p = project(h, w_mix, col_scale, seq)
        gates = gate_project(h, w_gate, b_gate[layer], seq)

        y_a = hgrn_mixer(p, c_hgrn, hgrn_lower_bounds, hgrn_norm[layer], layer, batch, seq)
        y_b = short_conv(p, c_conv, conv_w[layer], conv_b[layer], seq)
        cum = row_cumsum(lsf, batch, seq)
        y_c = fox_attention(p, 0, cum, batch, seq)

        merged = merge_branches(y_a, y_b, y_c, w_branch[layer].astype(BF16), gates, seq)
        xt = residual_project(merged, w_o[layer].astype(BF16), xt, mod, seq, 2, d // 2)

        i = layer // 2
        if layer % 2 == 0:
            dff = ffn_w1.shape[2]
            dff_pad = _round_up(dff, 1024)
            w1 = _pad_cols(ffn_w1[i], dff_pad).astype(BF16)
            w3 = _pad_cols(ffn_w3[i], dff_pad).astype(BF16)
            w2 = jnp.pad(ffn_w2[i], ((0, dff_pad - dff), (0, 0))).astype(BF16)
            h2 = norm_only(xt, norm_ffn[layer], mod, seq)
            act = glu(h2, w1, w3, seq)
            xt = residual_project(act, w2, xt, mod, seq, 5, dff_pad // 4)
        else:
            last = layer == depth - 1
            xt = moe_ffn(xt, norm_ffn[layer], mod, seq, router_w[i], router_b[i],
                         expert_w1[i], expert_w3[i], expert_w2[i], 5,
                         final_gain=norm_final if last else None)
    if depth % 2:
        xt = final_norm(xt, norm_final)
    return xt.reshape(batch, seq, d)
```

```python
import functools

import jax
import jax.numpy as jnp
import numpy as np
from jax import lax
from jax.experimental import pallas as pl
from jax.experimental.pallas import tpu as pltpu

F32 = jnp.float32
BF16 = jnp.bfloat16
FP8 = jnp.float8_e4m3fn
FP8_TARGET = 256.0

N_HEADS = 8
HEAD_DIM = 128
WIDTH = N_HEADS * HEAD_DIM
TOP_K = 2
EPS = 1e-6
NEG_INF = -1e30
LANES = 128
HGRN_CHUNK = 128
HGRN_DIAG = 16
NORM_ROWS = 512
VMEM_LIMIT = 56 * 1024 * 1024


def _cparams(*sem):
    return pltpu.CompilerParams(dimension_semantics=sem, vmem_limit_bytes=VMEM_LIMIT)


def _tile(n, pref):
    t = min(n, pref)
    while n % t:
        t //= 2
    return t


def _sigmoid(z):
    return 1.0 / (1.0 + jnp.exp(-z))


def _silu(z):
    return z * _sigmoid(z)


def _ada_kernel(ct_ref, w_ref, b_ref, o_ref, act_ref, *, batch):
    k = pl.program_id(2)
    tk = w_ref.shape[1]
    tn = o_ref.shape[2]

    @pl.when((pl.program_id(0) == 0) & (pl.program_id(1) == 0) & (k == 0))
    def _():
        act = _silu(ct_ref[...])
        for b in range(batch):
            act_ref[b] = jnp.broadcast_to(act[:, b:b + 1], act_ref.shape[1:])

    @pl.when(k == 0)
    def _():
        row = lax.broadcasted_iota(jnp.int32, o_ref.shape[1:], 0)
        o_ref[0] = jnp.where(row < batch, b_ref[0], 0.0)

    rows = pl.ds(pl.multiple_of(k * tk, tk), tk)
    for b in range(batch):
        a = act_ref[b, rows, :]
        for jb in range(tn // LANES):
            cols = slice(jb * LANES, (jb + 1) * LANES)
            o_ref[0, b:b + 1, cols] += jnp.sum(w_ref[0, :, cols] * a, axis=0, keepdims=True)


def ada_modulation(c, w_ada, b_ada):
    depth, d, n = w_ada.shape
    b = c.shape[0]
    rows = _round_up(b, 8)
    tn = _tile(n, 2048)
    tk = _tile(d, 1024)
    out = pl.pallas_call(
        functools.partial(_ada_kernel, batch=b),
        grid=(depth, n // tn, d // tk),
        in_specs=[
            pl.BlockSpec((d, b), lambda l, j, k: (0, 0)),
            pl.BlockSpec((1, tk, tn), lambda l, j, k: (l, k, j)),
            pl.BlockSpec((1, 1, tn), lambda l, j, k: (l, 0, j)),
        ],
        out_specs=pl.BlockSpec((1, rows, tn), lambda l, j, k: (l, 0, j)),
        out_shape=jax.ShapeDtypeStruct((depth, rows, n), F32),
        scratch_shapes=[pltpu.VMEM((b, d, LANES), F32)],
        compiler_params=_cparams("arbitrary", "arbitrary", "arbitrary"),
        name="ada_modulation",
    )(c.T, w_ada, b_ada.reshape(depth, 1, n))
    return out[:, :b].reshape(depth, b, 6, d)


def _norm_mod(x_ref, g_ref, mod_ref, shift_row, scale_row):
    x = x_ref[...]
    y = x * lax.rsqrt(jnp.mean(x * x, axis=-1, keepdims=True) + EPS) * g_ref[...]
    return y * (1.0 + mod_ref[0, scale_row:scale_row + 1, :]) + mod_ref[0, shift_row:shift_row + 1, :]


def _log_sigmoid(z):
    return jnp.minimum(z, 0.0) - jnp.log(1.0 + jnp.exp(-jnp.abs(z)))


def _norm_forget_kernel(x_ref, g_ref, mod_ref, wf_ref, bf_ref, h_ref, lsf_ref, *, shift_row, scale_row):
    h = _norm_mod(x_ref, g_ref, mod_ref, shift_row, scale_row).astype(BF16)
    h_ref[...] = h
    z = jnp.dot(h, wf_ref[...], preferred_element_type=F32) + bf_ref[...]
    lsf_ref[...] = _log_sigmoid(z)


def _split_bf16(v):
    hi = v.astype(BF16)
    lo = (v - hi.astype(F32)).astype(BF16)
    return hi, lo


def _norm_router_kernel(x_ref, g_ref, mod_ref, wr_ref, br_ref, h_ref, route_ref, *,
                        shift_row, scale_row, n_experts):
    h = _norm_mod(x_ref, g_ref, mod_ref, shift_row, scale_row)
    h_ref[...] = h
    h_hi, h_lo = _split_bf16(h)
    w_hi, w_lo = _split_bf16(wr_ref[...])
    logits = (jnp.dot(h_hi, w_hi, preferred_element_type=F32)
              + jnp.dot(h_hi, w_lo, preferred_element_type=F32)
              + jnp.dot(h_lo, w_hi, preferred_element_type=F32)) + br_ref[...]
    lane = lax.broadcasted_iota(jnp.int32, logits.shape, 1)
    lg = jnp.where(lane < n_experts, logits, -jnp.inf)
    m1 = jnp.max(lg, axis=1, keepdims=True)
    i1 = jnp.min(jnp.where(lg == m1, lane, LANES), axis=1, keepdims=True)
    lg2 = jnp.where(lane == i1, -jnp.inf, lg)
    m2 = jnp.max(lg2, axis=1, keepdims=True)
    i2 = jnp.min(jnp.where(lg2 == m2, lane, LANES), axis=1, keepdims=True)
    e = jnp.exp(m2 - m1)
    w1 = 1.0 / (1.0 + e)
    w2 = e / (1.0 + e)
    rec = jnp.where((lane == i1) | (lane == i2), 1.0, 0.0)
    rec = jnp.where(lane == n_experts, w1, rec)
    rec = jnp.where(lane == n_experts + 1, w2, rec)
    rec = jnp.where(lane == n_experts + 2, i1.astype(F32), rec)
    rec = jnp.where(lane == n_experts + 3, i2.astype(F32), rec)
    route_ref[...] = rec


def _norm_only_kernel(x_ref, g_ref, mod_ref, h_ref, *, shift_row, scale_row):
    h_ref[...] = _norm_mod(x_ref, g_ref, mod_ref, shift_row, scale_row).astype(BF16)


def _norm_call(body, x, gain, mod, seq, extra_in, extra_specs, extra_out, extra_out_specs, tm, h_dtype=BF16):
    t, d = x.shape
    per_seq = seq // tm
    in_specs = [
        pl.BlockSpec((tm, d), lambda i: (i, 0)),
        pl.BlockSpec((1, d), lambda i: (0, 0)),
        pl.BlockSpec((1, 6, d), lambda i: (i // per_seq, 0, 0)),
    ] + extra_specs
    out_shape = [jax.ShapeDtypeStruct((t, d), h_dtype)] + extra_out
    out_specs = [pl.BlockSpec((tm, d), lambda i: (i, 0))] + extra_out_specs
    return pl.pallas_call(
        body,
        grid=(t // tm,),
        in_specs=in_specs,
        out_specs=out_specs,
        out_shape=out_shape,
        compiler_params=_cparams("parallel"),
        name="norm_mod",
    )(x, gain.reshape(1, d), mod, *extra_in)


def norm_forget(x, gain, mod, seq, w_f, b_f):
    t, d = x.shape
    tm = _tile(seq, NORM_ROWS)
    body = functools.partial(_norm_forget_kernel, shift_row=0, scale_row=1)
    return _norm_call(
        body, x, gain, mod, seq, [w_f, b_f],
        [pl.BlockSpec((d, LANES), lambda i: (0, 0)), pl.BlockSpec((1, LANES), lambda i: (0, 0))],
        [jax.ShapeDtypeStruct((t, LANES), F32)], [pl.BlockSpec((tm, LANES), lambda i: (i, 0))], tm)


def norm_router(x, gain, mod, seq, w_r, b_r, n_experts):
    t, d = x.shape
    assert n_experts + 4 <= LANES
    tm = _tile(seq, NORM_ROWS)
    body = functools.partial(_norm_router_kernel, shift_row=3, scale_row=4, n_experts=n_experts)
    return _norm_call(
        body, x, gain, mod, seq, [w_r, b_r],
        [pl.BlockSpec((d, LANES), lambda i: (0, 0)), pl.BlockSpec((1, LANES), lambda i: (0, 0))],
        [jax.ShapeDtypeStruct((t, LANES), F32)], [pl.BlockSpec((tm, LANES), lambda i: (i, 0))], tm,
        h_dtype=F32)


def norm_only(x, gain, mod, seq):
    tm = _tile(seq, NORM_ROWS)
    body = functools.partial(_norm_only_kernel, shift_row=3, scale_row=4)
    return _norm_call(body, x, gain, mod, seq, [], [], [], [], tm)[0]


def _final_norm_kernel(x_ref, g_ref, o_ref):
    x = x_ref[...]
    o_ref[...] = x * lax.rsqrt(jnp.mean(x * x, axis=-1, keepdims=True) + EPS) * g_ref[...]


def final_norm(x, gain):
    t, d = x.shape
    tm = _tile(t, NORM_ROWS)
    return pl.pallas_call(
        _final_norm_kernel,
        grid=(t // tm,),
        in_specs=[pl.BlockSpec((tm, d), lambda i: (i, 0)), pl.BlockSpec((1, d), lambda i: (0, 0))],
        out_specs=pl.BlockSpec((tm, d), lambda i: (i, 0)),
        out_shape=jax.ShapeDtypeStruct((t, d), F32),
        compiler_params=_cparams("parallel"),
        name="final_norm",
    )(x, gain.reshape(1, d))


def _proj_kernel(a_ref, w_ref, s_ref, o_ref):
    acc = jnp.dot(a_ref[...], w_ref[...], preferred_element_type=F32)
    o_ref[...] = (acc * s_ref[...]).astype(o_ref.dtype)


def _pow2_scale(amax):
    return jnp.exp2(jnp.floor(jnp.log2(FP8_TARGET / jnp.maximum(amax, 1e-30))))


def fp8_weight(w):
    scale = _pow2_scale(jnp.max(jnp.abs(w)).astype(F32))
    return (w.astype(F32) * scale).astype(FP8), scale


def _gate_proj_kernel(a_ref, w_ref, b_ref, ws_ref, o_ref, a8_ref, inv_ref):
    @pl.when(pl.program_id(1) == 0)
    def _():
        a = a_ref[...].astype(F32)
        s = _pow2_scale(jnp.max(jnp.abs(a), axis=1, keepdims=True))
        a8_ref[...] = (a * s).astype(FP8)
        inv_ref[...] = 1.0 / (s * ws_ref[...])

    acc = jnp.dot(a8_ref[...], w_ref[...], preferred_element_type=F32)
    o_ref[...] = _sigmoid(acc * inv_ref[...] + b_ref[...]).astype(o_ref.dtype)


def _mm_tiles(m, n, seq):
    return _tile(seq, 1024), _tile(n, 1024)


def project(a, w, col_scale, seq):
    m, k = a.shape
    n = w.shape[1]
    tm, tn = _mm_tiles(m, n, seq)
    return pl.pallas_call(
        _proj_kernel,
        grid=(m // tm, n // tn),
        in_specs=[pl.BlockSpec((tm, k), lambda i, j: (i, 0)), pl.BlockSpec((k, tn), lambda i, j: (0, j)),
                  pl.BlockSpec((1, tn), lambda i, j: (0, j))],
        out_specs=pl.BlockSpec((tm, tn), lambda i, j: (i, j)),
        out_shape=jax.ShapeDtypeStruct((m, n), BF16),
        compiler_params=_cparams("parallel", "parallel"),
        name="project",
    )(a, w, col_scale.reshape(1, n))


def gate_project(a, w, bias, seq):
    m, k = a.shape
    n = w.shape[1]
    tm, tn = _mm_tiles(m, n, seq)
    w8, w_scale = fp8_weight(w)
    return pl.pallas_call(
        _gate_proj_kernel,
        grid=(m // tm, n // tn),
        in_specs=[pl.BlockSpec((tm, k), lambda i, j: (i, 0)), pl.BlockSpec((k, tn), lambda i, j: (0, j)),
                  pl.BlockSpec((1, tn), lambda i, j: (0, j)), pl.BlockSpec((1, 1), lambda i, j: (0, 0))],
        out_specs=pl.BlockSpec((tm, tn), lambda i, j: (i, j)),
        out_shape=jax.ShapeDtypeStruct((m, n), BF16),
        scratch_shapes=[pltpu.VMEM((tm, k), FP8), pltpu.VMEM((tm, 1), F32)],
        compiler_params=_cparams("parallel", "arbitrary"),
        name="gate_project",
    )(a, w8, bias.reshape(1, n), w_scale.reshape(1, 1))


def _residual_kernel(a_ref, w_ref, x_ref, mod_ref, o_ref, acc_ref, *, gate_row):
    k = pl.program_id(2)

    @pl.when(k == 0)
    def _():
        acc_ref[...] = jnp.zeros_like(acc_ref)

    acc_ref[...] += jnp.dot(a_ref[...], w_ref[...], preferred_element_type=F32)

    @pl.when(k == pl.num_programs(2) - 1)
    def _():
        o_ref[...] = x_ref[...] + mod_ref[0, gate_row:gate_row + 1, :] * acc_ref[...]


def residual_project(a, w, x, mod, seq, gate_row, tk_pref):
    m, kdim = a.shape
    n = w.shape[1]
    tm, tn = _mm_tiles(m, n, seq)
    tk = _tile(kdim, tk_pref)
    per_seq = seq // tm
    return pl.pallas_call(
        functools.partial(_residual_kernel, gate_row=gate_row),
        grid=(m // tm, n // tn, kdim // tk),
        in_specs=[
            pl.BlockSpec((tm, tk), lambda i, j, k: (i, k)),
            pl.BlockSpec((tk, tn), lambda i, j, k: (k, j)),
            pl.BlockSpec((tm, tn), lambda i, j, k: (i, j)),
            pl.BlockSpec((1, 6, tn), lambda i, j, k: (i // per_seq, 0, j)),
        ],
        out_specs=pl.BlockSpec((tm, tn), lambda i, j, k: (i, j)),
        out_shape=jax.ShapeDtypeStruct((m, n), F32),
        scratch_shapes=[pltpu.VMEM((tm, tn), F32)],
        compiler_params=_cparams("parallel", "parallel", "arbitrary"),
        name="residual_project",
    )(a, w, x, mod)


def _glu_kernel(h_ref, w1_ref, w3_ref, o_ref):
    h = h_ref[...]
    a = jnp.dot(h, w1_ref[...], preferred_element_type=F32)
    b = jnp.dot(h, w3_ref[...], preferred_element_type=F32)
    o_ref[...] = (_silu(a) * b).astype(o_ref.dtype)


def glu(h, w1, w3, seq):
    m, k = h.shape
    n = w1.shape[1]
    tm = _tile(seq, 1024)
    tn = _tile(n, 512)
    return pl.pallas_call(
        _glu_kernel,
        grid=(m // tm, n // tn),
        in_specs=[pl.BlockSpec((tm, k), lambda i, j: (i, 0)),
                  pl.BlockSpec((k, tn), lambda i, j: (0, j)),
                  pl.BlockSpec((k, tn), lambda i, j: (0, j))],
        out_specs=pl.BlockSpec((tm, tn), lambda i, j: (i, j)),
        out_shape=jax.ShapeDtypeStruct((m, n), BF16),
        compiler_params=_cparams("parallel", "parallel"),
        name="glu",
    )(h, w1, w3)


MOE_ROW_TILE = 512


def _invert_kernel(pos_ref, src_ref, *, n_tokens):
    def clear(p, carry):
        src_ref[p] = 0
        return carry

    lax.fori_loop(0, src_ref.shape[0], clear, 0, unroll=8)

    def place(t, carry):
        src_ref[pos_ref[t]] = t
        src_ref[pos_ref[n_tokens + t]] = t
        return carry

    lax.fori_loop(0, n_tokens, place, 0, unroll=8)


def invert_positions(pos, n_slots, n_tokens):
    return pl.pallas_call(
        functools.partial(_invert_kernel, n_tokens=n_tokens),
        in_specs=[pl.BlockSpec(memory_space=pltpu.SMEM)],
        out_specs=pl.BlockSpec(memory_space=pltpu.SMEM),
        out_shape=jax.ShapeDtypeStruct((n_slots,), jnp.int32),
        name="invert_positions",
    )(pos)


def _moe_up_kernel(src_ref, texp_ref, nv_ref, h_hbm, w1_ref, w3_ref, ws_ref, o_ref, buf, xs, inv, sem):
    i = pl.program_id(0)
    j = pl.program_id(1)
    nv = nv_ref[0]
    tm = xs.shape[0]
    slot = i % 2

    def row_copy(tile, r, s):
        tok = src_ref[tile * tm + r]
        return pltpu.make_async_copy(h_hbm.at[pl.ds(tok, 1), :], buf.at[s, pl.ds(r, 1), :], sem.at[s])

    def start_gather(tile, s):
        def body(r, carry):
            row_copy(tile, r, s).start()
            return carry
        lax.fori_loop(0, tm, body, 0, unroll=8)

    def wait_gather(tile, s):
        def body(r, carry):
            row_copy(tile, r, s).wait()
            return carry
        lax.fori_loop(0, tm, body, 0, unroll=8)

    @pl.when((j == 0) & (i < nv))
    def _():
        @pl.when(i == 0)
        def _():
            start_gather(0, 0)

        @pl.when(i + 1 < nv)
        def _():
            start_gather(i + 1, 1 - slot)

        wait_gather(i, slot)
        rows = buf[slot]
        s = _pow2_scale(jnp.max(jnp.abs(rows), axis=1, keepdims=True))
        xs[...] = (rows * s).astype(FP8)
        inv[...] = 1.0 / s

    @pl.when(i < nv)
    def _():
        x = xs[...]
        a = jnp.dot(x, w1_ref[0], preferred_element_type=F32) * (inv[...] / ws_ref[:, 0:1])
        b = jnp.dot(x, w3_ref[0], preferred_element_type=F32) * (inv[...] / ws_ref[:, 1:2])
        o_ref[...] = (_silu(a) * b).astype(o_ref.dtype)

    @pl.when(i >= nv)
    def _():
        o_ref[...] = jnp.zeros_like(o_ref)


MOE_UP_COLS = 256


def moe_up(h, w1, w3, w_scale, src, tile_expert, n_valid):
    d = h.shape[1]
    f = w1.shape[2]
    tm = MOE_ROW_TILE
    n_tiles = src.shape[0] // tm
    tn = _tile(f, MOE_UP_COLS)
    nj = f // tn

    def w_map(i, j, src, texp, nv):
        ie = jnp.minimum(i, nv[0] - 1)
        return (texp[ie], 0, jnp.where(i < nv[0], j, nj - 1))

    grid_spec = pltpu.PrefetchScalarGridSpec(
        num_scalar_prefetch=3,
        grid=(n_tiles, nj),
        in_specs=[pl.BlockSpec(memory_space=pl.ANY),
                  pl.BlockSpec((1, d, tn), w_map),
                  pl.BlockSpec((1, d, tn), w_map),
                  pl.BlockSpec((1, 2), lambda i, j, src, texp, nv: (0, 0))],
        out_specs=pl.BlockSpec((tm, tn), lambda i, j, src, texp, nv: (i, j)),
        scratch_shapes=[pltpu.VMEM((2, tm, d), F32), pltpu.VMEM((tm, d), FP8), pltpu.VMEM((tm, 1), F32),
                        pltpu.SemaphoreType.DMA((2,))],
    )
    return pl.pallas_call(
        _moe_up_kernel,
        grid_spec=grid_spec,
        out_shape=jax.ShapeDtypeStruct((n_tiles * tm, f), BF16),
        compiler_params=_cparams("arbitrary", "arbitrary"),
        name="moe_up",
    )(src, tile_expert, n_valid, h, w1, w3, w_scale)


def _moe_down_kernel(texp_ref, nv_ref, a_ref, w_ref, ws_ref, o_ref, a8_ref, inv_ref):
    valid = pl.program_id(0) < nv_ref[0]

    @pl.when(valid & (pl.program_id(1) == 0))
    def _():
        a = a_ref[...].astype(F32)
        s = _pow2_scale(jnp.max(jnp.abs(a), axis=1, keepdims=True))
        a8_ref[...] = (a * s).astype(FP8)
        inv_ref[...] = 1.0 / (s * ws_ref[...])

    @pl.when(valid)
    def _():
        o_ref[...] = jnp.dot(a8_ref[...], w_ref[0], preferred_element_type=F32) * inv_ref[...]

    @pl.when(jnp.logical_not(valid))
    def _():
        o_ref[...] = jnp.zeros_like(o_ref)


def moe_down(act, w2, w_scale, tile_expert, n_valid):
    m, f = act.shape
    d = w2.shape[2]
    tm = MOE_ROW_TILE
    tn = _tile(d, 1024)
    nj = d // tn

    def row(i, nv):
        return jnp.minimum(i, nv[0] - 1)

    def col(i, j, nv):
        return jnp.where(i < nv[0], j, nj - 1)

    grid_spec = pltpu.PrefetchScalarGridSpec(
        num_scalar_prefetch=2,
        grid=(m // tm, nj),
        in_specs=[pl.BlockSpec((tm, f), lambda i, j, texp, nv: (row(i, nv), 0)),
                  pl.BlockSpec((1, f, tn), lambda i, j, texp, nv: (texp[row(i, nv)], 0, col(i, j, nv))),
                  pl.BlockSpec((1, 1), lambda i, j, texp, nv: (0, 0))],
        out_specs=pl.BlockSpec((tm, tn), lambda i, j, texp, nv: (i, j)),
        scratch_shapes=[pltpu.VMEM((tm, f), FP8), pltpu.VMEM((tm, 1), F32)],
    )
    return pl.pallas_call(
        _moe_down_kernel,
        grid_spec=grid_spec,
        out_shape=jax.ShapeDtypeStruct((m, d), F32),
        compiler_params=_cparams("arbitrary", "arbitrary"),
        name="moe_down",
    )(tile_expert, n_valid, act, w2, w_scale)


def _moe_combine_kernel(pos_ref, y_hbm, x_ref, route_ref, mod_ref, fg_ref, o_ref, buf, sem, *,
                        n_tokens, n_experts, gate_row, final_norm):
    i = pl.program_id(0)
    tm = x_ref.shape[0]
    slot = i % 2

    def row_copy(tile, r, choice, s):
        p = pos_ref[choice * n_tokens + tile * tm + r]
        return pltpu.make_async_copy(y_hbm.at[pl.ds(p, 1), :], buf.at[s, choice, pl.ds(r, 1), :], sem.at[s])

    def start_gather(tile, s):
        def body(r, carry):
            row_copy(tile, r, 0, s).start()
            row_copy(tile, r, 1, s).start()
            return carry
        lax.fori_loop(0, tm, body, 0, unroll=8)

    def wait_gather(tile, s):
        def body(r, carry):
            row_copy(tile, r, 0, s).wait()
            row_copy(tile, r, 1, s).wait()
            return carry
        lax.fori_loop(0, tm, body, 0, unroll=8)

    @pl.when(i == 0)
    def _():
        start_gather(0, 0)

    @pl.when(i + 1 < pl.num_programs(0))
    def _():
        start_gather(i + 1, 1 - slot)

    wait_gather(i, slot)
    rec = route_ref[...]
    lane = lax.broadcasted_iota(jnp.int32, rec.shape, 1)
    w1 = jnp.sum(jnp.where(lane == n_experts, rec, 0.0), axis=1, keepdims=True)
    w2 = jnp.sum(jnp.where(lane == n_experts + 1, rec, 0.0), axis=1, keepdims=True)
    y = w1 * buf[slot, 0] + w2 * buf[slot, 1]
    out = x_ref[...] + mod_ref[0, gate_row:gate_row + 1, :] * y
    if final_norm:
        out = out * lax.rsqrt(jnp.mean(out * out, axis=-1, keepdims=True) + EPS) * fg_ref[...]
    o_ref[...] = out


def moe_combine(y, pos, x, route, mod, seq, n_experts, gate_row, final_gain=None):
    t, d = x.shape
    tm = _tile(seq, 256)
    per_seq = seq // tm
    final_norm = final_gain is not None
    fg = (final_gain if final_norm else jnp.ones((d,), F32)).reshape(1, d)
    grid_spec = pltpu.PrefetchScalarGridSpec(
        num_scalar_prefetch=1,
        grid=(t // tm,),
        in_specs=[pl.BlockSpec(memory_space=pl.ANY),
                  pl.BlockSpec((tm, d), lambda i, pos: (i, 0)),
                  pl.BlockSpec((tm, LANES), lambda i, pos: (i, 0)),
                  pl.BlockSpec((1, 6, d), lambda i, pos: (i // per_seq, 0, 0)),
                  pl.BlockSpec((1, d), lambda i, pos: (0, 0))],
        out_specs=pl.BlockSpec((tm, d), lambda i, pos: (i, 0)),
        scratch_shapes=[pltpu.VMEM((2, TOP_K, tm, d), F32), pltpu.SemaphoreType.DMA((2,))],
    )
    return pl.pallas_call(
        functools.partial(_moe_combine_kernel, n_tokens=t, n_experts=n_experts, gate_row=gate_row,
                          final_norm=final_norm),
        grid_spec=grid_spec,
        out_shape=jax.ShapeDtypeStruct((t, d), F32),
        compiler_params=_cparams("arbitrary"),
        name="moe_combine",
    )(pos, y, x, route, mod, fg)


def moe_ffn(x, gain, mod, seq, router_w, router_b, w1, w3, w2, gate_row, final_gain=None):
    t, d = x.shape
    n_experts = router_w.shape[1]
    tg = MOE_ROW_TILE
    w_r = _pad_cols(router_w, LANES)
    b_r = _pad_cols(router_b.reshape(1, n_experts), LANES)
    h, route = norm_router(x, gain, mod, seq, w_r, b_r, n_experts)
    cum = row_cumsum(route, 1, t)
    chosen = route[:, n_experts + 2:n_experts + 4].astype(jnp.int32)
    rank = jnp.take_along_axis(cum[:, :n_experts] - route[:, :n_experts], chosen, axis=1).astype(jnp.int32)
    counts = cum[t - 1, :n_experts].astype(jnp.int32)
    padded = (counts + tg - 1) // tg * tg
    ends = jnp.cumsum(padded)
    pos = (jnp.take(ends - padded, chosen) + rank).T.reshape(-1)
    n_tiles = (TOP_K * t + n_experts * (tg - 1)) // tg
    n_valid = (ends[n_experts - 1] // tg).reshape(1)
    tile_expert = jnp.minimum(jnp.searchsorted(ends, jnp.arange(n_tiles, dtype=jnp.int32) * tg, side="right"),
                              n_experts - 1).astype(jnp.int32)
    src = invert_positions(pos, n_tiles * tg, t)
    w1_8, s1 = fp8_weight(w1)
    w3_8, s3 = fp8_weight(w3)
    act = moe_up(h, w1_8, w3_8, jnp.stack([s1, s3]).reshape(1, 2), src, tile_expert, n_valid)
    w2_8, s2 = fp8_weight(w2)
    y = moe_down(act, w2_8, s2.reshape(1, 1), tile_expert, n_valid)
    return moe_combine(y, pos, x, route, mod, seq, n_experts, gate_row, final_gain)


def _merge_kernel(ya_ref, yb_ref, yc_ref, wa_ref, wb_ref, wc_ref, ga_ref, gb_ref, gc_ref, o_ref):
    out = ga_ref[...].astype(F32) * jnp.dot(ya_ref[...], wa_ref[0], preferred_element_type=F32)
    out += gb_ref[...].astype(F32) * jnp.dot(yb_ref[...], wb_ref[0], preferred_element_type=F32)
    out += gc_ref[...].astype(F32) * jnp.dot(yc_ref[...], wc_ref[0], preferred_element_type=F32)
    o_ref[...] = out.astype(o_ref.dtype)


def merge_branches(y_a, y_b, y_c, w_branch, gates, seq):
    m, k = y_a.shape
    n = w_branch.shape[2]
    tm = _tile(seq, 1024)
    tn = _tile(n, 1024)
    nj = n // tn
    y_spec = pl.BlockSpec((tm, k), lambda i, j: (i, 0))

    def w_spec(b):
        return pl.BlockSpec((1, k, tn), lambda i, j: (b, 0, j))

    def g_spec(b):
        return pl.BlockSpec((tm, tn), lambda i, j: (i, b * nj + j))

    return pl.pallas_call(
        _merge_kernel,
        grid=(m // tm, nj),
        in_specs=[y_spec, y_spec, y_spec, w_spec(0), w_spec(1), w_spec(2), g_spec(0), g_spec(1), g_spec(2)],
        out_specs=pl.BlockSpec((tm, tn), lambda i, j: (i, j)),
        out_shape=jax.ShapeDtypeStruct((m, n), BF16),
        compiler_params=_cparams("parallel", "parallel"),
        name="merge_branches",
    )(y_a, y_b, y_c, w_branch, w_branch, w_branch, gates, gates, gates)


def _conv_kernel(h_ref, c_ref, b_ref, hp_ref, cp_ref, w_ref, bias_ref, o_ref, *, blocks_per_seq):
    u = c_ref[...].astype(F32) * h_ref[...].astype(F32)
    tm = u.shape[0]
    halo = hp_ref.shape[0]
    up = cp_ref[...].astype(F32) * hp_ref[...].astype(F32)
    first = (pl.program_id(0) % blocks_per_seq) == 0
    up = jnp.where(first, 0.0, up)
    p1 = up[halo - 1:halo, :]
    p2 = up[halo - 2:halo - 1, :]
    row = lax.broadcasted_iota(jnp.int32, (tm, 1), 0)
    u1 = jnp.where(row == 0, p1, pltpu.roll(u, 1, 0))
    u2 = jnp.where(row == 0, p2, jnp.where(row == 1, p1, pltpu.roll(u, 2, 0)))
    y = w_ref[0:1, :] * u2 + w_ref[1:2, :] * u1 + w_ref[2:3, :] * u + bias_ref[...]
    o_ref[...] = (b_ref[...].astype(F32) * y).astype(o_ref.dtype)


def short_conv(p, col0, conv_w, conv_b, seq):
    t = p.shape[0]
    w = conv_w.shape[1]
    tm = _tile(seq, 512)
    halo = 16
    ratio = tm // halo
    prev = lambda i: jnp.maximum(i * ratio - 1, 0)
    c0 = col0 // w
    return pl.pallas_call(
        functools.partial(_conv_kernel, blocks_per_seq=seq // tm),
        grid=(t // tm,),
        in_specs=[
            pl.BlockSpec((tm, w), lambda i: (i, c0)),
            pl.BlockSpec((tm, w), lambda i: (i, c0 + 1)),
            pl.BlockSpec((tm, w), lambda i: (i, c0 + 2)),
            pl.BlockSpec((halo, w), lambda i: (prev(i), c0)),
            pl.BlockSpec((halo, w), lambda i: (prev(i), c0 + 1)),
            pl.BlockSpec((3, w), lambda i: (0, 0)),
            pl.BlockSpec((1, w), lambda i: (0, 0)),
        ],
        out_specs=pl.BlockSpec((tm, w), lambda i: (i, 0)),
        out_shape=jax.ShapeDtypeStruct((t, w), BF16),
        compiler_params=_cparams("parallel"),
        name="short_conv",
    )(p, p, p, p, p, conv_w, conv_b.reshape(1, w))


def _block_cumsum(x_ref, cum_ref, carry_ref):
    @pl.when(pl.program_id(1) == 0)
    def _():
        carry_ref[...] = jnp.zeros_like(carry_ref)

    x = x_ref[...]
    n = x.shape[0]
    r = lax.broadcasted_iota(jnp.int32, (n, n), 0)
    c = lax.broadcasted_iota(jnp.int32, (n, n), 1)
    tri = (r >= c).astype(BF16)
    x1 = x.astype(BF16)
    r1 = x - x1.astype(F32)
    x2 = r1.astype(BF16)
    x3 = (r1 - x2.astype(F32)).astype(BF16)
    cum = (jnp.dot(tri, x1, preferred_element_type=F32) + jnp.dot(tri, x2, preferred_element_type=F32)
           + jnp.dot(tri, x3, preferred_element_type=F32)) + carry_ref[...]
    cum_ref[...] = cum
    carry_ref[...] = cum[n - 1:n, :]
    return cum


def _cumsum_kernel(x_ref, cum_ref, carry_ref):
    _block_cumsum(x_ref, cum_ref, carry_ref)


def row_cumsum(x, n_seq, seq):
    blk = _tile(seq, 256)
    nb = seq // blk
    return pl.pallas_call(
        _cumsum_kernel,
        grid=(n_seq, nb),
        in_specs=[pl.BlockSpec((blk, LANES), lambda b, i: (b * nb + i, 0))],
        out_specs=pl.BlockSpec((blk, LANES), lambda b, i: (b * nb + i, 0)),
        out_shape=jax.ShapeDtypeStruct((n_seq * seq, LANES), F32),
        scratch_shapes=[pltpu.VMEM((1, LANES), F32)],
        compiler_params=_cparams("parallel", "arbitrary"),
        name="row_cumsum",
    )(x)


FOX_GROUP = 2
LOG2_E = 1.4426950408889634
FOX_SUM_ROWS = 16


def _fox_bias_columns(cum_rows, head, value_lane, ones_lane, ones):
    lane = lax.broadcasted_iota(jnp.int32, cum_rows.shape, 1)
    c = jnp.sum(jnp.where(lane == head, cum_rows, 0.0), axis=1, keepdims=True) * LOG2_E
    hi = c.astype(BF16).astype(F32)
    r1 = c - hi
    mid = r1.astype(BF16).astype(F32)
    lo = r1 - mid
    out = jnp.where((lane >= ones_lane) & (lane < ones_lane + 3), ones, 0.0)
    out = jnp.where(lane == value_lane, hi, out)
    out = jnp.where(lane == value_lane + 1, mid, out)
    out = jnp.where(lane == value_lane + 2, lo, out)
    return out.astype(BF16)


def _fox_kernel(q_ref, k_ref, v_ref, cq_ref, ck_ref, o_ref, kaug_ref, vt_ref, qaug_ref, s_ref, m_ref, acc_ref):
    hp = pl.program_id(1)
    i = pl.program_id(2)
    tq = q_ref.shape[0]
    tk = tq
    seq = k_ref.shape[0]

    @pl.when(i == 0)
    def _():
        ones_row = lax.broadcasted_iota(jnp.int32, (FOX_SUM_ROWS, tk), 0) == 0

        def fill(c, carry):
            rows = pl.ds(pl.multiple_of(c * tk, tk), tk)
            for g in range(FOX_GROUP):
                cols = slice(g * HEAD_DIM, (g + 1) * HEAD_DIM)
                kaug_ref[g, rows, 0:HEAD_DIM] = k_ref[rows, cols]
                kaug_ref[g, rows, HEAD_DIM:2 * HEAD_DIM] = _fox_bias_columns(
                    ck_ref[rows, :], hp * FOX_GROUP + g, 0, 3, 1.0)
                vt_ref[g, 0:HEAD_DIM, rows] = v_ref[rows, cols].astype(F32).T.astype(BF16)
                vt_ref[g, HEAD_DIM:, rows] = jnp.where(ones_row, 1.0, 0.0).astype(BF16)
            return carry

        lax.fori_loop(0, seq // tk, fill, 0)

    for g in range(FOX_GROUP):
        cols = slice(g * HEAD_DIM, (g + 1) * HEAD_DIM)
        qaug_ref[g, :, 0:HEAD_DIM] = q_ref[:, cols]
        qaug_ref[g, :, HEAD_DIM:2 * HEAD_DIM] = _fox_bias_columns(cq_ref[...], hp * FOX_GROUP + g, 3, 0, -1.0)
    m_ref[...] = jnp.full_like(m_ref, NEG_INF)
    acc_ref[...] = jnp.zeros_like(acc_ref)
    key = lax.broadcasted_iota(jnp.int32, (tk, tq), 0)
    qry = lax.broadcasted_iota(jnp.int32, (tk, tq), 1)

    def score(j, slot):
        start = pl.multiple_of(j * tk, tk)
        for g in range(FOX_GROUP):
            s_ref[slot, g] = lax.dot_general(kaug_ref[g, pl.ds(start, tk), :], qaug_ref[g],
                                             (((1,), (1,)), ((), ())), preferred_element_type=F32)

    def consume(j, slot, on_diagonal):
        start = pl.multiple_of(j * tk, tk)
        for g in range(FOX_GROUP):
            st = s_ref[slot, g]
            if on_diagonal:
                st = jnp.where(key <= qry, st, NEG_INF)
            m_old = m_ref[g]
            m_new = jnp.maximum(m_old, jnp.max(st, axis=0, keepdims=True))
            alpha = jnp.exp2(m_old - m_new)
            pt = jnp.exp2(st - m_new)
            acc_ref[g] = alpha * acc_ref[g] + jnp.dot(vt_ref[g, :, pl.ds(start, tk)], pt.astype(BF16),
                                                      preferred_element_type=F32)
            m_ref[g] = m_new

    def two_below_diagonal(p, carry):
        j = 2 * p
        score(j + 1, 1)
        consume(j, 0, False)
        score(j + 2, 0)
        consume(j + 1, 1, False)
        return carry

    score(0, 0)
    lax.fori_loop(0, i // 2, two_below_diagonal, 0)

    @pl.when(i % 2 == 0)
    def _():
        consume(i, 0, True)

    @pl.when(i % 2 == 1)
    def _():
        score(i, 1)
        consume(i - 1, 0, False)
        consume(i, 1, True)

    for g in range(FOX_GROUP):
        out = acc_ref[g, 0:HEAD_DIM, :] / acc_ref[g, HEAD_DIM:HEAD_DIM + 1, :]
        o_ref[:, g * HEAD_DIM:(g + 1) * HEAD_DIM] = out.T.astype(o_ref.dtype)


def fox_attention(p, col0, cum, batch, seq):
    tq = _tile(seq, 512)
    nq = seq // tq
    gw = FOX_GROUP * HEAD_DIM
    n_groups = N_HEADS // FOX_GROUP
    q0 = col0 // gw
    return pl.pallas_call(
        _fox_kernel,
        grid=(batch, n_groups, nq),
        in_specs=[
            pl.BlockSpec((tq, gw), lambda b, h, i: (b * nq + i, q0 + h)),
            pl.BlockSpec((seq, gw), lambda b, h, i: (b, q0 + n_groups + h)),
            pl.BlockSpec((seq, gw), lambda b, h, i: (b, q0 + 2 * n_groups + h)),
            pl.BlockSpec((tq, LANES), lambda b, h, i: (b * nq + i, 0)),
            pl.BlockSpec((seq, LANES), lambda b, h, i: (b, 0)),
        ],
        out_specs=pl.BlockSpec((tq, gw), lambda b, h, i: (b * nq + i, h)),
        out_shape=jax.ShapeDtypeStruct((batch * seq, WIDTH), BF16),
        scratch_shapes=[pltpu.VMEM((FOX_GROUP, seq, 2 * HEAD_DIM), BF16),
                        pltpu.VMEM((FOX_GROUP, HEAD_DIM + FOX_SUM_ROWS, seq), BF16),
                        pltpu.VMEM((FOX_GROUP, tq, 2 * HEAD_DIM), BF16),
                        pltpu.VMEM((2, FOX_GROUP, tq, tq), F32),
                        pltpu.VMEM((FOX_GROUP, 1, tq), F32),
                        pltpu.VMEM((FOX_GROUP, HEAD_DIM + FOX_SUM_ROWS, tq), F32)],
        compiler_params=_cparams("parallel", "parallel", "arbitrary"),
        name="fox_attention",
    )(p, p, p, cum, cum)


def _hgrn_levels():
    sizes = []
    half = HGRN_CHUNK // 2
    while half >= HGRN_DIAG:
        sizes.append(half)
        half //= 2
    return sizes


def _hgrn_coefficients():
    c = HGRN_CHUNK
    t = np.arange(c)[:, None]
    u = np.arange(c)[None, :]
    slabs = [(u <= t), (u > t)]
    for size in _hgrn_levels():
        ref = (t // (2 * size)) * (2 * size) + size - 1
        upper = (t % (2 * size)) >= size
        slabs.append(np.where(upper, (u > ref) & (u <= t), (u > t) & (u <= ref)))
    slabs.append((u <= t) & (u // HGRN_DIAG == t // HGRN_DIAG))
    coef = np.concatenate(slabs, axis=0).astype(np.float32)
    return np.concatenate([coef, coef], axis=1)


def _hgrn_level_masks():
    c = HGRN_CHUNK
    t = np.arange(c)[:, None]
    s = np.arange(c)[None, :]
    masks = []
    for size in _hgrn_levels():
        same = (t // (2 * size)) == (s // (2 * size))
        masks.append(same & ((t % (2 * size)) >= size) & ((s % (2 * size)) < size))
    masks.append((t // HGRN_DIAG == s // HGRN_DIAG) & (s <= t))
    return np.stack(masks).astype(np.float32)


def _hgrn_placement():
    place = np.zeros((HGRN_DIAG, HEAD_DIM, HGRN_CHUNK), np.float32)
    for j in range(HGRN_DIAG):
        place[j, :, j::HGRN_DIAG] = 1.0
    return place.reshape(HGRN_DIAG * HEAD_DIM, HGRN_CHUNK)


def _hgrn_kernel(q_ref, f_ref, i_ref, g_ref, lb_ref, norm_ref, coef_ref, mask_ref, place_ref, o_ref,
                 st_ref, e_ref, kf_ref, *, layer):
    c = HGRN_CHUNK
    blk_rows = HGRN_DIAG
    n_lev = len(_hgrn_levels())
    b_in_rows = (2 + n_lev) * c

    @pl.when(pl.program_id(1) == 0)
    def _():
        st_ref[...] = jnp.zeros_like(st_ref)

    lbr = lb_ref[...]
    le = jnp.exp(lbr - jnp.max(lbr, axis=0, keepdims=True))
    lp = le / jnp.sum(le, axis=0, keepdims=True)
    lb = jnp.zeros((1, WIDTH), F32)
    for r in range(1, layer + 1):
        lb = lb + lp[r:r + 1, :]

    f = lb + (1.0 - lb) * _sigmoid(f_ref[...].astype(F32))
    g = jnp.log(f)
    kf_ref[...] = 1.0 - f
    g_hi, g_lo = _split_bf16(g)
    e_ref[...] = jnp.dot(coef_ref[...], jnp.concatenate([g_hi, g_lo], axis=0), preferred_element_type=F32)

    def head(h, carry):
        lanes = pl.ds(pl.multiple_of(h * HEAD_DIM, HEAD_DIM), HEAD_DIM)
        q = q_ref[:, lanes].astype(F32)
        k = kf_ref[:, lanes]
        v = i_ref[:, lanes]
        st = st_ref[h]
        qe = (q * jnp.exp(e_ref[0:c, lanes])).astype(BF16)
        out = lax.dot_general(qe, st.astype(BF16), (((1,), (1,)), ((), ())), preferred_element_type=F32)
        attn = jnp.zeros((c, c), F32)
        for lev in range(n_lev):
            pw = jnp.exp(e_ref[(2 + lev) * c:(3 + lev) * c, lanes])
            a = lax.dot_general((q * pw).astype(BF16), (k * pw).astype(BF16), (((1,), (1,)), ((), ())),
                                preferred_element_type=F32)
            attn = attn + jnp.where(mask_ref[lev] > 0.0, a, 0.0)
        b_in = e_ref[b_in_rows:b_in_rows + c, lanes]

        def block_row(ref, base, j):
            rows = [jnp.broadcast_to(ref[pl.ds(base + blk * blk_rows + j, 1), lanes], (blk_rows, HEAD_DIM))
                    for blk in range(c // blk_rows)]
            return jnp.concatenate(rows, axis=0)

        z = []
        for j in range(blk_rows):
            k_j = block_row(kf_ref, 0, j)
            b_j = block_row(e_ref, b_in_rows, j)
            z.append((q * k_j * jnp.exp(jnp.minimum(b_in - b_j, 0.0))).astype(BF16))
        diag = jnp.dot(jnp.concatenate(z, axis=1), place_ref[...], preferred_element_type=F32)
        attn = attn + jnp.where(mask_ref[n_lev] > 0.0, diag, 0.0)
        out = out + jnp.dot(attn.astype(BF16), v, preferred_element_type=F32)
        ke = (k * jnp.exp(e_ref[c:2 * c, lanes])).astype(BF16)
        decay = jnp.exp(e_ref[c - 1:c, lanes])
        st_ref[h] = st * decay + lax.dot_general(v, ke, (((0,), (0,)), ((), ())), preferred_element_type=F32)
        out = out * lax.rsqrt(jnp.mean(out * out, axis=-1, keepdims=True) + EPS)
        out = out * norm_ref[:, lanes] * _silu(g_ref[:, lanes].astype(F32))
        o_ref[:, lanes] = out.astype(o_ref.dtype)
        return carry

    lax.fori_loop(0, N_HEADS, head, 0, unroll=4)


def hgrn_mixer(p, col0, lower_bounds, norm, layer, batch, seq):
    c = HGRN_CHUNK
    nc = seq // c
    depth = lower_bounds.shape[0]
    coef = jnp.asarray(_hgrn_coefficients(), BF16)
    masks = jnp.asarray(_hgrn_level_masks(), F32)
    place = jnp.asarray(_hgrn_placement(), BF16)
    n_slab = coef.shape[0] // c

    def col_spec(j):
        return pl.BlockSpec((c, WIDTH), lambda b, n: (b * nc + n, col0 // WIDTH + j))

    return pl.pallas_call(
        functools.partial(_hgrn_kernel, layer=layer),
        grid=(batch, nc),
        in_specs=[col_spec(0), col_spec(1), col_spec(2), col_spec(3),
                  pl.BlockSpec((depth, WIDTH), lambda b, n: (0, 0)),
                  pl.BlockSpec((1, WIDTH), lambda b, n: (0, 0)),
                  pl.BlockSpec(coef.shape, lambda b, n: (0, 0)),
                  pl.BlockSpec(masks.shape, lambda b, n: (0, 0, 0)),
                  pl.BlockSpec(place.shape, lambda b, n: (0, 0))],
        out_specs=pl.BlockSpec((c, WIDTH), lambda b, n: (b * nc + n, 0)),
        out_shape=jax.ShapeDtypeStruct((batch * seq, WIDTH), BF16),
        scratch_shapes=[pltpu.VMEM((N_HEADS, HEAD_DIM, HEAD_DIM), F32),
                        pltpu.VMEM((n_slab * c, WIDTH), F32),
                        pltpu.VMEM((c, WIDTH), F32)],
        compiler_params=_cparams("parallel", "arbitrary"),
        name="hgrn_mixer",
    )(p, p, p, p, lower_bounds, norm.reshape(1, WIDTH), coef, masks, place)


def _pad_cols(w, n):
    return jnp.pad(w, ((0, 0), (0, n - w.shape[1])))


def _round_up(n, m):
    return ((n + m - 1) // m) * m


def kernel(x, c, norm_mix, norm_ffn, w_ada, b_ada, w_in, b_gate, fox_b_f, hgrn_lower_bounds, hgrn_norm,
           conv_w, conv_b, w_branch, w_o, ffn_w1, ffn_w3, ffn_w2, router_w, router_b,
           expert_w1, expert_w3, expert_w2, norm_final):
    batch, seq, d = x.shape
    depth = w_ada.shape[0]
    t = batch * seq

    o_fox = 0
    o_ff = 3 * WIDTH
    o_hgrn = o_ff + N_HEADS
    o_conv = o_hgrn + 4 * WIDTH
    o_gate = o_conv + 3 * WIDTH

    mod_all = ada_modulation(c, w_ada, b_ada)
    xt = x.reshape(t, d)
    for layer in range(depth):
        mod = mod_all[layer]
        wl = w_in[layer]
        w_mix = jnp.concatenate([wl[:, o_fox:o_ff], wl[:, o_hgrn:o_gate]], axis=1).astype(BF16)
        c_hgrn = 3 * WIDTH
        c_conv = c_hgrn + 4 * WIDTH
        col_scale = jnp.ones((w_mix.shape[1],), F32).at[:WIDTH].set(HEAD_DIM ** -0.5 * LOG2_E)
        w_ff = _pad_cols(wl[:, o_ff:o_hgrn], LANES).astype(BF16)
        w_gate = wl[:, o_gate:].astype(BF16)
        b_ff = _pad_cols(fox_b_f[layer].reshape(1, N_HEADS), LANES)

        h, lsf = norm_forget(xt, norm_mix[layer], mod, seq, w_ff, b_ff)
        p = project(h, w_mix, col_scale, seq)
        gates = gate_project(h, w_gate, b_gate[layer], seq)

        y_a = hgrn_mixer(p, c_hgrn, hgrn_lower_bounds, hgrn_norm[layer], layer, batch, seq)
        y_b = short_conv(p, c_conv, conv_w[layer], conv_b[layer], seq)
        cum = row_cumsum(lsf, batch, seq)
        y_c = fox_attention(p, 0, cum, batch, seq)

        merged = merge_branches(y_a, y_b, y_c, w_branch[layer].astype(BF16), gates, seq)
        xt = residual_project(merged, w_o[layer].astype(BF16), xt, mod, seq, 2, d // 2)

        i = layer // 2
        if layer % 2 == 0:
            dff = ffn_w1.shape[2]
            dff_pad = _round_up(dff, 1024)
            w1 = _pad_cols(ffn_w1[i], dff_pad).astype(BF16)
            w3 = _pad_cols(ffn_w3[i], dff_pad).astype(BF16)
            w2 = jnp.pad(ffn_w2[i], ((0, dff_pad - dff), (0, 0))).astype(BF16)
            h2 = norm_only(xt, norm_ffn[layer], mod, seq)
            act = glu(h2, w1, w3, seq)
            xt = residual_project(act, w2, xt, mod, seq, 5, dff_pad // 4)
        else:
            last = layer == depth - 1
            xt = moe_ffn(xt, norm_ffn[layer], mod, seq, router_w[i], router_b[i],
                         expert_w1[i], expert_w3[i], expert_w2[i], 5,
                         final_gain=norm_final if last else None)
    if depth % 2:
        xt = final_norm(xt, norm_final)
    return xt.reshape(batch, seq, d)
```

```python
import functools

import jax
import jax.numpy as jnp
import numpy as np
from jax import lax
from jax.experimental import pallas as pl
from jax.experimental.pallas import tpu as pltpu

F32 = jnp.float32
BF16 = jnp.bfloat16
FP8 = jnp.float8_e4m3fn
FP8_TARGET = 256.0

N_HEADS = 8
HEAD_DIM = 128
WIDTH = N_HEADS * HEAD_DIM
TOP_K = 2
EPS = 1e-6
NEG_INF = -1e30
LANES = 128
HGRN_CHUNK = 128
HGRN_DIAG = 8
NORM_ROWS = 512
VMEM_LIMIT = 56 * 1024 * 1024


def _cparams(*sem):
    return pltpu.CompilerParams(dimension_semantics=sem, vmem_limit_bytes=VMEM_LIMIT)


def _tile(n, pref):
    t = min(n, pref)
    while n % t:
        t //= 2
    return t


def _sigmoid(z):
    return 1.0 / (1.0 + jnp.exp(-z))


def _silu(z):
    return z * _sigmoid(z)


def _ada_kernel(ct_ref, w_ref, b_ref, o_ref, act_ref, *, batch):
    k = pl.program_id(2)
    tk = w_ref.shape[1]
    tn = o_ref.shape[2]

    @pl.when((pl.program_id(0) == 0) & (pl.program_id(1) == 0) & (k == 0))
    def _():
        act = _silu(ct_ref[...])
        for b in range(batch):
            act_ref[b] = jnp.broadcast_to(act[:, b:b + 1], act_ref.shape[1:])

    @pl.when(k == 0)
    def _():
        row = lax.broadcasted_iota(jnp.int32, o_ref.shape[1:], 0)
        o_ref[0] = jnp.where(row < batch, b_ref[0], 0.0)

    rows = pl.ds(pl.multiple_of(k * tk, tk), tk)
    for b in range(batch):
        a = act_ref[b, rows, :]
        for jb in range(tn // LANES):
            cols = slice(jb * LANES, (jb + 1) * LANES)
            o_ref[0, b:b + 1, cols] += jnp.sum(w_ref[0, :, cols] * a, axis=0, keepdims=True)


def ada_modulation(c, w_ada, b_ada):
    depth, d, n = w_ada.shape
    b = c.shape[0]
    rows = _round_up(b, 8)
    tn = _tile(n, 2048)
    tk = _tile(d, 1024)
    out = pl.pallas_call(
        functools.partial(_ada_kernel, batch=b),
        grid=(depth, n // tn, d // tk),
        in_specs=[
            pl.BlockSpec((d, b), lambda l, j, k: (0, 0)),
            pl.BlockSpec((1, tk, tn), lambda l, j, k: (l, k, j)),
            pl.BlockSpec((1, 1, tn), lambda l, j, k: (l, 0, j)),
        ],
        out_specs=pl.BlockSpec((1, rows, tn), lambda l, j, k: (l, 0, j)),
        out_shape=jax.ShapeDtypeStruct((depth, rows, n), F32),
        scratch_shapes=[pltpu.VMEM((b, d, LANES), F32)],
        compiler_params=_cparams("arbitrary", "arbitrary", "arbitrary"),
        name="ada_modulation",
    )(c.T, w_ada, b_ada.reshape(depth, 1, n))
    return out[:, :b].reshape(depth, b, 6, d)


def _norm_mod(x_ref, g_ref, mod_ref, shift_row, scale_row):
    x = x_ref[...]
    y = x * lax.rsqrt(jnp.mean(x * x, axis=-1, keepdims=True) + EPS) * g_ref[...]
    return y * (1.0 + mod_ref[0, scale_row:scale_row + 1, :]) + mod_ref[0, shift_row:shift_row + 1, :]


def _log_sigmoid(z):
    return jnp.minimum(z, 0.0) - jnp.log(1.0 + jnp.exp(-jnp.abs(z)))


def _norm_forget_kernel(x_ref, g_ref, mod_ref, wf_ref, bf_ref, h_ref, lsf_ref, *, shift_row, scale_row):
    h = _norm_mod(x_ref, g_ref, mod_ref, shift_row, scale_row).astype(BF16)
    h_ref[...] = h
    z = jnp.dot(h, wf_ref[...], preferred_element_type=F32) + bf_ref[...]
    lsf_ref[...] = _log_sigmoid(z)


def _split_bf16(v):
    hi = v.astype(BF16)
    lo = (v - hi.astype(F32)).astype(BF16)
    return hi, lo


def _norm_router_kernel(x_ref, g_ref, mod_ref, wr_ref, br_ref, h_ref, route_ref, *,
                        shift_row, scale_row, n_experts):
    h = _norm_mod(x_ref, g_ref, mod_ref, shift_row, scale_row)
    h_ref[...] = h
    h_hi, h_lo = _split_bf16(h)
    w_hi, w_lo = _split_bf16(wr_ref[...])
    logits = (jnp.dot(h_hi, w_hi, preferred_element_type=F32)
              + jnp.dot(h_hi, w_lo, preferred_element_type=F32)
              + jnp.dot(h_lo, w_hi, preferred_element_type=F32)) + br_ref[...]
    lane = lax.broadcasted_iota(jnp.int32, logits.shape, 1)
    lg = jnp.where(lane < n_experts, logits, -jnp.inf)
    m1 = jnp.max(lg, axis=1, keepdims=True)
    i1 = jnp.min(jnp.where(lg == m1, lane, LANES), axis=1, keepdims=True)
    lg2 = jnp.where(lane == i1, -jnp.inf, lg)
    m2 = jnp.max(lg2, axis=1, keepdims=True)
    i2 = jnp.min(jnp.where(lg2 == m2, lane, LANES), axis=1, keepdims=True)
    e = jnp.exp(m2 - m1)
    w1 = 1.0 / (1.0 + e)
    w2 = e / (1.0 + e)
    rec = jnp.where((lane == i1) | (lane == i2), 1.0, 0.0)
    rec = jnp.where(lane == n_experts, w1, rec)
    rec = jnp.where(lane == n_experts + 1, w2, rec)
    rec = jnp.where(lane == n_experts + 2, i1.astype(F32), rec)
    rec = jnp.where(lane == n_experts + 3, i2.astype(F32), rec)
    route_ref[...] = rec


def _norm_only_kernel(x_ref, g_ref, mod_ref, h_ref, *, shift_row, scale_row):
    h_ref[...] = _norm_mod(x_ref, g_ref, mod_ref, shift_row, scale_row).astype(BF16)


def _norm_call(body, x, gain, mod, seq, extra_in, extra_specs, extra_out, extra_out_specs, tm, h_dtype=BF16):
    t, d = x.shape
    per_seq = seq // tm
    in_specs = [
        pl.BlockSpec((tm, d), lambda i: (i, 0)),
        pl.BlockSpec((1, d), lambda i: (0, 0)),
        pl.BlockSpec((1, 6, d), lambda i: (i // per_seq, 0, 0)),
    ] + extra_specs
    out_shape = [jax.ShapeDtypeStruct((t, d), h_dtype)] + extra_out
    out_specs = [pl.BlockSpec((tm, d), lambda i: (i, 0))] + extra_out_specs
    return pl.pallas_call(
        body,
        grid=(t // tm,),
        in_specs=in_specs,
        out_specs=out_specs,
        out_shape=out_shape,
        compiler_params=_cparams("parallel"),
        name="norm_mod",
    )(x, gain.reshape(1, d), mod, *extra_in)


def norm_forget(x, gain, mod, seq, w_f, b_f):
    t, d = x.shape
    tm = _tile(seq, NORM_ROWS)
    body = functools.partial(_norm_forget_kernel, shift_row=0, scale_row=1)
    return _norm_call(
        body, x, gain, mod, seq, [w_f, b_f],
        [pl.BlockSpec((d, LANES), lambda i: (0, 0)), pl.BlockSpec((1, LANES), lambda i: (0, 0))],
        [jax.ShapeDtypeStruct((t, LANES), F32)], [pl.BlockSpec((tm, LANES), lambda i: (i, 0))], tm)


def norm_router(x, gain, mod, seq, w_r, b_r, n_experts):
    t, d = x.shape
    assert n_experts + 4 <= LANES
    tm = _tile(seq, NORM_ROWS)
    body = functools.partial(_norm_router_kernel, shift_row=3, scale_row=4, n_experts=n_experts)
    return _norm_call(
        body, x, gain, mod, seq, [w_r, b_r],
        [pl.BlockSpec((d, LANES), lambda i: (0, 0)), pl.BlockSpec((1, LANES), lambda i: (0, 0))],
        [jax.ShapeDtypeStruct((t, LANES), F32)], [pl.BlockSpec((tm, LANES), lambda i: (i, 0))], tm,
        h_dtype=F32)


def norm_only(x, gain, mod, seq):
    tm = _tile(seq, NORM_ROWS)
    body = functools.partial(_norm_only_kernel, shift_row=3, scale_row=4)
    return _norm_call(body, x, gain, mod, seq, [], [], [], [], tm)[0]


def _final_norm_kernel(x_ref, g_ref, o_ref):
    x = x_ref[...]
    o_ref[...] = x * lax.rsqrt(jnp.mean(x * x, axis=-1, keepdims=True) + EPS) * g_ref[...]


def final_norm(x, gain):
    t, d = x.shape
    tm = _tile(t, NORM_ROWS)
    return pl.pallas_call(
        _final_norm_kernel,
        grid=(t // tm,),
        in_specs=[pl.BlockSpec((tm, d), lambda i: (i, 0)), pl.BlockSpec((1, d), lambda i: (0, 0))],
        out_specs=pl.BlockSpec((tm, d), lambda i: (i, 0)),
        out_shape=jax.ShapeDtypeStruct((t, d), F32),
        compiler_params=_cparams("parallel"),
        name="final_norm",
    )(x, gain.reshape(1, d))


def _proj_kernel(a_ref, w_ref, s_ref, o_ref):
    acc = jnp.dot(a_ref[...], w_ref[...], preferred_element_type=F32)
    o_ref[...] = (acc * s_ref[...]).astype(o_ref.dtype)


def _pow2_scale(amax):
    return jnp.exp2(jnp.floor(jnp.log2(FP8_TARGET / jnp.maximum(amax, 1e-30))))


def fp8_weight(w):
    scale = _pow2_scale(jnp.max(jnp.abs(w)).astype(F32))
    return (w.astype(F32) * scale).astype(FP8), scale


def _gate_proj_kernel(a_ref, w_ref, b_ref, ws_ref, o_ref, a8_ref, inv_ref):
    @pl.when(pl.program_id(1) == 0)
    def _():
        a = a_ref[...].astype(F32)
        s = _pow2_scale(jnp.max(jnp.abs(a), axis=1, keepdims=True))
        a8_ref[...] = (a * s).astype(FP8)
        inv_ref[...] = 1.0 / (s * ws_ref[...])

    acc = jnp.dot(a8_ref[...], w_ref[...], preferred_element_type=F32)
    o_ref[...] = _sigmoid(acc * inv_ref[...] + b_ref[...]).astype(o_ref.dtype)


def _mm_tiles(m, n, seq):
    return _tile(seq, 1024), _tile(n, 1024)


def project(a, w, col_scale, seq):
    m, k = a.shape
    n = w.shape[1]
    tm, tn = _mm_tiles(m, n, seq)
    return pl.pallas_call(
        _proj_kernel,
        grid=(m // tm, n // tn),
        in_specs=[pl.BlockSpec((tm, k), lambda i, j: (i, 0)), pl.BlockSpec((k, tn), lambda i, j: (0, j)),
                  pl.BlockSpec((1, tn), lambda i, j: (0, j))],
        out_specs=pl.BlockSpec((tm, tn), lambda i, j: (i, j)),
        out_shape=jax.ShapeDtypeStruct((m, n), BF16),
        compiler_params=_cparams("parallel", "parallel"),
        name="project",
    )(a, w, col_scale.reshape(1, n))


def gate_project(a, w, bias, seq):
    m, k = a.shape
    n = w.shape[1]
    tm, tn = _mm_tiles(m, n, seq)
    w8, w_scale = fp8_weight(w)
    return pl.pallas_call(
        _gate_proj_kernel,
        grid=(m // tm, n // tn),
        in_specs=[pl.BlockSpec((tm, k), lambda i, j: (i, 0)), pl.BlockSpec((k, tn), lambda i, j: (0, j)),
                  pl.BlockSpec((1, tn), lambda i, j: (0, j)), pl.BlockSpec((1, 1), lambda i, j: (0, 0))],
        out_specs=pl.BlockSpec((tm, tn), lambda i, j: (i, j)),
        out_shape=jax.ShapeDtypeStruct((m, n), BF16),
        scratch_shapes=[pltpu.VMEM((tm, k), FP8), pltpu.VMEM((tm, 1), F32)],
        compiler_params=_cparams("parallel", "arbitrary"),
        name="gate_project",
    )(a, w8, bias.reshape(1, n), w_scale.reshape(1, 1))


def _residual_kernel(a_ref, w_ref, x_ref, mod_ref, o_ref, acc_ref, *, gate_row):
    k = pl.program_id(2)

    @pl.when(k == 0)
    def _():
        acc_ref[...] = jnp.zeros_like(acc_ref)

    acc_ref[...] += jnp.dot(a_ref[...], w_ref[...], preferred_element_type=F32)

    @pl.when(k == pl.num_programs(2) - 1)
    def _():
        o_ref[...] = x_ref[...] + mod_ref[0, gate_row:gate_row + 1, :] * acc_ref[...]


def residual_project(a, w, x, mod, seq, gate_row, tk_pref):
    m, kdim = a.shape
    n = w.shape[1]
    tm, tn = _mm_tiles(m, n, seq)
    tk = _tile(kdim, tk_pref)
    per_seq = seq // tm
    return pl.pallas_call(
        functools.partial(_residual_kernel, gate_row=gate_row),
        grid=(m // tm, n // tn, kdim // tk),
        in_specs=[
            pl.BlockSpec((tm, tk), lambda i, j, k: (i, k)),
            pl.BlockSpec((tk, tn), lambda i, j, k: (k, j)),
            pl.BlockSpec((tm, tn), lambda i, j, k: (i, j)),
            pl.BlockSpec((1, 6, tn), lambda i, j, k: (i // per_seq, 0, j)),
        ],
        out_specs=pl.BlockSpec((tm, tn), lambda i, j, k: (i, j)),
        out_shape=jax.ShapeDtypeStruct((m, n), F32),
        scratch_shapes=[pltpu.VMEM((tm, tn), F32)],
        compiler_params=_cparams("parallel", "parallel", "arbitrary"),
        name="residual_project",
    )(a, w, x, mod)


def _glu_kernel(h_ref, w1_ref, w3_ref, o_ref):
    h = h_ref[...]
    a = jnp.dot(h, w1_ref[...], preferred_element_type=F32)
    b = jnp.dot(h, w3_ref[...], preferred_element_type=F32)
    o_ref[...] = (_silu(a) * b).astype(o_ref.dtype)


def glu(h, w1, w3, seq):
    m, k = h.shape
    n = w1.shape[1]
    tm = _tile(seq, 1024)
    tn = _tile(n, 512)
    return pl.pallas_call(
        _glu_kernel,
        grid=(m // tm, n // tn),
        in_specs=[pl.BlockSpec((tm, k), lambda i, j: (i, 0)),
                  pl.BlockSpec((k, tn), lambda i, j: (0, j)),
                  pl.BlockSpec((k, tn), lambda i, j: (0, j))],
        out_specs=pl.BlockSpec((tm, tn), lambda i, j: (i, j)),
        out_shape=jax.ShapeDtypeStruct((m, n), BF16),
        compiler_params=_cparams("parallel", "parallel"),
        name="glu",
    )(h, w1, w3)


MOE_ROW_TILE = 512


def _invert_kernel(pos_ref, src_ref, *, n_tokens):
    def clear(p, carry):
        src_ref[p] = 0
        return carry

    lax.fori_loop(0, src_ref.shape[0], clear, 0, unroll=8)

    def place(t, carry):
        src_ref[pos_ref[t]] = t
        src_ref[pos_ref[n_tokens + t]] = t
        return carry

    lax.fori_loop(0, n_tokens, place, 0, unroll=8)


def invert_positions(pos, n_slots, n_tokens):
    return pl.pallas_call(
        functools.partial(_invert_kernel, n_tokens=n_tokens),
        in_specs=[pl.BlockSpec(memory_space=pltpu.SMEM)],
        out_specs=pl.BlockSpec(memory_space=pltpu.SMEM),
        out_shape=jax.ShapeDtypeStruct((n_slots,), jnp.int32),
        name="invert_positions",
    )(pos)


def _moe_up_kernel(src_ref, texp_ref, nv_ref, h_hbm, w1_ref, w3_ref, ws_ref, o_ref, buf, xs, inv, sem):
    i = pl.program_id(0)
    j = pl.program_id(1)
    nv = nv_ref[0]
    tm = xs.shape[0]
    slot = i % 2

    def row_copy(tile, r, s):
        tok = src_ref[tile * tm + r]
        return pltpu.make_async_copy(h_hbm.at[pl.ds(tok, 1), :], buf.at[s, pl.ds(r, 1), :], sem.at[s])

    def start_gather(tile, s):
        def body(r, carry):
            row_copy(tile, r, s).start()
            return carry
        lax.fori_loop(0, tm, body, 0, unroll=8)

    def wait_gather(tile, s):
        def body(r, carry):
            row_copy(tile, r, s).wait()
            return carry
        lax.fori_loop(0, tm, body, 0, unroll=8)

    @pl.when((j == 0) & (i < nv))
    def _():
        @pl.when(i == 0)
        def _():
            start_gather(0, 0)

        @pl.when(i + 1 < nv)
        def _():
            start_gather(i + 1, 1 - slot)

        wait_gather(i, slot)
        rows = buf[slot]
        s = _pow2_scale(jnp.max(jnp.abs(rows), axis=1, keepdims=True))
        xs[...] = (rows * s).astype(FP8)
        inv[...] = 1.0 / s

    @pl.when(i < nv)
    def _():
        x = xs[...]
        a = jnp.dot(x, w1_ref[0], preferred_element_type=F32) * (inv[...] / ws_ref[:, 0:1])
        b = jnp.dot(x, w3_ref[0], preferred_element_type=F32) * (inv[...] / ws_ref[:, 1:2])
        o_ref[...] = (_silu(a) * b).astype(o_ref.dtype)

    @pl.when(i >= nv)
    def _():
        o_ref[...] = jnp.zeros_like(o_ref)


MOE_UP_COLS = 256


def moe_up(h, w1, w3, w_scale, src, tile_expert, n_valid):
    d = h.shape[1]
    f = w1.shape[2]
    tm = MOE_ROW_TILE
    n_tiles = src.shape[0] // tm
    tn = _tile(f, MOE_UP_COLS)
    nj = f // tn

    def w_map(i, j, src, texp, nv):
        ie = jnp.minimum(i, nv[0] - 1)
        return (texp[ie], 0, jnp.where(i < nv[0], j, nj - 1))

    grid_spec = pltpu.PrefetchScalarGridSpec(
        num_scalar_prefetch=3,
        grid=(n_tiles, nj),
        in_specs=[pl.BlockSpec(memory_space=pl.ANY),
                  pl.BlockSpec((1, d, tn), w_map),
                  pl.BlockSpec((1, d, tn), w_map),
                  pl.BlockSpec((1, 2), lambda i, j, src, texp, nv: (0, 0))],
        out_specs=pl.BlockSpec((tm, tn), lambda i, j, src, texp, nv: (i, j)),
        scratch_shapes=[pltpu.VMEM((2, tm, d), F32), pltpu.VMEM((tm, d), FP8), pltpu.VMEM((tm, 1), F32),
                        pltpu.SemaphoreType.DMA((2,))],
    )
    return pl.pallas_call(
        _moe_up_kernel,
        grid_spec=grid_spec,
        out_shape=jax.ShapeDtypeStruct((n_tiles * tm, f), BF16),
        compiler_params=_cparams("arbitrary", "arbitrary"),
        name="moe_up",
    )(src, tile_expert, n_valid, h, w1, w3, w_scale)


def _moe_down_kernel(texp_ref, nv_ref, a_ref, w_ref, ws_ref, o_ref, a8_ref, inv_ref):
    valid = pl.program_id(0) < nv_ref[0]

    @pl.when(valid & (pl.program_id(1) == 0))
    def _():
        a = a_ref[...].astype(F32)
        s = _pow2_scale(jnp.max(jnp.abs(a), axis=1, keepdims=True))
        a8_ref[...] = (a * s).astype(FP8)
        inv_ref[...] = 1.0 / (s * ws_ref[...])

    @pl.when(valid)
    def _():
        o_ref[...] = jnp.dot(a8_ref[...], w_ref[0], preferred_element_type=F32) * inv_ref[...]

    @pl.when(jnp.logical_not(valid))
    def _():
        o_ref[...] = jnp.zeros_like(o_ref)


def moe_down(act, w2, w_scale, tile_expert, n_valid):
    m, f = act.shape
    d = w2.shape[2]
    tm = MOE_ROW_TILE
    tn = _tile(d, 1024)
    nj = d // tn

    def row(i, nv):
        return jnp.minimum(i, nv[0] - 1)

    def col(i, j, nv):
        return jnp.where(i < nv[0], j, nj - 1)

    grid_spec = pltpu.PrefetchScalarGridSpec(
        num_scalar_prefetch=2,
        grid=(m // tm, nj),
        in_specs=[pl.BlockSpec((tm, f), lambda i, j, texp, nv: (row(i, nv), 0)),
                  pl.BlockSpec((1, f, tn), lambda i, j, texp, nv: (texp[row(i, nv)], 0, col(i, j, nv))),
                  pl.BlockSpec((1, 1), lambda i, j, texp, nv: (0, 0))],
        out_specs=pl.BlockSpec((tm, tn), lambda i, j, texp, nv: (i, j)),
        scratch_shapes=[pltpu.VMEM((tm, f), FP8), pltpu.VMEM((tm, 1), F32)],
    )
    return pl.pallas_call(
        _moe_down_kernel,
        grid_spec=grid_spec,
        out_shape=jax.ShapeDtypeStruct((m, d), F32),
        compiler_params=_cparams("arbitrary", "arbitrary"),
        name="moe_down",
    )(tile_expert, n_valid, act, w2, w_scale)


def _moe_combine_kernel(pos_ref, y_hbm, x_ref, route_ref, mod_ref, fg_ref, o_ref, buf, sem, *,
                        n_tokens, n_experts, gate_row, final_norm):
    i = pl.program_id(0)
    tm = x_ref.shape[0]
    slot = i % 2

    def row_copy(tile, r, choice, s):
        p = pos_ref[choice * n_tokens + tile * tm + r]
        return pltpu.make_async_copy(y_hbm.at[pl.ds(p, 1), :], buf.at[s, choice, pl.ds(r, 1), :], sem.at[s])

    def start_gather(tile, s):
        def body(r, carry):
            row_copy(tile, r, 0, s).start()
            row_copy(tile, r, 1, s).start()
            return carry
        lax.fori_loop(0, tm, body, 0, unroll=8)

    def wait_gather(tile, s):
        def body(r, carry):
            row_copy(tile, r, 0, s).wait()
            row_copy(tile, r, 1, s).wait()
            return carry
        lax.fori_loop(0, tm, body, 0, unroll=8)

    @pl.when(i == 0)
    def _():
        start_gather(0, 0)

    @pl.when(i + 1 < pl.num_programs(0))
    def _():
        start_gather(i + 1, 1 - slot)

    wait_gather(i, slot)
    rec = route_ref[...]
    lane = lax.broadcasted_iota(jnp.int32, rec.shape, 1)
    w1 = jnp.sum(jnp.where(lane == n_experts, rec, 0.0), axis=1, keepdims=True)
    w2 = jnp.sum(jnp.where(lane == n_experts + 1, rec, 0.0), axis=1, keepdims=True)
    y = w1 * buf[slot, 0] + w2 * buf[slot, 1]
    out = x_ref[...] + mod_ref[0, gate_row:gate_row + 1, :] * y
    if final_norm:
        out = out * lax.rsqrt(jnp.mean(out * out, axis=-1, keepdims=True) + EPS) * fg_ref[...]
    o_ref[...] = out


def moe_combine(y, pos, x, route, mod, seq, n_experts, gate_row, final_gain=None):
    t, d = x.shape
    tm = _tile(seq, 256)
    per_seq = seq // tm
    final_norm = final_gain is not None
    fg = (final_gain if final_norm else jnp.ones((d,), F32)).reshape(1, d)
    grid_spec = pltpu.PrefetchScalarGridSpec(
        num_scalar_prefetch=1,
        grid=(t // tm,),
        in_specs=[pl.BlockSpec(memory_space=pl.ANY),
                  pl.BlockSpec((tm, d), lambda i, pos: (i, 0)),
                  pl.BlockSpec((tm, LANES), lambda i, pos: (i, 0)),
                  pl.BlockSpec((1, 6, d), lambda i, pos: (i // per_seq, 0, 0)),
                  pl.BlockSpec((1, d), lambda i, pos: (0, 0))],
        out_specs=pl.BlockSpec((tm, d), lambda i, pos: (i, 0)),
        scratch_shapes=[pltpu.VMEM((2, TOP_K, tm, d), F32), pltpu.SemaphoreType.DMA((2,))],
    )
    return pl.pallas_call(
        functools.partial(_moe_combine_kernel, n_tokens=t, n_experts=n_experts, gate_row=gate_row,
                          final_norm=final_norm),
        grid_spec=grid_spec,
        out_shape=jax.ShapeDtypeStruct((t, d), F32),
        compiler_params=_cparams("arbitrary"),
        name="moe_combine",
    )(pos, y, x, route, mod, fg)


def moe_ffn(x, gain, mod, seq, router_w, router_b, w1, w3, w2, gate_row, final_gain=None):
    t, d = x.shape
    n_experts = router_w.shape[1]
    tg = MOE_ROW_TILE
    w_r = _pad_cols(router_w, LANES)
    b_r = _pad_cols(router_b.reshape(1, n_experts), LANES)
    h, route = norm_router(x, gain, mod, seq, w_r, b_r, n_experts)
    cum = row_cumsum(route, 1, t)
    chosen = route[:, n_experts + 2:n_experts + 4].astype(jnp.int32)
    rank = jnp.take_along_axis(cum[:, :n_experts] - route[:, :n_experts], chosen, axis=1).astype(jnp.int32)
    counts = cum[t - 1, :n_experts].astype(jnp.int32)
    padded = (counts + tg - 1) // tg * tg
    ends = jnp.cumsum(padded)
    pos = (jnp.take(ends - padded, chosen) + rank).T.reshape(-1)
    n_tiles = (TOP_K * t + n_experts * (tg - 1)) // tg
    n_valid = (ends[n_experts - 1] // tg).reshape(1)
    tile_expert = jnp.minimum(jnp.searchsorted(ends, jnp.arange(n_tiles, dtype=jnp.int32) * tg, side="right"),
                              n_experts - 1).astype(jnp.int32)
    src = invert_positions(pos, n_tiles * tg, t)
    w1_8, s1 = fp8_weight(w1)
    w3_8, s3 = fp8_weight(w3)
    act = moe_up(h, w1_8, w3_8, jnp.stack([s1, s3]).reshape(1, 2), src, tile_expert, n_valid)
    w2_8, s2 = fp8_weight(w2)
    y = moe_down(act, w2_8, s2.reshape(1, 1), tile_expert, n_valid)
    return moe_combine(y, pos, x, route, mod, seq, n_experts, gate_row, final_gain)


def _merge_kernel(ya_ref, yb_ref, yc_ref, wa_ref, wb_ref, wc_ref, ga_ref, gb_ref, gc_ref, o_ref):
    out = ga_ref[...].astype(F32) * jnp.dot(ya_ref[...], wa_ref[0], preferred_element_type=F32)
    out += gb_ref[...].astype(F32) * jnp.dot(yb_ref[...], wb_ref[0], preferred_element_type=F32)
    out += gc_ref[...].astype(F32) * jnp.dot(yc_ref[...], wc_ref[0], preferred_element_type=F32)
    o_ref[...] = out.astype(o_ref.dtype)


def merge_branches(y_a, y_b, y_c, w_branch, gates, seq):
    m, k = y_a.shape
    n = w_branch.shape[2]
    tm = _tile(seq, 1024)
    tn = _tile(n, 1024)
    nj = n // tn
    y_spec = pl.BlockSpec((tm, k), lambda i, j: (i, 0))

    def w_spec(b):
        return pl.BlockSpec((1, k, tn), lambda i, j: (b, 0, j))

    def g_spec(b):
        return pl.BlockSpec((tm, tn), lambda i, j: (i, b * nj + j))

    return pl.pallas_call(
        _merge_kernel,
        grid=(m // tm, nj),
        in_specs=[y_spec, y_spec, y_spec, w_spec(0), w_spec(1), w_spec(2), g_spec(0), g_spec(1), g_spec(2)],
        out_specs=pl.BlockSpec((tm, tn), lambda i, j: (i, j)),
        out_shape=jax.ShapeDtypeStruct((m, n), BF16),
        compiler_params=_cparams("parallel", "parallel"),
        name="merge_branches",
    )(y_a, y_b, y_c, w_branch, w_branch, w_branch, gates, gates, gates)


def _conv_kernel(h_ref, c_ref, b_ref, hp_ref, cp_ref, w_ref, bias_ref, o_ref, *, blocks_per_seq):
    u = c_ref[...].astype(F32) * h_ref[...].astype(F32)
    tm = u.shape[0]
    halo = hp_ref.shape[0]
    up = cp_ref[...].astype(F32) * hp_ref[...].astype(F32)
    first = (pl.program_id(0) % blocks_per_seq) == 0
    up = jnp.where(first, 0.0, up)
    p1 = up[halo - 1:halo, :]
    p2 = up[halo - 2:halo - 1, :]
    row = lax.broadcasted_iota(jnp.int32, (tm, 1), 0)
    u1 = jnp.where(row == 0, p1, pltpu.roll(u, 1, 0))
    u2 = jnp.where(row == 0, p2, jnp.where(row == 1, p1, pltpu.roll(u, 2, 0)))
    y = w_ref[0:1, :] * u2 + w_ref[1:2, :] * u1 + w_ref[2:3, :] * u + bias_ref[...]
    o_ref[...] = (b_ref[...].astype(F32) * y).astype(o_ref.dtype)


def short_conv(p, col0, conv_w, conv_b, seq):
    t = p.shape[0]
    w = conv_w.shape[1]
    tm = _tile(seq, 512)
    halo = 16
    ratio = tm // halo
    prev = lambda i: jnp.maximum(i * ratio - 1, 0)
    c0 = col0 // w
    return pl.pallas_call(
        functools.partial(_conv_kernel, blocks_per_seq=seq // tm),
        grid=(t // tm,),
        in_specs=[
            pl.BlockSpec((tm, w), lambda i: (i, c0)),
            pl.BlockSpec((tm, w), lambda i: (i, c0 + 1)),
            pl.BlockSpec((tm, w), lambda i: (i, c0 + 2)),
            pl.BlockSpec((halo, w), lambda i: (prev(i), c0)),
            pl.BlockSpec((halo, w), lambda i: (prev(i), c0 + 1)),
            pl.BlockSpec((3, w), lambda i: (0, 0)),
            pl.BlockSpec((1, w), lambda i: (0, 0)),
        ],
        out_specs=pl.BlockSpec((tm, w), lambda i: (i, 0)),
        out_shape=jax.ShapeDtypeStruct((t, w), BF16),
        compiler_params=_cparams("parallel"),
        name="short_conv",
    )(p, p, p, p, p, conv_w, conv_b.reshape(1, w))


def _block_cumsum(x_ref, cum_ref, carry_ref):
    @pl.when(pl.program_id(1) == 0)
    def _():
        carry_ref[...] = jnp.zeros_like(carry_ref)

    x = x_ref[...]
    n = x.shape[0]
    r = lax.broadcasted_iota(jnp.int32, (n, n), 0)
    c = lax.broadcasted_iota(jnp.int32, (n, n), 1)
    tri = (r >= c).astype(BF16)
    x1 = x.astype(BF16)
    r1 = x - x1.astype(F32)
    x2 = r1.astype(BF16)
    x3 = (r1 - x2.astype(F32)).astype(BF16)
    cum = (jnp.dot(tri, x1, preferred_element_type=F32) + jnp.dot(tri, x2, preferred_element_type=F32)
           + jnp.dot(tri, x3, preferred_element_type=F32)) + carry_ref[...]
    cum_ref[...] = cum
    carry_ref[...] = cum[n - 1:n, :]
    return cum


def _cumsum_kernel(x_ref, cum_ref, carry_ref):
    _block_cumsum(x_ref, cum_ref, carry_ref)


def row_cumsum(x, n_seq, seq):
    blk = _tile(seq, 256)
    nb = seq // blk
    return pl.pallas_call(
        _cumsum_kernel,
        grid=(n_seq, nb),
        in_specs=[pl.BlockSpec((blk, LANES), lambda b, i: (b * nb + i, 0))],
        out_specs=pl.BlockSpec((blk, LANES), lambda b, i: (b * nb + i, 0)),
        out_shape=jax.ShapeDtypeStruct((n_seq * seq, LANES), F32),
        scratch_shapes=[pltpu.VMEM((1, LANES), F32)],
        compiler_params=_cparams("parallel", "arbitrary"),
        name="row_cumsum",
    )(x)


FOX_GROUP = 2
LOG2_E = 1.4426950408889634
FOX_SUM_ROWS = 16


def _fox_bias_columns(cum_rows, head, value_lane, ones_lane, ones):
    lane = lax.broadcasted_iota(jnp.int32, cum_rows.shape, 1)
    c = jnp.sum(jnp.where(lane == head, cum_rows, 0.0), axis=1, keepdims=True) * LOG2_E
    hi = c.astype(BF16).astype(F32)
    r1 = c - hi
    mid = r1.astype(BF16).astype(F32)
    lo = r1 - mid
    out = jnp.where((lane >= ones_lane) & (lane < ones_lane + 3), ones, 0.0)
    out = jnp.where(lane == value_lane, hi, out)
    out = jnp.where(lane == value_lane + 1, mid, out)
    out = jnp.where(lane == value_lane + 2, lo, out)
    return out.astype(BF16)


def _fox_kernel(q_ref, k_ref, v_ref, cq_ref, ck_ref, o_ref, kaug_ref, vt_ref, qaug_ref, s_ref, m_ref, acc_ref):
    hp = pl.program_id(1)
    i = pl.program_id(2)
    tq = q_ref.shape[0]
    tk = tq
    seq = k_ref.shape[0]

    @pl.when(i == 0)
    def _():
        ones_row = lax.broadcasted_iota(jnp.int32, (FOX_SUM_ROWS, tk), 0) == 0

        def fill(c, carry):
            rows = pl.ds(pl.multiple_of(c * tk, tk), tk)
            for g in range(FOX_GROUP):
                cols = slice(g * HEAD_DIM, (g + 1) * HEAD_DIM)
                kaug_ref[g, rows, 0:HEAD_DIM] = k_ref[rows, cols]
                kaug_ref[g, rows, HEAD_DIM:2 * HEAD_DIM] = _fox_bias_columns(
                    ck_ref[rows, :], hp * FOX_GROUP + g, 0, 3, 1.0)
                vt_ref[g, 0:HEAD_DIM, rows] = v_ref[rows, cols].astype(F32).T.astype(BF16)
                vt_ref[g, HEAD_DIM:, rows] = jnp.where(ones_row, 1.0, 0.0).astype(BF16)
            return carry

        lax.fori_loop(0, seq // tk, fill, 0)

    for g in range(FOX_GROUP):
        cols = slice(g * HEAD_DIM, (g + 1) * HEAD_DIM)
        qaug_ref[g, :, 0:HEAD_DIM] = q_ref[:, cols]
        qaug_ref[g, :, HEAD_DIM:2 * HEAD_DIM] = _fox_bias_columns(cq_ref[...], hp * FOX_GROUP + g, 3, 0, -1.0)
    m_ref[...] = jnp.full_like(m_ref, NEG_INF)
    acc_ref[...] = jnp.zeros_like(acc_ref)
    key = lax.broadcasted_iota(jnp.int32, (tk, tq), 0)
    qry = lax.broadcasted_iota(jnp.int32, (tk, tq), 1)

    def score(j, slot):
        start = pl.multiple_of(j * tk, tk)
        for g in range(FOX_GROUP):
            s_ref[slot, g] = lax.dot_general(kaug_ref[g, pl.ds(start, tk), :], qaug_ref[g],
                                             (((1,), (1,)), ((), ())), preferred_element_type=F32)

    def consume(j, slot, on_diagonal):
        start = pl.multiple_of(j * tk, tk)
        for g in range(FOX_GROUP):
            st = s_ref[slot, g]
            if on_diagonal:
                st = jnp.where(key <= qry, st, NEG_INF)
            m_old = m_ref[g]
            m_new = jnp.maximum(m_old, jnp.max(st, axis=0, keepdims=True))
            alpha = jnp.exp2(m_old - m_new)
            pt = jnp.exp2(st - m_new)
            acc_ref[g] = alpha * acc_ref[g] + jnp.dot(vt_ref[g, :, pl.ds(start, tk)], pt.astype(BF16),
                                                      preferred_element_type=F32)
            m_ref[g] = m_new

    def two_below_diagonal(p, carry):
        j = 2 * p
        score(j + 1, 1)
        consume(j, 0, False)
        score(j + 2, 0)
        consume(j + 1, 1, False)
        return carry

    score(0, 0)
    lax.fori_loop(0, i // 2, two_below_diagonal, 0)

    @pl.when(i % 2 == 0)
    def _():
        consume(i, 0, True)

    @pl.when(i % 2 == 1)
    def _():
        score(i, 1)
        consume(i - 1, 0, False)
        consume(i, 1, True)

    for g in range(FOX_GROUP):
        out = acc_ref[g, 0:HEAD_DIM, :] / acc_ref[g, HEAD_DIM:HEAD_DIM + 1, :]
        o_ref[:, g * HEAD_DIM:(g + 1) * HEAD_DIM] = out.T.astype(o_ref.dtype)


def fox_attention(p, col0, cum, batch, seq):
    tq = _tile(seq, 512)
    nq = seq // tq
    gw = FOX_GROUP * HEAD_DIM
    n_groups = N_HEADS // FOX_GROUP
    q0 = col0 // gw
    return pl.pallas_call(
        _fox_kernel,
        grid=(batch, n_groups, nq),
        in_specs=[
            pl.BlockSpec((tq, gw), lambda b, h, i: (b * nq + i, q0 + h)),
            pl.BlockSpec((seq, gw), lambda b, h, i: (b, q0 + n_groups + h)),
            pl.BlockSpec((seq, gw), lambda b, h, i: (b, q0 + 2 * n_groups + h)),
            pl.BlockSpec((tq, LANES), lambda b, h, i: (b * nq + i, 0)),
            pl.BlockSpec((seq, LANES), lambda b, h, i: (b, 0)),
        ],
        out_specs=pl.BlockSpec((tq, gw), lambda b, h, i: (b * nq + i, h)),
        out_shape=jax.ShapeDtypeStruct((batch * seq, WIDTH), BF16),
        scratch_shapes=[pltpu.VMEM((FOX_GROUP, seq, 2 * HEAD_DIM), BF16),
                        pltpu.VMEM((FOX_GROUP, HEAD_DIM + FOX_SUM_ROWS, seq), BF16),
                        pltpu.VMEM((FOX_GROUP, tq, 2 * HEAD_DIM), BF16),
                        pltpu.VMEM((2, FOX_GROUP, tq, tq), F32),
                        pltpu.VMEM((FOX_GROUP, 1, tq), F32),
                        pltpu.VMEM((FOX_GROUP, HEAD_DIM + FOX_SUM_ROWS, tq), F32)],
        compiler_params=_cparams("parallel", "parallel", "arbitrary"),
        name="fox_attention",
    )(p, p, p, cum, cum)


def _hgrn_levels():
    sizes = []
    half = HGRN_CHUNK // 2
    while half >= HGRN_DIAG:
        sizes.append(half)
        half //= 2
    return sizes


def _hgrn_coefficients():
    c = HGRN_CHUNK
    t = np.arange(c)[:, None]
    u = np.arange(c)[None, :]
    slabs = [(u <= t), (u > t)]
    for size in _hgrn_levels():
        ref = (t // (2 * size)) * (2 * size) + size - 1
        upper = (t % (2 * size)) >= size
        slabs.append(np.where(upper, (u > ref) & (u <= t), (u > t) & (u <= ref)))
    slabs.append((u <= t) & (u // HGRN_DIAG == t // HGRN_DIAG))
    coef = np.concatenate(slabs, axis=0).astype(np.float32)
    return np.concatenate([coef, coef], axis=1)


def _hgrn_level_masks():
    c = HGRN_CHUNK
    t = np.arange(c)[:, None]
    s = np.arange(c)[None, :]
    masks = []
    for size in _hgrn_levels():
        same = (t // (2 * size)) == (s // (2 * size))
        masks.append(same & ((t % (2 * size)) >= size) & ((s % (2 * size)) < size))
    masks.append((t // HGRN_DIAG == s // HGRN_DIAG) & (s <= t))
    return np.stack(masks).astype(np.float32)


def _hgrn_placement():
    place = np.zeros((HGRN_DIAG, HEAD_DIM, HGRN_CHUNK), np.float32)
    for j in range(HGRN_DIAG):
        place[j, :, j::HGRN_DIAG] = 1.0
    return place.reshape(HGRN_DIAG * HEAD_DIM, HGRN_CHUNK)


def _hgrn_kernel(q_ref, f_ref, i_ref, g_ref, lb_ref, norm_ref, coef_ref, mask_ref, place_ref, o_ref,
                 st_ref, e_ref, kf_ref, *, layer):
    c = HGRN_CHUNK
    blk_rows = HGRN_DIAG
    n_lev = len(_hgrn_levels())
    b_in_rows = (2 + n_lev) * c

    @pl.when(pl.program_id(1) == 0)
    def _():
        st_ref[...] = jnp.zeros_like(st_ref)

    lbr = lb_ref[...]
    le = jnp.exp(lbr - jnp.max(lbr, axis=0, keepdims=True))
    lp = le / jnp.sum(le, axis=0, keepdims=True)
    lb = jnp.zeros((1, WIDTH), F32)
    for r in range(1, layer + 1):
        lb = lb + lp[r:r + 1, :]

    f = lb + (1.0 - lb) * _sigmoid(f_ref[...].astype(F32))
    g = jnp.log(f)
    kf_ref[...] = 1.0 - f
    g_hi, g_lo = _split_bf16(g)
    e_ref[...] = jnp.dot(coef_ref[...], jnp.concatenate([g_hi, g_lo], axis=0), preferred_element_type=F32)

    def head(h, carry):
        lanes = pl.ds(pl.multiple_of(h * HEAD_DIM, HEAD_DIM), HEAD_DIM)
        q = q_ref[:, lanes].astype(F32)
        k = kf_ref[:, lanes]
        v = i_ref[:, lanes]
        st = st_ref[h]
        qe = (q * jnp.exp(e_ref[0:c, lanes])).astype(BF16)
        out = lax.dot_general(qe, st.astype(BF16), (((1,), (1,)), ((), ())), preferred_element_type=F32)
        attn = jnp.zeros((c, c), F32)
        for lev in range(n_lev):
            pw = jnp.exp(e_ref[(2 + lev) * c:(3 + lev) * c, lanes])
            a = lax.dot_general((q * pw).astype(BF16), (k * pw).astype(BF16), (((1,), (1,)), ((), ())),
                                preferred_element_type=F32)
            attn = attn + jnp.where(mask_ref[lev] > 0.0, a, 0.0)
        b_in = e_ref[b_in_rows:b_in_rows + c, lanes]

        def block_row(ref, base, j):
            rows = [jnp.broadcast_to(ref[pl.ds(base + blk * blk_rows + j, 1), lanes], (blk_rows, HEAD_DIM))
                    for blk in range(c // blk_rows)]
            return jnp.concatenate(rows, axis=0)

        z = []
        for j in range(blk_rows):
            k_j = block_row(kf_ref, 0, j)
            b_j = block_row(e_ref, b_in_rows, j)
            z.append((q * k_j * jnp.exp(jnp.minimum(b_in - b_j, 0.0))).astype(BF16))
        diag = jnp.dot(jnp.concatenate(z, axis=1), place_ref[...], preferred_element_type=F32)
        attn = attn + jnp.where(mask_ref[n_lev] > 0.0, diag, 0.0)
        out = out + jnp.dot(attn.astype(BF16), v, preferred_element_type=F32)
        ke = (k * jnp.exp(e_ref[c:2 * c, lanes])).astype(BF16)
        decay = jnp.exp(e_ref[c - 1:c, lanes])
        st_ref[h] = st * decay + lax.dot_general(v, ke, (((0,), (0,)), ((), ())), preferred_element_type=F32)
        out = out * lax.rsqrt(jnp.mean(out * out, axis=-1, keepdims=True) + EPS)
        out = out * norm_ref[:, lanes] * _silu(g_ref[:, lanes].astype(F32))
        o_ref[:, lanes] = out.astype(o_ref.dtype)
        return carry

    lax.fori_loop(0, N_HEADS, head, 0, unroll=8)


def hgrn_mixer(p, col0, lower_bounds, norm, layer, batch, seq):
    c = HGRN_CHUNK
    nc = seq // c
    depth = lower_bounds.shape[0]
    coef = jnp.asarray(_hgrn_coefficients(), BF16)
    masks = jnp.asarray(_hgrn_level_masks(), F32)
    place = jnp.asarray(_hgrn_placement(), BF16)
    n_slab = coef.shape[0] // c

    def col_spec(j):
        return pl.BlockSpec((c, WIDTH), lambda b, n: (b * nc + n, col0 // WIDTH + j))

    return pl.pallas_call(
        functools.partial(_hgrn_kernel, layer=layer),
        grid=(batch, nc),
        in_specs=[col_spec(0), col_spec(1), col_spec(2), col_spec(3),
                  pl.BlockSpec((depth, WIDTH), lambda b, n: (0, 0)),
                  pl.BlockSpec((1, WIDTH), lambda b, n: (0, 0)),
                  pl.BlockSpec(coef.shape, lambda b, n: (0, 0)),
                  pl.BlockSpec(masks.shape, lambda b, n: (0, 0, 0)),
                  pl.BlockSpec(place.shape, lambda b, n: (0, 0))],
        out_specs=pl.BlockSpec((c, WIDTH), lambda b, n: (b * nc + n, 0)),
        out_shape=jax.ShapeDtypeStruct((batch * seq, WIDTH), BF16),
        scratch_shapes=[pltpu.VMEM((N_HEADS, HEAD_DIM, HEAD_DIM), F32),
                        pltpu.VMEM((n_slab * c, WIDTH), F32),
                        pltpu.VMEM((c, WIDTH), F32)],
        compiler_params=_cparams("parallel", "arbitrary"),
        name="hgrn_mixer",
    )(p, p, p, p, lower_bounds, norm.reshape(1, WIDTH), coef, masks, place)


def _pad_cols(w, n):
    return jnp.pad(w, ((0, 0), (0, n - w.shape[1])))


def _round_up(n, m):
    return ((n + m - 1) // m) * m


def kernel(x, c, norm_mix, norm_ffn, w_ada, b_ada, w_in, b_gate, fox_b_f, hgrn_lower_bounds, hgrn_norm,
           conv_w, conv_b, w_branch, w_o, ffn_w1, ffn_w3, ffn_w2, router_w, router_b,
           expert_w1, expert_w3, expert_w2, norm_final):
    batch, seq, d = x.shape
    depth = w_ada.shape[0]
    t = batch * seq

    o_fox = 0
    o_ff = 3 * WIDTH
    o_hgrn = o_ff + N_HEADS
    o_conv = o_hgrn + 4 * WIDTH
    o_gate = o_conv + 3 * WIDTH

    mod_all = ada_modulation(c, w_ada, b_ada)
    xt = x.reshape(t, d)
    for layer in range(depth):
        mod = mod_all[layer]
        wl = w_in[layer]
        w_mix = jnp.concatenate([wl[:, o_fox:o_ff], wl[:, o_hgrn:o_gate]], axis=1).astype(BF16)
        c_hgrn = 3 * WIDTH
        c_conv = c_hgrn + 4 * WIDTH
        col_scale = jnp.ones((w_mix.shape[1],), F32).at[:WIDTH].set(HEAD_DIM ** -0.5 * LOG2_E)
        w_ff = _pad_cols(wl[:, o_ff:o_hgrn], LANES).astype(BF16)
        w_gate = wl[:, o_gate:].astype(BF16)
        b_ff = _pad_cols(fox_b_f[layer].reshape(1, N_HEADS), LANES)

        h, lsf = norm_forget(xt, norm_mix[layer], mod, seq, w_ff, b_ff)
        p = project(h, w_mix, col_scale, seq)
        gates = gate_project(h, w_gate, b_gate[layer], seq)

        y_a = hgrn_mixer(p, c_hgrn, hgrn_lower_bounds, hgrn_norm[layer], layer, batch, seq)
        y_b = short_conv(p, c_conv, conv_w[layer], conv_b[layer], seq)
        cum = row_cumsum(lsf, batch, seq)
        y_c = fox_attention(p, 0, cum, batch, seq)

        merged = merge_branches(y_a, y_b, y_c, w_branch[layer].astype(BF16), gates, seq)
        xt = residual_project(merged, w_o[layer].astype(BF16), xt, mod, seq, 2, d // 2)

        i = layer // 2
        if layer % 2 == 0:
            dff = ffn_w1.shape[2]
            dff_pad = _round_up(dff, 1024)
            w1 = _pad_cols(ffn_w1[i], dff_pad).astype(BF16)
            w3 = _pad_cols(ffn_w3[i], dff_pad).astype(BF16)
            w2 = jnp.pad(ffn_w2[i], ((0, dff_pad - dff), (0, 0))).astype(BF16)
            h2 = norm_only(xt, norm_ffn[layer], mod, seq)
            act = glu(h2, w1, w3, seq)
            xt = residual_project(act, w2, xt, mod, seq, 5, dff_pad // 4)
        else:
            last = layer == depth - 1
            xt = moe_ffn(xt, norm_ffn[layer], mod, seq, router_w[i], router_b[i],
                         expert_w1[i], expert_w3[i], expert_w2[i], 5,
                         final_gain=norm_final if last else None)
    if depth % 2:
        xt = final_norm(xt, norm_final)
    return xt.reshape(batch, seq, d)
```
